```python
import math
import jax, jax.numpy as jnp
from jax import lax
import numpy as np

D_MODEL = 2048
BATCH = 4
SEQ = 2048
DEPTH = 2

N_BRANCH = 4
W_BRANCH = D_MODEL // 4
W_SSM = W_BRANCH
W_POOL = W_BRANCH
W_CONV = W_BRANCH
W_GMLP = W_BRANCH
SSM_GROUP = 16
SSM_GROUPS = W_SSM // SSM_GROUP
SSM_STATE = 64
POOL_WINDOWS = (2, 4, 8, 16)
POOL_GROUPS = len(POOL_WINDOWS)
POOL_GW = W_POOL // POOL_GROUPS
CONV_WIDTH = 31
GMLP_CHUNK = 128
GMLP_HEADS = 4
GMLP_HD = W_GMLP // GMLP_HEADS
D_FF = 4 * D_MODEL
EPS = 1e-6

OFF_SSM = 0
OFF_POOL = OFF_SSM + W_SSM
OFF_CONV = OFF_POOL + W_POOL
OFF_GMLP = OFF_CONV + 2 * W_CONV
OFF_GATE = OFF_GMLP + 2 * W_GMLP
IN_COLS = OFF_GATE + N_BRANCH * D_MODEL

kernel_name = 'hybrid_s5_pool_conv_gmlp_block'


def rms_norm(x, g):
    xf = x.astype(jnp.float32)
    y = xf * lax.rsqrt(jnp.mean(xf * xf, axis=-1, keepdims=True) + EPS)
    return (y * g.astype(jnp.float32)).astype(x.dtype)


def layer_norm(x, g, b):
    xf = x.astype(jnp.float32)
    mu = jnp.mean(xf, axis=-1, keepdims=True)
    xc = xf - mu
    var = jnp.mean(xc * xc, axis=-1, keepdims=True)
    y = xc * lax.rsqrt(var + EPS)
    return (y * g.astype(jnp.float32) + b.astype(jnp.float32)).astype(x.dtype)


def s5_mixer(u, a_re, a_im, log_dt, b_re, b_im, c_re, c_im, d_skip, w_glu):
    nb, L, _ = u.shape
    uf = u.astype(jnp.float32)
    ug = uf.reshape(nb, L, SSM_GROUPS, SSM_GROUP)
    dt = jnp.exp(log_dt.astype(jnp.float32))[:, None]
    ar = a_re.astype(jnp.float32)
    ai = a_im.astype(jnp.float32)
    mag = jnp.exp(ar * dt)
    ang = ai * dt
    lr = mag * jnp.cos(ang)
    li = mag * jnp.sin(ang)
    den = ar * ar + ai * ai
    fr = ((lr - 1.0) * ar + li * ai) / den
    fi = (li * ar - (lr - 1.0) * ai) / den
    br = b_re.astype(jnp.float32)
    bi = b_im.astype(jnp.float32)
    bbr = fr[:, :, None] * br - fi[:, :, None] * bi
    bbi = fr[:, :, None] * bi + fi[:, :, None] * br
    bur = jnp.einsum('gnp,blgp->blgn', bbr, ug)
    bui = jnp.einsum('gnp,blgp->blgn', bbi, ug)
    lam_r = jnp.broadcast_to(lr, bur.shape)
    lam_i = jnp.broadcast_to(li, bur.shape)

    def combine(e1, e2):
        a1r, a1i, b1r, b1i = e1
        a2r, a2i, b2r, b2i = e2
        return (a1r * a2r - a1i * a2i,
                a1r * a2i + a1i * a2r,
                a2r * b1r - a2i * b1i + b2r,
                a2r * b1i + a2i * b1r + b2i)

    _, _, sr, si = lax.associative_scan(combine, (lam_r, lam_i, bur, bui), axis=1)
    y = (jnp.einsum('gpn,blgn->blgp', c_re.astype(jnp.float32), sr)
         - jnp.einsum('gpn,blgn->blgp', c_im.astype(jnp.float32), si))
    y = y.reshape(nb, L, W_SSM) + d_skip.astype(jnp.float32) * uf
    z = jax.nn.gelu(y)
    zg = z @ w_glu.astype(jnp.float32)
    out = zg[..., :W_SSM] * jax.nn.sigmoid(zg[..., W_SSM:])
    return out.astype(u.dtype)


def pool_mixer(u, w_pool, pool_scale):
    nb, L, _ = u.shape
    uf = u.astype(jnp.float32)
    cs = jnp.pad(jnp.cumsum(uf, axis=1), ((0, 0), (1, 0), (0, 0)))
    t = jnp.arange(L)
    outs = []
    for gi, w in enumerate(POOL_WINDOWS):
        c = cs[:, :, gi * POOL_GW:(gi + 1) * POOL_GW]
        upper = c[:, 1:]
        lower = jnp.pad(c[:, :L + 1 - w], ((0, 0), (w - 1, 0), (0, 0)))
        count = jnp.minimum(t + 1, w).astype(jnp.float32)[None, :, None]
        mean = (upper - lower) / count
        outs.append(mean - uf[:, :, gi * POOL_GW:(gi + 1) * POOL_GW])
    pooled = jnp.stack(outs, axis=2)
    mixed = jnp.einsum('blgc,gcd->blgd', pooled, w_pool.astype(jnp.float32))
    out = mixed.reshape(nb, L, W_POOL) * pool_scale.astype(jnp.float32)
    return out.astype(u.dtype)


def conv_mixer(val, gate, w_dw, b_dw, ln_g, ln_b):
    v = val * jax.nn.sigmoid(gate)
    kern = w_dw.astype(v.dtype).reshape(CONV_WIDTH, 1, W_CONV)
    y = lax.conv_general_dilated(v, kern, window_strides=(1,), padding=[(CONV_WIDTH - 1, 0)],
                                 dimension_numbers=('NWC', 'WIO', 'NWC'),
                                 feature_group_count=W_CONV)
    y = y + b_dw.astype(y.dtype)
    y = layer_norm(y, ln_g, ln_b)
    return jax.nn.silu(y)


def gmlp_mixer(u, v, ln_g, ln_b, w_s, b_s):
    nb, L, _ = u.shape
    u = jax.nn.gelu(u)
    v = layer_norm(jax.nn.gelu(v), ln_g, ln_b)
    nc = L // GMLP_CHUNK
    vc = v.reshape(nb, nc, GMLP_CHUNK, GMLP_HEADS, GMLP_HD)
    mask = jnp.tril(jnp.ones((GMLP_CHUNK, GMLP_CHUNK), dtype=bool))
    ws = jnp.where(mask[None], w_s, jnp.zeros_like(w_s))
    sv = jnp.einsum('hts,bcshd->bcthd', ws, vc) + jnp.transpose(b_s)[None, None, :, :, None]
    return u * sv.reshape(nb, L, W_GMLP).astype(u.dtype)


def setup_inputs(seed: int = 0) -> dict:
    key = jax.random.key(seed)
    ks = iter(jax.random.split(key, 40))

    def nrm(shape, scale):
        return jax.random.normal(next(ks), shape, jnp.float32) * scale

    def gain(shape):
        return 1.0 + nrm(shape, 0.02)

    G, N, P = SSM_GROUPS, SSM_STATE, SSM_GROUP
    x = jax.random.normal(next(ks), (BATCH, SEQ, D_MODEL), jnp.float32)
    a_re = -0.5 + nrm((DEPTH, G, N), 0.01)
    a_im = jnp.pi * jnp.arange(N, dtype=jnp.float32)[None, None, :] + nrm((DEPTH, G, N), 0.01)
    log_dt = jax.random.uniform(next(ks), (DEPTH, G), jnp.float32,
                                minval=math.log(1e-3), maxval=math.log(1e-1))
    return {
        'x': x,
        'g_pre_mix': gain((DEPTH, D_MODEL)),
        'w_in': nrm((DEPTH, D_MODEL, IN_COLS), D_MODEL ** -0.5),
        'ssm_a_re': a_re,
        'ssm_a_im': a_im,
        'ssm_log_dt': log_dt,
        'ssm_b_re': nrm((DEPTH, G, N, P), (2 * P) ** -0.5),
        'ssm_b_im': nrm((DEPTH, G, N, P), (2 * P) ** -0.5),
        'ssm_c_re': nrm((DEPTH, G, P, N), (2 * N) ** -0.5),
        'ssm_c_im': nrm((DEPTH, G, P, N), (2 * N) ** -0.5),
        'ssm_d': nrm((DEPTH, W_SSM), 1.0),
        'ssm_w_glu': nrm((DEPTH, W_SSM, 2 * W_SSM), W_SSM ** -0.5),
        'pool_w': nrm((DEPTH, POOL_GROUPS, POOL_GW, POOL_GW), POOL_GW ** -0.5),
        'pool_scale': gain((DEPTH, W_POOL)),
        'conv_w': nrm((DEPTH, CONV_WIDTH, W_CONV), CONV_WIDTH ** -0.5),
        'conv_b': nrm((DEPTH, W_CONV), 0.02),
        'conv_ln_g': gain((DEPTH, W_CONV)),
        'conv_ln_b': nrm((DEPTH, W_CONV), 0.02),
        'gmlp_ln_g': gain((DEPTH, W_GMLP)),
        'gmlp_ln_b': nrm((DEPTH, W_GMLP), 0.02),
        'gmlp_ws': nrm((DEPTH, GMLP_HEADS, GMLP_CHUNK, GMLP_CHUNK), GMLP_CHUNK ** -0.5),
        'gmlp_bs': gain((DEPTH, GMLP_HEADS, GMLP_CHUNK)),
        'w_branch': nrm((DEPTH, N_BRANCH, W_BRANCH, D_MODEL), W_BRANCH ** -0.5),
        'w_o': nrm((DEPTH, D_MODEL, D_MODEL), D_MODEL ** -0.5),
        'g_post_mix': gain((DEPTH, D_MODEL)),
        'g_pre_mlp': gain((DEPTH, D_MODEL)),
        'w_ff1': nrm((DEPTH, D_MODEL, D_FF), D_MODEL ** -0.5),
        'w_ff2': nrm((DEPTH, D_FF, D_MODEL), D_FF ** -0.5),
        'g_post_mlp': gain((DEPTH, D_MODEL)),
    }


def reference(x, g_pre_mix, w_in, ssm_a_re, ssm_a_im, ssm_log_dt, ssm_b_re, ssm_b_im,
              ssm_c_re, ssm_c_im, ssm_d, ssm_w_glu, pool_w, pool_scale, conv_w, conv_b,
              conv_ln_g, conv_ln_b, gmlp_ln_g, gmlp_ln_b, gmlp_ws, gmlp_bs, w_branch, w_o,
              g_post_mix, g_pre_mlp, w_ff1, w_ff2, g_post_mlp):
    nb, L, _ = x.shape
    for l in range(DEPTH):
        h = rms_norm(x, g_pre_mix[l])
        proj = h @ w_in[l]
        y_ssm = s5_mixer(proj[..., OFF_SSM:OFF_SSM + W_SSM], ssm_a_re[l], ssm_a_im[l],
                         ssm_log_dt[l], ssm_b_re[l], ssm_b_im[l], ssm_c_re[l], ssm_c_im[l],
                         ssm_d[l], ssm_w_glu[l])
        y_pool = pool_mixer(proj[..., OFF_POOL:OFF_POOL + W_POOL], pool_w[l], pool_scale[l])
        y_conv = conv_mixer(proj[..., OFF_CONV:OFF_CONV + W_CONV],
                            proj[..., OFF_CONV + W_CONV:OFF_GMLP],
                            conv_w[l], conv_b[l], conv_ln_g[l], conv_ln_b[l])
        y_gmlp = gmlp_mixer(proj[..., OFF_GMLP:OFF_GMLP + W_GMLP],
                            proj[..., OFF_GMLP + W_GMLP:OFF_GATE],
                            gmlp_ln_g[l], gmlp_ln_b[l], gmlp_ws[l], gmlp_bs[l])
        ys = jnp.stack([y_ssm, y_pool, y_conv, y_gmlp], axis=2)
        yb = jnp.einsum('blkc,kcd->blkd', ys, w_branch[l])
        gates = jax.nn.sigmoid(
            proj[..., OFF_GATE:].reshape(nb, L, N_BRANCH, D_MODEL).astype(jnp.float32))
        merged = jnp.sum(yb.astype(jnp.float32) * gates, axis=2).astype(x.dtype)
        mix = merged @ w_o[l]
        x = x + rms_norm(mix, g_post_mix[l])
        h2 = rms_norm(x, g_pre_mlp[l])
        f = jnp.square(jax.nn.relu(h2 @ w_ff1[l])) @ w_ff2[l]
        x = x + rms_norm(f, g_post_mlp[l])
    return x
```

```python
import functools

import jax
import jax.numpy as jnp
from jax import lax
from jax.experimental import pallas as pl
from jax.experimental.pallas import tpu as pltpu

D_MODEL = 2048
DEPTH = 2
N_BRANCH = 4
W_BRANCH = D_MODEL // 4
SSM_GROUP = 16
SSM_GROUPS = W_BRANCH // SSM_GROUP
SSM_STATE = 64
POOL_WINDOWS = (2, 4, 8, 16)
POOL_GW = W_BRANCH // len(POOL_WINDOWS)
CONV_WIDTH = 31
GMLP_CHUNK = 128
GMLP_HEADS = 4
GMLP_HD = W_BRANCH // GMLP_HEADS
D_FF = 4 * D_MODEL
EPS = 1e-6

OFF_SSM = 0
OFF_POOL = OFF_SSM + W_BRANCH
OFF_CONV = OFF_POOL + W_BRANCH
OFF_GMLP = OFF_CONV + 2 * W_BRANCH
OFF_GATE = OFF_GMLP + 2 * W_BRANCH

SSM_T = 16
SSM_LANES = SSM_T * SSM_GROUP
SSM_SCAN_STEPS = 7
CONV_HALO = 32
POOL_HALO = 16
MIX_TC = 256
CONV_RB = 64

VMEM_LIMIT = 56 * 1024 * 1024
BF16 = jnp.bfloat16
F32 = jnp.float32


def _cparams(sem):
    return pltpu.CompilerParams(dimension_semantics=sem, vmem_limit_bytes=VMEM_LIMIT)


def _sigmoid(x):
    return 0.5 * (1.0 + jnp.tanh(0.5 * x))


def _gelu(x):
    c = 0.7978845608028654
    return 0.5 * x * (1.0 + jnp.tanh(c * (x + 0.044715 * (x * x * x))))


def _rms(x, g):
    ms = jnp.mean(x * x, axis=-1, keepdims=True)
    return x * lax.rsqrt(ms + EPS) * g


def _layer_norm(x, g, b):
    mu = jnp.mean(x, axis=-1, keepdims=True)
    xc = x - mu
    var = jnp.mean(xc * xc, axis=-1, keepdims=True)
    return xc * lax.rsqrt(var + EPS) * g + b


def _norm_proj_kernel(x_ref, g_ref, w_ref, h_ref, o_ref, *, rows):
    @pl.when(pl.program_id(1) == 0)
    def _():
        def body(i, carry):
            r = pl.multiple_of(i * rows, rows)
            h_ref[pl.ds(r, rows), :] = _rms(x_ref[pl.ds(r, rows), :], g_ref[...]).astype(h_ref.dtype)
            return carry
        lax.fori_loop(0, x_ref.shape[0] // rows, body, 0)

    o_ref[...] = jnp.dot(h_ref[...], w_ref[...], preferred_element_type=F32)


def _norm_proj(x2, g, w, *, tm=512, tn=1024):
    m, d = x2.shape
    n = w.shape[1]
    return pl.pallas_call(
        functools.partial(_norm_proj_kernel, rows=64),
        grid=(m // tm, n // tn),
        in_specs=[
            pl.BlockSpec((tm, d), lambda i, j: (i, 0)),
            pl.BlockSpec((1, d), lambda i, j: (0, 0)),
            pl.BlockSpec((d, tn), lambda i, j: (0, j)),
        ],
        out_specs=[
            pl.BlockSpec((tm, d), lambda i, j: (i, 0)),
            pl.BlockSpec((tm, tn), lambda i, j: (i, j)),
        ],
        out_shape=[
            jax.ShapeDtypeStruct((m, d), BF16),
            jax.ShapeDtypeStruct((m, n), F32),
        ],
        compiler_params=_cparams(("parallel", "arbitrary")),
        name="norm_proj",
    )(x2, g, w)


def _ssm_prep_kernel(ar_r, ai_r, ldt_r, ar_c, ai_c, ldt_c, cre_ref, cim_ref, bre_ref, bim_ref,
                     cl_ref, bp_ref, k_ref, lama_ref, lamb_ref):
    n = SSM_STATE

    def lam(ar, ai, ldt):
        dt = jnp.exp(ldt)
        mag = jnp.exp(ar * dt)
        ang = ai * dt
        return mag * jnp.cos(ang), mag * jnp.sin(ang)

    lr, li = lam(ar_r[0], ai_r[0], ldt_r[0])
    cre = cre_ref[0]
    cim = cim_ref[0]
    pr = jnp.ones_like(lr)
    pi = jnp.zeros_like(lr)
    for k in range(SSM_T + 1):
        cl_ref[0, k, :, 0:n] = cre * pr - cim * pi
        cl_ref[0, k, :, n:2 * n] = -(cre * pi + cim * pr)
        if k < SSM_T:
            pr, pi = pr * lr - pi * li, pr * li + pi * lr
    qr, qi = pr, pi
    for j in range(8):
        if j < SSM_SCAN_STEPS:
            lama_ref[0, j:j + 1, 0:n] = qr
            lama_ref[0, j:j + 1, n:2 * n] = qr
            lamb_ref[0, j:j + 1, 0:n] = -qi
            lamb_ref[0, j:j + 1, n:2 * n] = qi
            qr, qi = qr * qr - qi * qi, 2.0 * (qr * qi)
        else:
            lama_ref[0, j:j + 1, :] = jnp.zeros((1, 2 * n), F32)
            lamb_ref[0, j:j + 1, :] = jnp.zeros((1, 2 * n), F32)

    ar = ar_c[0]
    ai = ai_c[0]
    lrc, lic = lam(ar, ai, ldt_c[0])
    den = ar * ar + ai * ai
    fr = ((lrc - 1.0) * ar + lic * ai) / den
    fi = (lic * ar - (lrc - 1.0) * ai) / den
    br = bre_ref[0]
    bi = bim_ref[0]
    bbr = fr * br - fi * bi
    bbi = fr * bi + fi * br
    pr = jnp.ones_like(lrc)
    pi = jnp.zeros_like(lrc)
    for k in range(SSM_T):
        s = SSM_T - 1 - k
        bp_ref[0, s, 0:n, :] = pr * bbr - pi * bbi
        bp_ref[0, s, n:2 * n, :] = pr * bbi + pi * bbr
        pr, pi = pr * lrc - pi * lic, pr * lic + pi * lrc

    clmat = cl_ref[0, 0:SSM_T].reshape(SSM_T * SSM_GROUP, 2 * n)
    k_ref[0] = jnp.dot(clmat, bp_ref[0, SSM_T - 1], preferred_element_type=F32,
                       precision=lax.Precision.HIGHEST)


def _ssm_prep(a_re, a_im, log_dt, b_re, b_im, c_re, c_im):
    n, p = SSM_STATE, SSM_GROUP
    gg = a_re.shape[0] * a_re.shape[1]
    ar = a_re.reshape(gg, n)
    ai = a_im.reshape(gg, n)
    ldt = jnp.broadcast_to(log_dt.reshape(gg, 1), (gg, n))
    row = lambda a: a.reshape(gg, 1, n)
    col = lambda a: a.reshape(gg, n, 1)
    rspec = pl.BlockSpec((1, 1, n), lambda g: (g, 0, 0))
    cspec = pl.BlockSpec((1, n, 1), lambda g: (g, 0, 0))
    return pl.pallas_call(
        _ssm_prep_kernel,
        grid=(gg,),
        in_specs=[rspec, rspec, rspec, cspec, cspec, cspec,
                  pl.BlockSpec((1, p, n), lambda g: (g, 0, 0)),
                  pl.BlockSpec((1, p, n), lambda g: (g, 0, 0)),
                  pl.BlockSpec((1, n, p), lambda g: (g, 0, 0)),
                  pl.BlockSpec((1, n, p), lambda g: (g, 0, 0))],
        out_specs=[
            pl.BlockSpec((1, SSM_T + 1, p, 2 * n), lambda g: (g, 0, 0, 0)),
            pl.BlockSpec((1, SSM_T, 2 * n, p), lambda g: (g, 0, 0, 0)),
            pl.BlockSpec((1, SSM_T * p, p), lambda g: (g, 0, 0)),
            pl.BlockSpec((1, 8, 2 * n), lambda g: (g, 0, 0)),
            pl.BlockSpec((1, 8, 2 * n), lambda g: (g, 0, 0)),
        ],
        out_shape=[
            jax.ShapeDtypeStruct((gg, SSM_T + 1, p, 2 * n), F32),
            jax.ShapeDtypeStruct((gg, SSM_T, 2 * n, p), F32),
            jax.ShapeDtypeStruct((gg, SSM_T * p, p), F32),
            jax.ShapeDtypeStruct((gg, 8, 2 * n), F32),
            jax.ShapeDtypeStruct((gg, 8, 2 * n), F32),
        ],
        compiler_params=_cparams(("arbitrary",)),
        name="ssm_prep",
    )(row(ar), row(ai), row(ldt), col(ar), col(ai), col(ldt),
      c_re.reshape(gg, p, n), c_im.reshape(gg, p, n), b_re.reshape(gg, n, p), b_im.reshape(gg, n, p))


def _ssm_operators(cl, bp, kall):
    gg = cl.shape[0]
    n2, p, t = 2 * SSM_STATE, SSM_GROUP, SSM_T
    cpow = jnp.transpose(cl[:, 1:], (0, 3, 1, 2)).reshape(gg, n2, t * p)
    bpow = jnp.transpose(bp, (0, 1, 3, 2)).reshape(gg, t * p, n2)
    kt = kall.reshape(gg, t, p, p)
    s_idx = jnp.arange(t)[:, None]
    t_idx = jnp.arange(t)[None, :]
    lag = jnp.clip(t_idx - s_idx, 0, t - 1)
    k4 = kt[:, lag]
    k4 = jnp.where((t_idx >= s_idx)[None, :, :, None, None], k4, 0.0)
    toep = jnp.transpose(k4, (0, 1, 4, 2, 3)).reshape(gg, t * p, t * p)
    return toep.astype(BF16), bpow.astype(BF16), cpow.astype(BF16)


def _ssm_kernel(u_ref, toep_ref, bpow_ref, cpow_ref, lama_ref, lamb_ref, y_ref, xs_ref, *, nb, pad):
    rows = u_ref.shape[1]
    n2 = 2 * SSM_STATE
    u = u_ref[0]
    xs_ref[0:pad, :] = jnp.zeros((pad, n2), F32)
    xs_ref[pad:pad + rows, :] = jnp.dot(u, bpow_ref[0], preferred_element_type=F32)
    for j in range(SSM_SCAN_STEPS):
        sh = nb * (1 << j)
        cur = xs_ref[pad:pad + rows, :]
        prev = xs_ref[pad - sh:pad - sh + rows, :]
        la = lama_ref[0, j:j + 1, :]
        lb = lamb_ref[0, j:j + 1, :]
        xs_ref[pad:pad + rows, :] = cur + prev * la + pltpu.roll(prev, SSM_STATE, axis=1) * lb
    xprev = xs_ref[pad - nb:pad - nb + rows, :].astype(BF16)
    y_ref[0] = (jnp.dot(u, toep_ref[0], preferred_element_type=F32)
                + jnp.dot(xprev, cpow_ref[0], preferred_element_type=F32))


def _ssm_scan(up, toep, bpow, cpow, lama, lamb, *, nb):
    g, rows, lanes = up.shape
    n2 = 2 * SSM_STATE
    pad = nb * (1 << (SSM_SCAN_STEPS - 1))
    gspec = lambda a, b: pl.BlockSpec((1, a, b), lambda i: (i, 0, 0))
    return pl.pallas_call(
        functools.partial(_ssm_kernel, nb=nb, pad=pad),
        grid=(g,),
        in_specs=[gspec(rows, lanes), gspec(lanes, lanes), gspec(lanes, n2), gspec(n2, lanes),
                  gspec(8, n2), gspec(8, n2)],
        out_specs=gspec(rows, lanes),
        out_shape=jax.ShapeDtypeStruct((g, rows, lanes), F32),
        scratch_shapes=[pltpu.VMEM((pad + rows, n2), F32)],
        compiler_params=_cparams(("parallel",)),
        name="ssm_scan",
    )(up, toep, bpow, cpow, lama, lamb)


def _mix_kernel(pm_ref, yraw_ref, d_ref, wglu_ref, poolw_ref, pscale_ref, convw_ref, convb_ref,
                clg_ref, clb_ref, glg_ref, glb_ref, ws_ref, bst_ref,
                yssm_ref, ypool_ref, yconv_ref, ygmlp_ref, vbuf, pbuf):
    c = pl.program_id(1)
    tc = pm_ref.shape[0]
    w = W_BRANCH

    @pl.when(c == 0)
    def _():
        vbuf[0:CONV_HALO, :] = jnp.zeros((CONV_HALO, w), F32)
        pbuf[0:POOL_HALO, :] = jnp.zeros((POOL_HALO, w), F32)

    y = yraw_ref[...] + d_ref[...] * pm_ref[:, OFF_SSM:OFF_SSM + w]
    zg = jnp.dot(_gelu(y).astype(BF16), wglu_ref[...], preferred_element_type=F32)
    yssm_ref[...] = (zg[:, :w] * _sigmoid(zg[:, w:])).astype(yssm_ref.dtype)

    pbuf[POOL_HALO:POOL_HALO + tc, :] = pm_ref[:, OFF_POOL:OFF_POOL + w]
    t = c * tc + lax.broadcasted_iota(jnp.int32, (tc, 1), 0)
    for gi, win in enumerate(POOL_WINDOWS):
        cols = slice(gi * POOL_GW, (gi + 1) * POOL_GW)
        tok = pbuf[POOL_HALO:POOL_HALO + tc, cols]
        s = tok
        for j in range(1, win):
            s = s + pbuf[POOL_HALO - j:POOL_HALO - j + tc, cols]
        count = jnp.minimum(t + 1, win).astype(F32)
        pooled = s / count - tok
        mixed = jnp.dot(pooled.astype(BF16), poolw_ref[gi], preferred_element_type=F32)
        ypool_ref[:, cols] = (mixed * pscale_ref[:, cols]).astype(ypool_ref.dtype)
    pbuf[0:POOL_HALO, :] = pbuf[tc:tc + POOL_HALO, :]

    vbuf[CONV_HALO:CONV_HALO + tc, :] = (pm_ref[:, OFF_CONV:OFF_CONV + w]
                                         * _sigmoid(pm_ref[:, OFF_CONV + w:OFF_CONV + 2 * w]))
    base = CONV_HALO - (CONV_WIDTH - 1)
    for rb in range(tc // CONV_RB):
        r0 = rb * CONV_RB
        acc = jnp.broadcast_to(convb_ref[...], (CONV_RB, w))
        for k in range(CONV_WIDTH):
            acc = acc + convw_ref[k:k + 1, :] * vbuf[r0 + base + k:r0 + base + k + CONV_RB, :]
        yn = _layer_norm(acc, clg_ref[...], clb_ref[...])
        yconv_ref[r0:r0 + CONV_RB, :] = (yn * _sigmoid(yn)).astype(yconv_ref.dtype)
    vbuf[0:CONV_HALO, :] = vbuf[tc:tc + CONV_HALO, :]

    ug = _gelu(pm_ref[:, OFF_GMLP:OFF_GMLP + w])
    vg = _layer_norm(_gelu(pm_ref[:, OFF_GMLP + w:OFF_GMLP + 2 * w]), glg_ref[...], glb_ref[...]).astype(BF16)
    ri = lax.broadcasted_iota(jnp.int32, (GMLP_CHUNK, GMLP_CHUNK), 0)
    ci = lax.broadcasted_iota(jnp.int32, (GMLP_CHUNK, GMLP_CHUNK), 1)
    causal = ri >= ci
    for h in range(GMLP_HEADS):
        wsm = jnp.where(causal, ws_ref[h], 0.0).astype(BF16)
        cols = slice(h * GMLP_HD, (h + 1) * GMLP_HD)
        for ch in range(tc // GMLP_CHUNK):
            rws = slice(ch * GMLP_CHUNK, (ch + 1) * GMLP_CHUNK)
            sv = jnp.dot(wsm, vg[rws, cols], preferred_element_type=F32) + bst_ref[:, h:h + 1]
            ygmlp_ref[rws, cols] = (ug[rws, cols] * sv).astype(ygmlp_ref.dtype)


def _mixers(pm, yraw, d, wglu, poolw, pscale, convw, convb, clg, clb, glg, glb, ws, bst, *, nb, seq):
    m = pm.shape[0]
    tc = MIX_TC
    nct = seq // tc
    w = W_BRANCH
    row = lambda width: pl.BlockSpec((tc, width), lambda b, c: (b * nct + c, 0))
    full = lambda a: pl.BlockSpec(a.shape, lambda b, c: (0,) * a.ndim)
    args = (d, wglu, poolw, pscale, convw, convb, clg, clb, glg, glb, ws, bst)
    return pl.pallas_call(
        _mix_kernel,
        grid=(nb, nct),
        in_specs=[row(OFF_GATE), row(w)] + [full(a) for a in args],
        out_specs=[row(w)] * 4,
        out_shape=[jax.ShapeDtypeStruct((m, w), BF16)] * 4,
        scratch_shapes=[pltpu.VMEM((CONV_HALO + tc, w), F32), pltpu.VMEM((POOL_HALO + tc, w), F32)],
        compiler_params=_cparams(("parallel", "arbitrary")),
        name="mixers",
    )(pm, yraw, *args)


def _merge_kernel(h_ref, y0, y1, y2, y3, g0, g1, g2, g3, b0, b1, b2, b3, o_ref):
    ys = (y0, y1, y2, y3)
    wg = (g0, g1, g2, g3)
    wb = (b0, b1, b2, b3)
    h = h_ref[...]
    acc = None
    for k in range(N_BRANCH):
        gate = jnp.dot(h, wg[k][...], preferred_element_type=F32)
        yb = jnp.dot(ys[k][...], wb[k][0], preferred_element_type=F32)
        term = yb * _sigmoid(gate)
        acc = term if acc is None else acc + term
    o_ref[...] = acc.astype(o_ref.dtype)


def _merge(h, ys, wgate, wbranch, *, tm=512, tn=512):
    m, d = h.shape
    w = W_BRANCH
    nj = d // tn
    gate_spec = lambda k: pl.BlockSpec((d, tn), lambda i, j: (0, k * nj + j))
    br_spec = lambda k: pl.BlockSpec((1, w, tn), lambda i, j: (k, 0, j))
    return pl.pallas_call(
        _merge_kernel,
        grid=(m // tm, nj),
        in_specs=[pl.BlockSpec((tm, d), lambda i, j: (i, 0))]
        + [pl.BlockSpec((tm, w), lambda i, j: (i, 0))] * N_BRANCH
        + [gate_spec(k) for k in range(N_BRANCH)]
        + [br_spec(k) for k in range(N_BRANCH)],
        out_specs=pl.BlockSpec((tm, tn), lambda i, j: (i, j)),
        out_shape=jax.ShapeDtypeStruct((m, d), BF16),
        compiler_params=_cparams(("parallel", "arbitrary")),
        name="gated_merge",
    )(h, *ys, *([wgate] * N_BRANCH), *([wbranch] * N_BRANCH))


def _out_kernel(m_ref, w_ref, x_ref, g_ref, o_ref):
    mix = jnp.dot(m_ref[...], w_ref[...], preferred_element_type=F32)
    o_ref[...] = x_ref[...] + _rms(mix, g_ref[...])


def _out_proj(merged, w_o, x2, g, *, tm=256):
    m, d = x2.shape
    return pl.pallas_call(
        _out_kernel,
        grid=(m // tm,),
        in_specs=[
            pl.BlockSpec((tm, d), lambda i: (i, 0)),
            pl.BlockSpec((d, d), lambda i: (0, 0)),
            pl.BlockSpec((tm, d), lambda i: (i, 0)),
            pl.BlockSpec((1, d), lambda i: (0, 0)),
        ],
        out_specs=pl.BlockSpec((tm, d), lambda i: (i, 0)),
        out_shape=jax.ShapeDtypeStruct((m, d), F32),
        compiler_params=_cparams(("parallel",)),
        name="out_proj",
    )(merged, w_o, x2, g)


def _ffn_kernel(x_ref, gpre_ref, w1_ref, w2_ref, gpost_ref, o_ref, h_ref, acc_ref, *, rows):
    j = pl.program_id(1)

    @pl.when(j == 0)
    def _():
        def body(i, carry):
            r = pl.multiple_of(i * rows, rows)
            h_ref[pl.ds(r, rows), :] = _rms(x_ref[pl.ds(r, rows), :], gpre_ref[...]).astype(h_ref.dtype)
            return carry
        lax.fori_loop(0, x_ref.shape[0] // rows, body, 0)

    a = jnp.dot(h_ref[...], w1_ref[...], preferred_element_type=F32)
    a = jnp.maximum(a, 0.0)
    a = (a * a).astype(BF16)
    part = jnp.dot(a, w2_ref[...], preferred_element_type=F32)

    @pl.when(j == 0)
    def _():
        acc_ref[...] = part

    @pl.when(j > 0)
    def _():
        acc_ref[...] += part

    @pl.when(j == pl.num_programs(1) - 1)
    def _():
        def body(i, carry):
            r = pl.multiple_of(i * rows, rows)
            o_ref[pl.ds(r, rows), :] = (x_ref[pl.ds(r, rows), :]
                                        + _rms(acc_ref[pl.ds(r, rows), :], gpost_ref[...]))
            return carry
        lax.fori_loop(0, x_ref.shape[0] // rows, body, 0)


def _ffn(x2, gpre, w1, w2, gpost, *, tm=512, tf=512):
    m, d = x2.shape
    f = w1.shape[1]
    return pl.pallas_call(
        functools.partial(_ffn_kernel, rows=64),
        grid=(m // tm, f // tf),
        in_specs=[
            pl.BlockSpec((tm, d), lambda i, j: (i, 0)),
            pl.BlockSpec((1, d), lambda i, j: (0, 0)),
            pl.BlockSpec((d, tf), lambda i, j: (0, j)),
            pl.BlockSpec((tf, d), lambda i, j: (j, 0)),
            pl.BlockSpec((1, d), lambda i, j: (0, 0)),
        ],
        out_specs=pl.BlockSpec((tm, d), lambda i, j: (i, 0)),
        out_shape=jax.ShapeDtypeStruct((m, d), F32),
        scratch_shapes=[pltpu.VMEM((tm, d), BF16), pltpu.VMEM((tm, d), F32)],
        compiler_params=_cparams(("parallel", "arbitrary")),
        name="ffn",
    )(x2, gpre, w1, w2, gpost)


def kernel(x, g_pre_mix, w_in, ssm_a_re, ssm_a_im, ssm_log_dt, ssm_b_re, ssm_b_im, ssm_c_re, ssm_c_im,
           ssm_d, ssm_w_glu, pool_w, pool_scale, conv_w, conv_b, conv_ln_g, conv_ln_b, gmlp_ln_g,
           gmlp_ln_b, gmlp_ws, gmlp_bs, w_branch, w_o, g_post_mix, g_pre_mlp, w_ff1, w_ff2, g_post_mlp):
    nb, seq, d = x.shape
    m = nb * seq
    depth = w_in.shape[0]
    g, p, t = SSM_GROUPS, SSM_GROUP, SSM_T
    nchunk = seq // t
    assert d == D_MODEL and seq % MIX_TC == 0 and nchunk == 1 << SSM_SCAN_STEPS

    cl, bp, kall, lama, lamb = _ssm_prep(ssm_a_re, ssm_a_im, ssm_log_dt, ssm_b_re, ssm_b_im,
                                         ssm_c_re, ssm_c_im)
    toep, bpow, cpow = _ssm_operators(cl, bp, kall)

    row = lambda a: a.reshape(1, -1)
    x2 = x.reshape(m, d)
    for l in range(depth):
        gs = slice(l * g, (l + 1) * g)
        w_in_l = w_in[l].astype(BF16)
        h, pm = _norm_proj(x2, row(g_pre_mix[l]), w_in_l[:, :OFF_GATE])

        up = pm[:, OFF_SSM:OFF_SSM + W_BRANCH].reshape(nb, nchunk, t, g, p)
        up = jnp.transpose(up, (3, 1, 0, 2, 4)).reshape(g, nchunk * nb, t * p).astype(BF16)
        yp = _ssm_scan(up, toep[gs], bpow[gs], cpow[gs], lama[gs], lamb[gs], nb=nb)
        yraw = jnp.transpose(yp.reshape(g, nchunk, nb, t, p), (2, 1, 3, 0, 4)).reshape(m, W_BRANCH)

        ys = _mixers(pm, yraw, row(ssm_d[l]), ssm_w_glu[l].astype(BF16), pool_w[l].astype(BF16),
                     row(pool_scale[l]), conv_w[l], row(conv_b[l]), row(conv_ln_g[l]), row(conv_ln_b[l]),
                     row(gmlp_ln_g[l]), row(gmlp_ln_b[l]), gmlp_ws[l], jnp.transpose(gmlp_bs[l]),
                     nb=nb, seq=seq)
        merged = _merge(h, ys, w_in_l[:, OFF_GATE:], w_branch[l].astype(BF16))
        x2 = _out_proj(merged, w_o[l].astype(BF16), x2, row(g_post_mix[l]))
        x2 = _ffn(x2, row(g_pre_mlp[l]), w_ff1[l].astype(BF16), w_ff2[l].astype(BF16), row(g_post_mlp[l]))
    return x2.reshape(nb, seq, d)
```

```python
import functools

import jax
import jax.numpy as jnp
from jax import lax
from jax.experimental import pallas as pl
from jax.experimental.pallas import tpu as pltpu

D_MODEL = 2048
DEPTH = 2
N_BRANCH = 4
W_BRANCH = D_MODEL // 4
SSM_GROUP = 16
SSM_GROUPS = W_BRANCH // SSM_GROUP
SSM_STATE = 64
POOL_WINDOWS = (2, 4, 8, 16)
POOL_GW = W_BRANCH // len(POOL_WINDOWS)
CONV_WIDTH = 31
GMLP_CHUNK = 128
GMLP_HEADS = 4
GMLP_HD = W_BRANCH // GMLP_HEADS
D_FF = 4 * D_MODEL
EPS = 1e-6

OFF_SSM = 0
OFF_POOL = OFF_SSM + W_BRANCH
OFF_CONV = OFF_POOL + W_BRANCH
OFF_GMLP = OFF_CONV + 2 * W_BRANCH
OFF_GATE = OFF_GMLP + 2 * W_BRANCH

SSM_T = 16
SSM_LANES = SSM_T * SSM_GROUP
SSM_SCAN_STEPS = 7
CONV_HALO = 32
POOL_HALO = 16
MIX_TC = 256
CONV_RB = 64

VMEM_LIMIT = 56 * 1024 * 1024
BF16 = jnp.bfloat16
F32 = jnp.float32


def _cparams(sem):
    return pltpu.CompilerParams(dimension_semantics=sem, vmem_limit_bytes=VMEM_LIMIT)


def _sigmoid(x):
    return 0.5 * (1.0 + jnp.tanh(0.5 * x))


def _gelu(x):
    c = 0.7978845608028654
    return 0.5 * x * (1.0 + jnp.tanh(c * (x + 0.044715 * (x * x * x))))


def _rms(x, g):
    ms = jnp.mean(x * x, axis=-1, keepdims=True)
    return x * lax.rsqrt(ms + EPS) * g


def _layer_norm(x, g, b):
    mu = jnp.mean(x, axis=-1, keepdims=True)
    xc = x - mu
    var = jnp.mean(xc * xc, axis=-1, keepdims=True)
    return xc * lax.rsqrt(var + EPS) * g + b


def _cast_kernel(w_ref, o_ref):
    o_ref[...] = w_ref[...].astype(o_ref.dtype)


def _cast_bf16(w, rows):
    depth, r, c = w.shape
    spec = pl.BlockSpec((1, rows, c), lambda l, i: (l, i, 0))
    return pl.pallas_call(
        _cast_kernel,
        grid=(depth, r // rows),
        in_specs=[spec],
        out_specs=spec,
        out_shape=jax.ShapeDtypeStruct(w.shape, BF16),
        compiler_params=_cparams(("parallel", "parallel")),
        name="cast_bf16",
    )(w)


def _norm_proj_kernel(x_ref, g_ref, w_ref, h_ref, o_ref, *, rows):
    def body(i, carry):
        r = pl.multiple_of(i * rows, rows)
        h_ref[pl.ds(r, rows), :] = _rms(x_ref[pl.ds(r, rows), :], g_ref[...]).astype(h_ref.dtype)
        return carry
    lax.fori_loop(0, x_ref.shape[0] // rows, body, 0)
    o_ref[...] = jnp.dot(h_ref[...], w_ref[...], preferred_element_type=F32)


def _norm_proj(x2, g, w_in, layer, *, tm=512):
    m, d = x2.shape
    n = OFF_GATE
    return pl.pallas_call(
        functools.partial(_norm_proj_kernel, rows=64),
        grid=(m // tm,),
        in_specs=[
            pl.BlockSpec((tm, d), lambda i: (i, 0)),
            pl.BlockSpec((1, d), lambda i: (0, 0)),
            pl.BlockSpec((None, d, n), lambda i: (layer, 0, 0), pipeline_mode=pl.Buffered(1)),
        ],
        out_specs=[
            pl.BlockSpec((tm, d), lambda i: (i, 0)),
            pl.BlockSpec((tm, n), lambda i: (i, 0)),
        ],
        out_shape=[
            jax.ShapeDtypeStruct((m, d), BF16),
            jax.ShapeDtypeStruct((m, n), F32),
        ],
        compiler_params=_cparams(("parallel",)),
        name="norm_proj",
    )(x2, g, w_in)


def _ssm_prep_kernel(ar_r, ai_r, ldt_r, ar_c, ai_c, ldt_c, cre_ref, cim_ref, bre_ref, bim_ref,
                     cl_ref, bp_ref, k_ref, lama_ref, lamb_ref):
    n = SSM_STATE

    def lam(ar, ai, ldt):
        dt = jnp.exp(ldt)
        mag = jnp.exp(ar * dt)
        ang = ai * dt
        return mag * jnp.cos(ang), mag * jnp.sin(ang)

    lr, li = lam(ar_r[0], ai_r[0], ldt_r[0])
    cre = cre_ref[0]
    cim = cim_ref[0]
    pr = jnp.ones_like(lr)
    pi = jnp.zeros_like(lr)
    for k in range(SSM_T + 1):
        cl_ref[0, k, :, 0:n] = cre * pr - cim * pi
        cl_ref[0, k, :, n:2 * n] = -(cre * pi + cim * pr)
        if k < SSM_T:
            pr, pi = pr * lr - pi * li, pr * li + pi * lr
    qr, qi = pr, pi
    for j in range(8):
        if j < SSM_SCAN_STEPS:
            lama_ref[0, j:j + 1, 0:n] = qr
            lama_ref[0, j:j + 1, n:2 * n] = qr
            lamb_ref[0, j:j + 1, 0:n] = -qi
            lamb_ref[0, j:j + 1, n:2 * n] = qi
            qr, qi = qr * qr - qi * qi, 2.0 * (qr * qi)
        else:
            lama_ref[0, j:j + 1, :] = jnp.zeros((1, 2 * n), F32)
            lamb_ref[0, j:j + 1, :] = jnp.zeros((1, 2 * n), F32)

    ar = ar_c[0]
    ai = ai_c[0]
    lrc, lic = lam(ar, ai, ldt_c[0])
    den = ar * ar + ai * ai
    fr = ((lrc - 1.0) * ar + lic * ai) / den
    fi = (lic * ar - (lrc - 1.0) * ai) / den
    br = bre_ref[0]
    bi = bim_ref[0]
    bbr = fr * br - fi * bi
    bbi = fr * bi + fi * br
    pr = jnp.ones_like(lrc)
    pi = jnp.zeros_like(lrc)
    for k in range(SSM_T):
        s = SSM_T - 1 - k
        bp_ref[0, s, 0:n, :] = pr * bbr - pi * bbi
        bp_ref[0, s, n:2 * n, :] = pr * bbi + pi * bbr
        pr, pi = pr * lrc - pi * lic, pr * lic + pi * lrc

    clmat = cl_ref[0, 0:SSM_T].reshape(SSM_T * SSM_GROUP, 2 * n)
    k_ref[0] = jnp.dot(clmat, bp_ref[0, SSM_T - 1], preferred_element_type=F32,
                       precision=lax.Precision.HIGHEST)


def _ssm_prep(a_re, a_im, log_dt, b_re, b_im, c_re, c_im):
    n, p = SSM_STATE, SSM_GROUP
    gg = a_re.shape[0] * a_re.shape[1]
    ar = a_re.reshape(gg, n)
    ai = a_im.reshape(gg, n)
    ldt = jnp.broadcast_to(log_dt.reshape(gg, 1), (gg, n))
    row = lambda a: a.reshape(gg, 1, n)
    col = lambda a: a.reshape(gg, n, 1)
    rspec = pl.BlockSpec((1, 1, n), lambda g: (g, 0, 0))
    cspec = pl.BlockSpec((1, n, 1), lambda g: (g, 0, 0))
    return pl.pallas_call(
        _ssm_prep_kernel,
        grid=(gg,),
        in_specs=[rspec, rspec, rspec, cspec, cspec, cspec,
                  pl.BlockSpec((1, p, n), lambda g: (g, 0, 0)),
                  pl.BlockSpec((1, p, n), lambda g: (g, 0, 0)),
                  pl.BlockSpec((1, n, p), lambda g: (g, 0, 0)),
                  pl.BlockSpec((1, n, p), lambda g: (g, 0, 0))],
        out_specs=[
            pl.BlockSpec((1, SSM_T + 1, p, 2 * n), lambda g: (g, 0, 0, 0)),
            pl.BlockSpec((1, SSM_T, 2 * n, p), lambda g: (g, 0, 0, 0)),
            pl.BlockSpec((1, SSM_T * p, p), lambda g: (g, 0, 0)),
            pl.BlockSpec((1, 8, 2 * n), lambda g: (g, 0, 0)),
            pl.BlockSpec((1, 8, 2 * n), lambda g: (g, 0, 0)),
        ],
        out_shape=[
            jax.ShapeDtypeStruct((gg, SSM_T + 1, p, 2 * n), F32),
            jax.ShapeDtypeStruct((gg, SSM_T, 2 * n, p), F32),
            jax.ShapeDtypeStruct((gg, SSM_T * p, p), F32),
            jax.ShapeDtypeStruct((gg, 8, 2 * n), F32),
            jax.ShapeDtypeStruct((gg, 8, 2 * n), F32),
        ],
        compiler_params=_cparams(("arbitrary",)),
        name="ssm_prep",
    )(row(ar), row(ai), row(ldt), col(ar), col(ai), col(ldt),
      c_re.reshape(gg, p, n), c_im.reshape(gg, p, n), b_re.reshape(gg, n, p), b_im.reshape(gg, n, p))


def _ssm_operators(cl, bp, kall):
    gg = cl.shape[0]
    n2, p, t = 2 * SSM_STATE, SSM_GROUP, SSM_T
    cpow = jnp.transpose(cl[:, 1:], (0, 3, 1, 2)).reshape(gg, n2, t * p)
    bpow = jnp.transpose(bp, (0, 1, 3, 2)).reshape(gg, t * p, n2)
    kt = kall.reshape(gg, t, p, p)
    s_idx = jnp.arange(t)[:, None]
    t_idx = jnp.arange(t)[None, :]
    lag = jnp.clip(t_idx - s_idx, 0, t - 1)
    k4 = kt[:, lag]
    k4 = jnp.where((t_idx >= s_idx)[None, :, :, None, None], k4, 0.0)
    toep = jnp.transpose(k4, (0, 1, 4, 2, 3)).reshape(gg, t * p, t * p)
    return toep.astype(BF16), bpow.astype(BF16), cpow.astype(BF16)


def _ssm_kernel(u_ref, toep_ref, bpow_ref, cpow_ref, lama_ref, lamb_ref, y_ref, xs_ref, *, nb, pad):
    rows = u_ref.shape[1]
    n2 = 2 * SSM_STATE
    u = u_ref[0]
    xs_ref[0:pad, :] = jnp.zeros((pad, n2), F32)
    xs_ref[pad:pad + rows, :] = jnp.dot(u, bpow_ref[0], preferred_element_type=F32)
    for j in range(SSM_SCAN_STEPS):
        sh = nb * (1 << j)
        cur = xs_ref[pad:pad + rows, :]
        prev = xs_ref[pad - sh:pad - sh + rows, :]
        la = lama_ref[0, j:j + 1, :]
        lb = lamb_ref[0, j:j + 1, :]
        xs_ref[pad:pad + rows, :] = cur + prev * la + pltpu.roll(prev, SSM_STATE, axis=1) * lb
    xprev = xs_ref[pad - nb:pad - nb + rows, :].astype(BF16)
    y_ref[0] = (jnp.dot(u, toep_ref[0], preferred_element_type=F32)
                + jnp.dot(xprev, cpow_ref[0], preferred_element_type=F32))


def _ssm_scan(up, toep, bpow, cpow, lama, lamb, *, nb):
    g, rows, lanes = up.shape
    n2 = 2 * SSM_STATE
    pad = nb * (1 << (SSM_SCAN_STEPS - 1))
    gspec = lambda a, b: pl.BlockSpec((1, a, b), lambda i: (i, 0, 0))
    return pl.pallas_call(
        functools.partial(_ssm_kernel, nb=nb, pad=pad),
        grid=(g,),
        in_specs=[gspec(rows, lanes), gspec(lanes, lanes), gspec(lanes, n2), gspec(n2, lanes),
                  gspec(8, n2), gspec(8, n2)],
        out_specs=gspec(rows, lanes),
        out_shape=jax.ShapeDtypeStruct((g, rows, lanes), F32),
        scratch_shapes=[pltpu.VMEM((pad + rows, n2), F32)],
        compiler_params=_cparams(("parallel",)),
        name="ssm_scan",
    )(up, toep, bpow, cpow, lama, lamb)


def _mix_kernel(pm_ref, yraw_ref, d_ref, wglu_ref, poolw_ref, pscale_ref, convw_ref, convb_ref,
                clg_ref, clb_ref, glg_ref, glb_ref, ws_ref, bst_ref,
                yssm_ref, ypool_ref, yconv_ref, ygmlp_ref, vbuf, pbuf):
    c = pl.program_id(1)
    tc = pm_ref.shape[0]
    w = W_BRANCH

    @pl.when(c == 0)
    def _():
        vbuf[0:CONV_HALO, :] = jnp.zeros((CONV_HALO, w), F32)
        pbuf[0:POOL_HALO, :] = jnp.zeros((POOL_HALO, w), F32)

    y = yraw_ref[...] + d_ref[...] * pm_ref[:, OFF_SSM:OFF_SSM + w]
    zg = jnp.dot(_gelu(y).astype(BF16), wglu_ref[...], preferred_element_type=F32)
    yssm_ref[...] = (zg[:, :w] * _sigmoid(zg[:, w:])).astype(yssm_ref.dtype)

    pbuf[POOL_HALO:POOL_HALO + tc, :] = pm_ref[:, OFF_POOL:OFF_POOL + w]
    t = c * tc + lax.broadcasted_iota(jnp.int32, (tc, 1), 0)
    for gi, win in enumerate(POOL_WINDOWS):
        cols = slice(gi * POOL_GW, (gi + 1) * POOL_GW)
        tok = pbuf[POOL_HALO:POOL_HALO + tc, cols]
        s = tok
        for j in range(1, win):
            s = s + pbuf[POOL_HALO - j:POOL_HALO - j + tc, cols]
        count = jnp.minimum(t + 1, win).astype(F32)
        pooled = s / count - tok
        mixed = jnp.dot(pooled.astype(BF16), poolw_ref[gi], preferred_element_type=F32)
        ypool_ref[:, cols] = (mixed * pscale_ref[:, cols]).astype(ypool_ref.dtype)
    pbuf[0:POOL_HALO, :] = pbuf[tc:tc + POOL_HALO, :]

    vbuf[CONV_HALO:CONV_HALO + tc, :] = (pm_ref[:, OFF_CONV:OFF_CONV + w]
                                         * _sigmoid(pm_ref[:, OFF_CONV + w:OFF_CONV + 2 * w]))
    base = CONV_HALO - (CONV_WIDTH - 1)
    for rb in range(tc // CONV_RB):
        r0 = rb * CONV_RB
        acc = jnp.broadcast_to(convb_ref[...], (CONV_RB, w))
        for k in range(CONV_WIDTH):
            acc = acc + convw_ref[k:k + 1, :] * vbuf[r0 + base + k:r0 + base + k + CONV_RB, :]
        yn = _layer_norm(acc, clg_ref[...], clb_ref[...])
        yconv_ref[r0:r0 + CONV_RB, :] = (yn * _sigmoid(yn)).astype(yconv_ref.dtype)
    vbuf[0:CONV_HALO, :] = vbuf[tc:tc + CONV_HALO, :]

    ug = _gelu(pm_ref[:, OFF_GMLP:OFF_GMLP + w])
    vg = _layer_norm(_gelu(pm_ref[:, OFF_GMLP + w:OFF_GMLP + 2 * w]), glg_ref[...], glb_ref[...]).astype(BF16)
    ri = lax.broadcasted_iota(jnp.int32, (GMLP_CHUNK, GMLP_CHUNK), 0)
    ci = lax.broadcasted_iota(jnp.int32, (GMLP_CHUNK, GMLP_CHUNK), 1)
    causal = ri >= ci
    for h in range(GMLP_HEADS):
        wsm = jnp.where(causal, ws_ref[h], 0.0).astype(BF16)
        cols = slice(h * GMLP_HD, (h + 1) * GMLP_HD)
        for ch in range(tc // GMLP_CHUNK):
            rws = slice(ch * GMLP_CHUNK, (ch + 1) * GMLP_CHUNK)
            sv = jnp.dot(wsm, vg[rws, cols], preferred_element_type=F32) + bst_ref[:, h:h + 1]
            ygmlp_ref[rws, cols] = (ug[rws, cols] * sv).astype(ygmlp_ref.dtype)


def _mixers(pm, yraw, d, wglu, poolw, pscale, convw, convb, clg, clb, glg, glb, ws, bst, *, nb, seq):
    m = pm.shape[0]
    tc = MIX_TC
    nct = seq // tc
    w = W_BRANCH
    row = lambda width: pl.BlockSpec((tc, width), lambda b, c: (b * nct + c, 0))
    full = lambda a: pl.BlockSpec(a.shape, lambda b, c: (0,) * a.ndim)
    args = (d, wglu, poolw, pscale, convw, convb, clg, clb, glg, glb, ws, bst)
    return pl.pallas_call(
        _mix_kernel,
        grid=(nb, nct),
        in_specs=[row(OFF_GATE), row(w)] + [full(a) for a in args],
        out_specs=[row(w)] * 4,
        out_shape=[jax.ShapeDtypeStruct((m, w), BF16)] * 4,
        scratch_shapes=[pltpu.VMEM((CONV_HALO + tc, w), F32), pltpu.VMEM((POOL_HALO + tc, w), F32)],
        compiler_params=_cparams(("parallel", "arbitrary")),
        name="mixers",
    )(pm, yraw, *args)


def _merge_kernel(h_ref, y0, y1, y2, y3, g0, g1, g2, g3, b0, b1, b2, b3, o_ref):
    ys = (y0, y1, y2, y3)
    wg = (g0, g1, g2, g3)
    wb = (b0, b1, b2, b3)
    h = h_ref[...]
    acc = None
    for k in range(N_BRANCH):
        gate = jnp.dot(h, wg[k][...], preferred_element_type=F32)
        yb = jnp.dot(ys[k][...], wb[k][0], preferred_element_type=F32)
        term = yb * _sigmoid(gate)
        acc = term if acc is None else acc + term
    o_ref[...] = acc.astype(o_ref.dtype)


def _merge(h, ys, wgate, wbranch, layer, *, tm=1024, tn=512):
    m, d = h.shape
    w = W_BRANCH
    nj = d // tn
    j0 = OFF_GATE // tn
    gate_spec = lambda k: pl.BlockSpec((None, d, tn), lambda i, j: (layer, 0, j0 + k * nj + j))
    br_spec = lambda k: pl.BlockSpec((1, w, tn), lambda i, j: (layer * N_BRANCH + k, 0, j))
    return pl.pallas_call(
        _merge_kernel,
        grid=(m // tm, nj),
        in_specs=[pl.BlockSpec((tm, d), lambda i, j: (i, 0))]
        + [pl.BlockSpec((tm, w), lambda i, j: (i, 0))] * N_BRANCH
        + [gate_spec(k) for k in range(N_BRANCH)]
        + [br_spec(k) for k in range(N_BRANCH)],
        out_specs=pl.BlockSpec((tm, tn), lambda i, j: (i, j)),
        out_shape=jax.ShapeDtypeStruct((m, d), BF16),
        compiler_params=_cparams(("parallel", "arbitrary")),
        name="gated_merge",
    )(h, *ys, *([wgate] * N_BRANCH), *([wbranch] * N_BRANCH))


def _out_kernel(m_ref, w_ref, x_ref, g_ref, o_ref):
    mix = jnp.dot(m_ref[...], w_ref[...], preferred_element_type=F32)
    o_ref[...] = x_ref[...] + _rms(mix, g_ref[...])


def _out_proj(merged, w_o, x2, g, layer, *, tm=256):
    m, d = x2.shape
    return pl.pallas_call(
        _out_kernel,
        grid=(m // tm,),
        in_specs=[
            pl.BlockSpec((tm, d), lambda i: (i, 0)),
            pl.BlockSpec((None, d, d), lambda i: (layer, 0, 0), pipeline_mode=pl.Buffered(1)),
            pl.BlockSpec((tm, d), lambda i: (i, 0)),
            pl.BlockSpec((1, d), lambda i: (0, 0)),
        ],
        out_specs=pl.BlockSpec((tm, d), lambda i: (i, 0)),
        out_shape=jax.ShapeDtypeStruct((m, d), F32),
        compiler_params=_cparams(("parallel",)),
        name="out_proj",
    )(merged, w_o, x2, g)


def _ffn_kernel(x_ref, gpre_ref, w1_ref, w2_ref, gpost_ref, o_ref, h_ref, *, rows):
    j = pl.program_id(1)
    nrb = x_ref.shape[0] // rows

    @pl.when(j == 0)
    def _():
        def body(i, carry):
            r = pl.multiple_of(i * rows, rows)
            h_ref[pl.ds(r, rows), :] = _rms(x_ref[pl.ds(r, rows), :], gpre_ref[...]).astype(h_ref.dtype)
            o_ref[pl.ds(r, rows), :] = jnp.zeros((rows, o_ref.shape[1]), o_ref.dtype)
            return carry
        lax.fori_loop(0, nrb, body, 0)

    a = jnp.dot(h_ref[...], w1_ref[...], preferred_element_type=F32)
    a = jnp.maximum(a, 0.0)
    a = (a * a).astype(BF16)
    nw = 512
    for c in range(o_ref.shape[1] // nw):
        o_ref[:, c * nw:(c + 1) * nw] += jnp.dot(a, w2_ref[:, c * nw:(c + 1) * nw],
                                                 preferred_element_type=F32)

    @pl.when(j == pl.num_programs(1) - 1)
    def _():
        def body(i, carry):
            r = pl.multiple_of(i * rows, rows)
            o_ref[pl.ds(r, rows), :] = (x_ref[pl.ds(r, rows), :]
                                        + _rms(o_ref[pl.ds(r, rows), :], gpost_ref[...]))
            return carry
        lax.fori_loop(0, nrb, body, 0)


def _ffn(x2, gpre, w1, w2, gpost, layer, *, tm=1024, tf=512):
    m, d = x2.shape
    f = w1.shape[2]
    return pl.pallas_call(
        functools.partial(_ffn_kernel, rows=64),
        grid=(m // tm, f // tf),
        in_specs=[
            pl.BlockSpec((tm, d), lambda i, j: (i, 0)),
            pl.BlockSpec((1, d), lambda i, j: (0, 0)),
            pl.BlockSpec((None, d, tf), lambda i, j: (layer, 0, j)),
            pl.BlockSpec((None, tf, d), lambda i, j: (layer, j, 0)),
            pl.BlockSpec((1, d), lambda i, j: (0, 0)),
        ],
        out_specs=pl.BlockSpec((tm, d), lambda i, j: (i, 0)),
        out_shape=jax.ShapeDtypeStruct((m, d), F32),
        scratch_shapes=[pltpu.VMEM((tm, d), BF16)],
        compiler_params=_cparams(("parallel", "arbitrary")),
        name="ffn",
    )(x2, gpre, w1, w2, gpost)


def kernel(x, g_pre_mix, w_in, ssm_a_re, ssm_a_im, ssm_log_dt, ssm_b_re, ssm_b_im, ssm_c_re, ssm_c_im,
           ssm_d, ssm_w_glu, pool_w, pool_scale, conv_w, conv_b, conv_ln_g, conv_ln_b, gmlp_ln_g,
           gmlp_ln_b, gmlp_ws, gmlp_bs, w_branch, w_o, g_post_mix, g_pre_mlp, w_ff1, w_ff2, g_post_mlp):
    nb, seq, d = x.shape
    m = nb * seq
    depth = w_in.shape[0]
    g, p, t = SSM_GROUPS, SSM_GROUP, SSM_T
    nchunk = seq // t
    assert d == D_MODEL and seq % MIX_TC == 0 and nchunk == 1 << SSM_SCAN_STEPS

    cl, bp, kall, lama, lamb = _ssm_prep(ssm_a_re, ssm_a_im, ssm_log_dt, ssm_b_re, ssm_b_im,
                                         ssm_c_re, ssm_c_im)
    toep, bpow, cpow = _ssm_operators(cl, bp, kall)

    w_in_b = _cast_bf16(w_in, 256)
    w_br_b = _cast_bf16(w_branch.reshape(depth, N_BRANCH * W_BRANCH, d), 1024).reshape(
        depth * N_BRANCH, W_BRANCH, d)
    w_o_b = _cast_bf16(w_o, 1024)
    w_ff1_b = _cast_bf16(w_ff1, 256)
    w_ff2_b = _cast_bf16(w_ff2, 1024)

    row = lambda a: a.reshape(1, -1)
    x2 = x.reshape(m, d)
    for l in range(depth):
        gs = slice(l * g, (l + 1) * g)
        h, pm = _norm_proj(x2, row(g_pre_mix[l]), w_in_b, l)

        up = pm[:, OFF_SSM:OFF_SSM + W_BRANCH].reshape(nb, nchunk, t, g, p)
        up = jnp.transpose(up, (3, 1, 0, 2, 4)).reshape(g, nchunk * nb, t * p).astype(BF16)
        yp = _ssm_scan(up, toep[gs], bpow[gs], cpow[gs], lama[gs], lamb[gs], nb=nb)
        yraw = jnp.transpose(yp.reshape(g, nchunk, nb, t, p), (2, 1, 3, 0, 4)).reshape(m, W_BRANCH)

        ys = _mixers(pm, yraw, row(ssm_d[l]), ssm_w_glu[l].astype(BF16), pool_w[l].astype(BF16),
                     row(pool_scale[l]), conv_w[l], row(conv_b[l]), row(conv_ln_g[l]), row(conv_ln_b[l]),
                     row(gmlp_ln_g[l]), row(gmlp_ln_b[l]), gmlp_ws[l], jnp.transpose(gmlp_bs[l]),
                     nb=nb, seq=seq)
        merged = _merge(h, ys, w_in_b, w_br_b, l)
        x2 = _out_proj(merged, w_o_b, x2, row(g_post_mix[l]), l)
        x2 = _ffn(x2, row(g_pre_mlp[l]), w_ff1_b, w_ff2_b, row(g_post_mlp[l]), l)
    return x2.reshape(nb, seq, d)
```

```python
import functools

import jax
import jax.numpy as jnp
from jax import lax
from jax.experimental import pallas as pl
from jax.experimental.pallas import tpu as pltpu

D_MODEL = 2048
DEPTH = 2
N_BRANCH = 4
W_BRANCH = D_MODEL // 4
SSM_GROUP = 16
SSM_GROUPS = W_BRANCH // SSM_GROUP
SSM_STATE = 64
POOL_WINDOWS = (2, 4, 8, 16)
POOL_GW = W_BRANCH // len(POOL_WINDOWS)
CONV_WIDTH = 31
GMLP_CHUNK = 128
GMLP_HEADS = 4
GMLP_HD = W_BRANCH // GMLP_HEADS
D_FF = 4 * D_MODEL
EPS = 1e-6

OFF_SSM = 0
OFF_POOL = OFF_SSM + W_BRANCH
OFF_CONV = OFF_POOL + W_BRANCH
OFF_GMLP = OFF_CONV + 2 * W_BRANCH
OFF_GATE = OFF_GMLP + 2 * W_BRANCH

SSM_T = 16
SSM_LANES = SSM_T * SSM_GROUP
SSM_SCAN_STEPS = 7
CONV_HALO = 32
POOL_HALO = 16
MIX_TC = 256
CONV_RB = 64

VMEM_LIMIT = 56 * 1024 * 1024
BF16 = jnp.bfloat16
F32 = jnp.float32


def _cparams(sem):
    return pltpu.CompilerParams(dimension_semantics=sem, vmem_limit_bytes=VMEM_LIMIT)


def _sigmoid(x):
    return 0.5 * (1.0 + jnp.tanh(0.5 * x))


def _gelu(x):
    c = 0.7978845608028654
    return 0.5 * x * (1.0 + jnp.tanh(c * (x + 0.044715 * (x * x * x))))


def _rms(x, g):
    ms = jnp.mean(x * x, axis=-1, keepdims=True)
    return x * lax.rsqrt(ms + EPS) * g


def _layer_norm(x, g, b):
    mu = jnp.mean(x, axis=-1, keepdims=True)
    xc = x - mu
    var = jnp.mean(xc * xc, axis=-1, keepdims=True)
    return xc * lax.rsqrt(var + EPS) * g + b


def _cast_kernel(w_ref, o_ref):
    o_ref[...] = w_ref[...].astype(o_ref.dtype)


def _cast_bf16(w, rows):
    depth, r, c = w.shape
    spec = pl.BlockSpec((1, rows, c), lambda l, i: (l, i, 0))
    return pl.pallas_call(
        _cast_kernel,
        grid=(depth, r // rows),
        in_specs=[spec],
        out_specs=spec,
        out_shape=jax.ShapeDtypeStruct(w.shape, BF16),
        compiler_params=_cparams(("parallel", "parallel")),
        name="cast_bf16",
    )(w)


def _norm_proj_kernel(x_ref, g_ref, w_ref, h_ref, o_ref, *, rows):
    def body(i, carry):
        r = pl.multiple_of(i * rows, rows)
        h_ref[pl.ds(r, rows), :] = _rms(x_ref[pl.ds(r, rows), :], g_ref[...]).astype(h_ref.dtype)
        return carry
    lax.fori_loop(0, x_ref.shape[0] // rows, body, 0)
    o_ref[...] = jnp.dot(h_ref[...], w_ref[...], preferred_element_type=F32)


def _norm_proj(x2, g, w_in, layer, *, tm=512):
    m, d = x2.shape
    n = OFF_GATE
    return pl.pallas_call(
        functools.partial(_norm_proj_kernel, rows=64),
        grid=(m // tm,),
        in_specs=[
            pl.BlockSpec((tm, d), lambda i: (i, 0)),
            pl.BlockSpec((1, d), lambda i: (0, 0)),
            pl.BlockSpec((None, d, n), lambda i: (layer, 0, 0), pipeline_mode=pl.Buffered(1)),
        ],
        out_specs=[
            pl.BlockSpec((tm, d), lambda i: (i, 0)),
            pl.BlockSpec((tm, n), lambda i: (i, 0)),
        ],
        out_shape=[
            jax.ShapeDtypeStruct((m, d), BF16),
            jax.ShapeDtypeStruct((m, n), F32),
        ],
        compiler_params=_cparams(("parallel",)),
        name="norm_proj",
    )(x2, g, w_in)


def _ssm_prep_kernel(ar_r, ai_r, ldt_r, ar_c, ai_c, ldt_c, cre_ref, cim_ref, bre_ref, bim_ref,
                     cl_ref, bp_ref, k_ref, lama_ref, lamb_ref):
    n = SSM_STATE

    def lam(ar, ai, ldt):
        dt = jnp.exp(ldt)
        mag = jnp.exp(ar * dt)
        ang = ai * dt
        return mag * jnp.cos(ang), mag * jnp.sin(ang)

    lr, li = lam(ar_r[0], ai_r[0], ldt_r[0])
    cre = cre_ref[0]
    cim = cim_ref[0]
    pr = jnp.ones_like(lr)
    pi = jnp.zeros_like(lr)
    for k in range(SSM_T + 1):
        cl_ref[0, k, :, 0:n] = cre * pr - cim * pi
        cl_ref[0, k, :, n:2 * n] = -(cre * pi + cim * pr)
        if k < SSM_T:
            pr, pi = pr * lr - pi * li, pr * li + pi * lr
    qr, qi = pr, pi
    for j in range(8):
        if j < SSM_SCAN_STEPS:
            lama_ref[0, j:j + 1, 0:n] = qr
            lama_ref[0, j:j + 1, n:2 * n] = qr
            lamb_ref[0, j:j + 1, 0:n] = -qi
            lamb_ref[0, j:j + 1, n:2 * n] = qi
            qr, qi = qr * qr - qi * qi, 2.0 * (qr * qi)
        else:
            lama_ref[0, j:j + 1, :] = jnp.zeros((1, 2 * n), F32)
            lamb_ref[0, j:j + 1, :] = jnp.zeros((1, 2 * n), F32)

    ar = ar_c[0]
    ai = ai_c[0]
    lrc, lic = lam(ar, ai, ldt_c[0])
    den = ar * ar + ai * ai
    fr = ((lrc - 1.0) * ar + lic * ai) / den
    fi = (lic * ar - (lrc - 1.0) * ai) / den
    br = bre_ref[0]
    bi = bim_ref[0]
    bbr = fr * br - fi * bi
    bbi = fr * bi + fi * br
    pr = jnp.ones_like(lrc)
    pi = jnp.zeros_like(lrc)
    for k in range(SSM_T):
        s = SSM_T - 1 - k
        bp_ref[0, s, 0:n, :] = pr * bbr - pi * bbi
        bp_ref[0, s, n:2 * n, :] = pr * bbi + pi * bbr
        pr, pi = pr * lrc - pi * lic, pr * lic + pi * lrc

    clmat = cl_ref[0, 0:SSM_T].reshape(SSM_T * SSM_GROUP, 2 * n)
    k_ref[0] = jnp.dot(clmat, bp_ref[0, SSM_T - 1], preferred_element_type=F32,
                       precision=lax.Precision.HIGHEST)


def _ssm_prep(a_re, a_im, log_dt, b_re, b_im, c_re, c_im):
    n, p = SSM_STATE, SSM_GROUP
    gg = a_re.shape[0] * a_re.shape[1]
    ar = a_re.reshape(gg, n)
    ai = a_im.reshape(gg, n)
    ldt = jnp.broadcast_to(log_dt.reshape(gg, 1), (gg, n))
    row = lambda a: a.reshape(gg, 1, n)
    col = lambda a: a.reshape(gg, n, 1)
    rspec = pl.BlockSpec((1, 1, n), lambda g: (g, 0, 0))
    cspec = pl.BlockSpec((1, n, 1), lambda g: (g, 0, 0))
    return pl.pallas_call(
        _ssm_prep_kernel,
        grid=(gg,),
        in_specs=[rspec, rspec, rspec, cspec, cspec, cspec,
                  pl.BlockSpec((1, p, n), lambda g: (g, 0, 0)),
                  pl.BlockSpec((1, p, n), lambda g: (g, 0, 0)),
                  pl.BlockSpec((1, n, p), lambda g: (g, 0, 0)),
                  pl.BlockSpec((1, n, p), lambda g: (g, 0, 0))],
        out_specs=[
            pl.BlockSpec((1, SSM_T + 1, p, 2 * n), lambda g: (g, 0, 0, 0)),
            pl.BlockSpec((1, SSM_T, 2 * n, p), lambda g: (g, 0, 0, 0)),
            pl.BlockSpec((1, SSM_T * p, p), lambda g: (g, 0, 0)),
            pl.BlockSpec((1, 8, 2 * n), lambda g: (g, 0, 0)),
            pl.BlockSpec((1, 8, 2 * n), lambda g: (g, 0, 0)),
        ],
        out_shape=[
            jax.ShapeDtypeStruct((gg, SSM_T + 1, p, 2 * n), F32),
            jax.ShapeDtypeStruct((gg, SSM_T, 2 * n, p), F32),
            jax.ShapeDtypeStruct((gg, SSM_T * p, p), F32),
            jax.ShapeDtypeStruct((gg, 8, 2 * n), F32),
            jax.ShapeDtypeStruct((gg, 8, 2 * n), F32),
        ],
        compiler_params=_cparams(("arbitrary",)),
        name="ssm_prep",
    )(row(ar), row(ai), row(ldt), col(ar), col(ai), col(ldt),
      c_re.reshape(gg, p, n), c_im.reshape(gg, p, n), b_re.reshape(gg, n, p), b_im.reshape(gg, n, p))


def _ssm_operators(cl, bp, kall):
    gg = cl.shape[0]
    n2, p, t = 2 * SSM_STATE, SSM_GROUP, SSM_T
    cpow = jnp.transpose(cl[:, 1:], (0, 3, 1, 2)).reshape(gg, n2, t * p)
    bpow = jnp.transpose(bp, (0, 1, 3, 2)).reshape(gg, t * p, n2)
    kt = kall.reshape(gg, t, p, p)
    s_idx = jnp.arange(t)[:, None]
    t_idx = jnp.arange(t)[None, :]
    lag = jnp.clip(t_idx - s_idx, 0, t - 1)
    k4 = kt[:, lag]
    k4 = jnp.where((t_idx >= s_idx)[None, :, :, None, None], k4, 0.0)
    toep = jnp.transpose(k4, (0, 1, 4, 2, 3)).reshape(gg, t * p, t * p)
    return toep.astype(BF16), bpow.astype(BF16), cpow.astype(BF16)


SSM_GB = 128 // SSM_GROUP
SSM_RB = 16


def _piece_transpose(xs, masks):
    n = len(xs)
    w = 128 // n
    rolled = []
    for k in range(n):
        wk = xs[k % n]
        for a in range(1, n):
            wk = jnp.where(masks[a], xs[(a + k) % n], wk)
        rolled.append(pltpu.roll(wk, w * k, axis=1) if k else wk)
    ys = []
    for a in range(n):
        ya = rolled[(0 - a) % n]
        for b in range(1, n):
            ya = jnp.where(masks[b], rolled[(b - a) % n], ya)
        ys.append(ya)
    return ys


def _ssm_kernel(u_ref, toep_ref, bpow_ref, cpow_ref, lama_ref, lamb_ref, y_ref, up_ref, yp_ref, xs_ref,
                *, nchunk, pad):
    rows = up_ref.shape[1]
    n2 = 2 * SSM_STATE
    t = SSM_T
    lane = lax.broadcasted_iota(jnp.int32, (1, 128), 1)
    masks = [lane // SSM_GROUP == a for a in range(SSM_GB)]
    nhalf = t // SSM_GB

    def to_chunks(rb, carry):
        r = pl.multiple_of(rb * SSM_RB, SSM_RB)
        for hf in range(nhalf):
            xs = [u_ref[pl.ds(r * t + hf * SSM_GB + b, SSM_RB, stride=t), :] for b in range(SSM_GB)]
            ys = _piece_transpose(xs, masks)
            for a in range(SSM_GB):
                up_ref[a, pl.ds(r, SSM_RB), hf * 128:(hf + 1) * 128] = ys[a].astype(up_ref.dtype)
        return carry
    lax.fori_loop(0, rows // SSM_RB, to_chunks, 0)

    cidx = lax.broadcasted_iota(jnp.int32, (rows, 1), 0) % nchunk
    xs_ref[0:pad, :] = jnp.zeros((pad, n2), F32)
    for a in range(SSM_GB):
        u = up_ref[a]
        xs_ref[pad:pad + rows, :] = jnp.dot(u, bpow_ref[a], preferred_element_type=F32)
        for j in range(SSM_SCAN_STEPS):
            sh = 1 << j
            cur = xs_ref[pad:pad + rows, :]
            prev = jnp.where(cidx >= sh, xs_ref[pad - sh:pad - sh + rows, :], 0.0)
            la = lama_ref[a, j:j + 1, :]
            lb = lamb_ref[a, j:j + 1, :]
            xs_ref[pad:pad + rows, :] = cur + prev * la + pltpu.roll(prev, SSM_STATE, axis=1) * lb
        xprev = jnp.where(cidx >= 1, xs_ref[pad - 1:pad - 1 + rows, :], 0.0).astype(BF16)
        yp_ref[a] = (jnp.dot(u, toep_ref[a], preferred_element_type=F32)
                     + jnp.dot(xprev, cpow_ref[a], preferred_element_type=F32))

    def to_tokens(rb, carry):
        r = pl.multiple_of(rb * SSM_RB, SSM_RB)
        for hf in range(nhalf):
            ys = [yp_ref[a, pl.ds(r, SSM_RB), hf * 128:(hf + 1) * 128] for a in range(SSM_GB)]
            xs = _piece_transpose(ys, masks)
            for b in range(SSM_GB):
                y_ref[pl.ds(r * t + hf * SSM_GB + b, SSM_RB, stride=t), :] = xs[b]
        return carry
    lax.fori_loop(0, rows // SSM_RB, to_tokens, 0)


def _ssm_scan(pm, toep, bpow, cpow, lama, lamb, layer, *, nchunk):
    m = pm.shape[0]
    rows = m // SSM_T
    lanes = SSM_LANES
    n2 = 2 * SSM_STATE
    pad = 1 << (SSM_SCAN_STEPS - 1)
    nblk = W_BRANCH // 128
    gspec = lambda a, b: pl.BlockSpec((SSM_GB, a, b), lambda i: (layer * nblk + i, 0, 0))
    return pl.pallas_call(
        functools.partial(_ssm_kernel, nchunk=nchunk, pad=pad),
        grid=(nblk,),
        in_specs=[pl.BlockSpec((m, 128), lambda i: (0, i)),
                  gspec(lanes, lanes), gspec(lanes, n2), gspec(n2, lanes), gspec(8, n2), gspec(8, n2)],
        out_specs=pl.BlockSpec((m, 128), lambda i: (0, i)),
        out_shape=jax.ShapeDtypeStruct((m, W_BRANCH), F32),
        scratch_shapes=[pltpu.VMEM((SSM_GB, rows, lanes), BF16), pltpu.VMEM((SSM_GB, rows, lanes), F32),
                        pltpu.VMEM((pad + rows, n2), F32)],
        compiler_params=_cparams(("parallel",)),
        name="ssm_scan",
    )(pm, toep, bpow, cpow, lama, lamb)


def _mix_kernel(pm_ref, yraw_ref, d_ref, wglu_ref, poolw_ref, pscale_ref, convw_ref, convb_ref,
                clg_ref, clb_ref, glg_ref, glb_ref, ws_ref, bst_ref,
                yssm_ref, ypool_ref, yconv_ref, ygmlp_ref, vbuf, pbuf):
    c = pl.program_id(1)
    tc = pm_ref.shape[0]
    w = W_BRANCH

    @pl.when(c == 0)
    def _():
        vbuf[0:CONV_HALO, :] = jnp.zeros((CONV_HALO, w), F32)
        pbuf[0:POOL_HALO, :] = jnp.zeros((POOL_HALO, w), F32)

    y = yraw_ref[...] + d_ref[...] * pm_ref[:, OFF_SSM:OFF_SSM + w]
    zg = jnp.dot(_gelu(y).astype(BF16), wglu_ref[...], preferred_element_type=F32)
    yssm_ref[...] = (zg[:, :w] * _sigmoid(zg[:, w:])).astype(yssm_ref.dtype)

    pbuf[POOL_HALO:POOL_HALO + tc, :] = pm_ref[:, OFF_POOL:OFF_POOL + w]
    t = c * tc + lax.broadcasted_iota(jnp.int32, (tc, 1), 0)
    for gi, win in enumerate(POOL_WINDOWS):
        cols = slice(gi * POOL_GW, (gi + 1) * POOL_GW)
        tok = pbuf[POOL_HALO:POOL_HALO + tc, cols]
        s = tok
        for j in range(1, win):
            s = s + pbuf[POOL_HALO - j:POOL_HALO - j + tc, cols]
        count = jnp.minimum(t + 1, win).astype(F32)
        pooled = s / count - tok
        mixed = jnp.dot(pooled.astype(BF16), poolw_ref[gi], preferred_element_type=F32)
        ypool_ref[:, cols] = (mixed * pscale_ref[:, cols]).astype(ypool_ref.dtype)
    pbuf[0:POOL_HALO, :] = pbuf[tc:tc + POOL_HALO, :]

    vbuf[CONV_HALO:CONV_HALO + tc, :] = (pm_ref[:, OFF_CONV:OFF_CONV + w]
                                         * _sigmoid(pm_ref[:, OFF_CONV + w:OFF_CONV + 2 * w]))
    base = CONV_HALO - (CONV_WIDTH - 1)
    for rb in range(tc // CONV_RB):
        r0 = rb * CONV_RB
        acc = jnp.broadcast_to(convb_ref[...], (CONV_RB, w))
        for k in range(CONV_WIDTH):
            acc = acc + convw_ref[k:k + 1, :] * vbuf[r0 + base + k:r0 + base + k + CONV_RB, :]
        yn = _layer_norm(acc, clg_ref[...], clb_ref[...])
        yconv_ref[r0:r0 + CONV_RB, :] = (yn * _sigmoid(yn)).astype(yconv_ref.dtype)
    vbuf[0:CONV_HALO, :] = vbuf[tc:tc + CONV_HALO, :]

    ug = _gelu(pm_ref[:, OFF_GMLP:OFF_GMLP + w])
    vg = _layer_norm(_gelu(pm_ref[:, OFF_GMLP + w:OFF_GMLP + 2 * w]), glg_ref[...], glb_ref[...]).astype(BF16)
    ri = lax.broadcasted_iota(jnp.int32, (GMLP_CHUNK, GMLP_CHUNK), 0)
    ci = lax.broadcasted_iota(jnp.int32, (GMLP_CHUNK, GMLP_CHUNK), 1)
    causal = ri >= ci
    for h in range(GMLP_HEADS):
        wsm = jnp.where(causal, ws_ref[h], 0.0).astype(BF16)
        cols = slice(h * GMLP_HD, (h + 1) * GMLP_HD)
        for ch in range(tc // GMLP_CHUNK):
            rws = slice(ch * GMLP_CHUNK, (ch + 1) * GMLP_CHUNK)
            sv = jnp.dot(wsm, vg[rws, cols], preferred_element_type=F32) + bst_ref[:, h:h + 1]
            ygmlp_ref[rws, cols] = (ug[rws, cols] * sv).astype(ygmlp_ref.dtype)


def _mixers(pm, yraw, d, wglu, poolw, pscale, convw, convb, clg, clb, glg, glb, ws, bst, *, nb, seq):
    m = pm.shape[0]
    tc = MIX_TC
    nct = seq // tc
    w = W_BRANCH
    row = lambda width: pl.BlockSpec((tc, width), lambda b, c: (b * nct + c, 0))
    full = lambda a: pl.BlockSpec(a.shape, lambda b, c: (0,) * a.ndim)
    args = (d, wglu, poolw, pscale, convw, convb, clg, clb, glg, glb, ws, bst)
    return pl.pallas_call(
        _mix_kernel,
        grid=(nb, nct),
        in_specs=[row(OFF_GATE), row(w)] + [full(a) for a in args],
        out_specs=[row(w)] * 4,
        out_shape=[jax.ShapeDtypeStruct((m, w), BF16)] * 4,
        scratch_shapes=[pltpu.VMEM((CONV_HALO + tc, w), F32), pltpu.VMEM((POOL_HALO + tc, w), F32)],
        compiler_params=_cparams(("parallel", "arbitrary")),
        name="mixers",
    )(pm, yraw, *args)


def _merge_kernel(h_ref, y0, y1, y2, y3, g0, g1, g2, g3, b0, b1, b2, b3, o_ref):
    ys = (y0, y1, y2, y3)
    wg = (g0, g1, g2, g3)
    wb = (b0, b1, b2, b3)
    h = h_ref[...]
    acc = None
    for k in range(N_BRANCH):
        gate = jnp.dot(h, wg[k][...], preferred_element_type=F32)
        yb = jnp.dot(ys[k][...], wb[k][0], preferred_element_type=F32)
        term = yb * _sigmoid(gate)
        acc = term if acc is None else acc + term
    o_ref[...] = acc.astype(o_ref.dtype)


def _merge(h, ys, wgate, wbranch, layer, *, tm=1024, tn=512):
    m, d = h.shape
    w = W_BRANCH
    nj = d // tn
    j0 = OFF_GATE // tn
    gate_spec = lambda k: pl.BlockSpec((None, d, tn), lambda i, j: (layer, 0, j0 + k * nj + j))
    br_spec = lambda k: pl.BlockSpec((1, w, tn), lambda i, j: (layer * N_BRANCH + k, 0, j))
    return pl.pallas_call(
        _merge_kernel,
        grid=(m // tm, nj),
        in_specs=[pl.BlockSpec((tm, d), lambda i, j: (i, 0))]
        + [pl.BlockSpec((tm, w), lambda i, j: (i, 0))] * N_BRANCH
        + [gate_spec(k) for k in range(N_BRANCH)]
        + [br_spec(k) for k in range(N_BRANCH)],
        out_specs=pl.BlockSpec((tm, tn), lambda i, j: (i, j)),
        out_shape=jax.ShapeDtypeStruct((m, d), BF16),
        compiler_params=_cparams(("parallel", "arbitrary")),
        name="gated_merge",
    )(h, *ys, *([wgate] * N_BRANCH), *([wbranch] * N_BRANCH))


def _out_kernel(m_ref, w_ref, x_ref, g_ref, o_ref):
    mix = jnp.dot(m_ref[...], w_ref[...], preferred_element_type=F32)
    o_ref[...] = x_ref[...] + _rms(mix, g_ref[...])


def _out_proj(merged, w_o, x2, g, layer, *, tm=256):
    m, d = x2.shape
    return pl.pallas_call(
        _out_kernel,
        grid=(m // tm,),
        in_specs=[
            pl.BlockSpec((tm, d), lambda i: (i, 0)),
            pl.BlockSpec((None, d, d), lambda i: (layer, 0, 0), pipeline_mode=pl.Buffered(1)),
            pl.BlockSpec((tm, d), lambda i: (i, 0)),
            pl.BlockSpec((1, d), lambda i: (0, 0)),
        ],
        out_specs=pl.BlockSpec((tm, d), lambda i: (i, 0)),
        out_shape=jax.ShapeDtypeStruct((m, d), F32),
        compiler_params=_cparams(("parallel",)),
        name="out_proj",
    )(merged, w_o, x2, g)


def _ffn_kernel(x_ref, gpre_ref, w1_ref, w2_ref, gpost_ref, o_ref, h_ref, *, rows):
    j = pl.program_id(1)
    nrb = x_ref.shape[0] // rows

    @pl.when(j == 0)
    def _():
        def body(i, carry):
            r = pl.multiple_of(i * rows, rows)
            h_ref[pl.ds(r, rows), :] = _rms(x_ref[pl.ds(r, rows), :], gpre_ref[...]).astype(h_ref.dtype)
            o_ref[pl.ds(r, rows), :] = jnp.zeros((rows, o_ref.shape[1]), o_ref.dtype)
            return carry
        lax.fori_loop(0, nrb, body, 0)

    a = jnp.dot(h_ref[...], w1_ref[...], preferred_element_type=F32)
    a = jnp.maximum(a, 0.0)
    a = (a * a).astype(BF16)
    nw = 512
    for c in range(o_ref.shape[1] // nw):
        o_ref[:, c * nw:(c + 1) * nw] += jnp.dot(a, w2_ref[:, c * nw:(c + 1) * nw],
                                                 preferred_element_type=F32)

    @pl.when(j == pl.num_programs(1) - 1)
    def _():
        def body(i, carry):
            r = pl.multiple_of(i * rows, rows)
            o_ref[pl.ds(r, rows), :] = (x_ref[pl.ds(r, rows), :]
                                        + _rms(o_ref[pl.ds(r, rows), :], gpost_ref[...]))
            return carry
        lax.fori_loop(0, nrb, body, 0)


def _ffn(x2, gpre, w1, w2, gpost, layer, *, tm=1024, tf=512):
    m, d = x2.shape
    f = w1.shape[2]
    return pl.pallas_call(
        functools.partial(_ffn_kernel, rows=64),
        grid=(m // tm, f // tf),
        in_specs=[
            pl.BlockSpec((tm, d), lambda i, j: (i, 0)),
            pl.BlockSpec((1, d), lambda i, j: (0, 0)),
            pl.BlockSpec((None, d, tf), lambda i, j: (layer, 0, j)),
            pl.BlockSpec((None, tf, d), lambda i, j: (layer, j, 0)),
            pl.BlockSpec((1, d), lambda i, j: (0, 0)),
        ],
        out_specs=pl.BlockSpec((tm, d), lambda i, j: (i, 0)),
        out_shape=jax.ShapeDtypeStruct((m, d), F32),
        scratch_shapes=[pltpu.VMEM((tm, d), BF16)],
        compiler_params=_cparams(("parallel", "arbitrary")),
        name="ffn",
    )(x2, gpre, w1, w2, gpost)


def kernel(x, g_pre_mix, w_in, ssm_a_re, ssm_a_im, ssm_log_dt, ssm_b_re, ssm_b_im, ssm_c_re, ssm_c_im,
           ssm_d, ssm_w_glu, pool_w, pool_scale, conv_w, conv_b, conv_ln_g, conv_ln_b, gmlp_ln_g,
           gmlp_ln_b, gmlp_ws, gmlp_bs, w_branch, w_o, g_post_mix, g_pre_mlp, w_ff1, w_ff2, g_post_mlp):
    nb, seq, d = x.shape
    m = nb * seq
    depth = w_in.shape[0]
    g, p, t = SSM_GROUPS, SSM_GROUP, SSM_T
    nchunk = seq // t
    assert d == D_MODEL and seq % MIX_TC == 0 and nchunk == 1 << SSM_SCAN_STEPS

    cl, bp, kall, lama, lamb = _ssm_prep(ssm_a_re, ssm_a_im, ssm_log_dt, ssm_b_re, ssm_b_im,
                                         ssm_c_re, ssm_c_im)
    toep, bpow, cpow = _ssm_operators(cl, bp, kall)

    w_in_b = _cast_bf16(w_in, 256)
    w_br_b = _cast_bf16(w_branch.reshape(depth, N_BRANCH * W_BRANCH, d), 1024).reshape(
        depth * N_BRANCH, W_BRANCH, d)
    w_o_b = _cast_bf16(w_o, 1024)
    w_ff1_b = _cast_bf16(w_ff1, 256)
    w_ff2_b = _cast_bf16(w_ff2, 1024)

    row = lambda a: a.reshape(1, -1)
    x2 = x.reshape(m, d)
    for l in range(depth):
        h, pm = _norm_proj(x2, row(g_pre_mix[l]), w_in_b, l)

        yraw = _ssm_scan(pm, toep, bpow, cpow, lama, lamb, l, nchunk=nchunk)

        ys = _mixers(pm, yraw, row(ssm_d[l]), ssm_w_glu[l].astype(BF16), pool_w[l].astype(BF16),
                     row(pool_scale[l]), conv_w[l], row(conv_b[l]), row(conv_ln_g[l]), row(conv_ln_b[l]),
                     row(gmlp_ln_g[l]), row(gmlp_ln_b[l]), gmlp_ws[l], jnp.transpose(gmlp_bs[l]),
                     nb=nb, seq=seq)
        merged = _merge(h, ys, w_in_b, w_br_b, l)
        x2 = _out_proj(merged, w_o_b, x2, row(g_post_mix[l]), l)
        x2 = _ffn(x2, row(g_pre_mlp[l]), w_ff1_b, w_ff2_b, row(g_post_mlp[l]), l)
    return x2.reshape(nb, seq, d)
```

```python
import functools

import jax
import jax.numpy as jnp
from jax import lax
from jax.experimental import pallas as pl
from jax.experimental.pallas import tpu as pltpu

D_MODEL = 2048
DEPTH = 2
N_BRANCH = 4
W_BRANCH = D_MODEL // 4
SSM_GROUP = 16
SSM_GROUPS = W_BRANCH // SSM_GROUP
SSM_STATE = 64
POOL_WINDOWS = (2, 4, 8, 16)
POOL_GW = W_BRANCH // len(POOL_WINDOWS)
CONV_WIDTH = 31
GMLP_CHUNK = 128
GMLP_HEADS = 4
GMLP_HD = W_BRANCH // GMLP_HEADS
D_FF = 4 * D_MODEL
EPS = 1e-6

OFF_SSM = 0
OFF_POOL = OFF_SSM + W_BRANCH
OFF_CONV = OFF_POOL + W_BRANCH
OFF_GMLP = OFF_CONV + 2 * W_BRANCH
OFF_GATE = OFF_GMLP + 2 * W_BRANCH

SSM_T = 16
SSM_LANES = SSM_T * SSM_GROUP
SSM_SCAN_STEPS = 7
CONV_HALO = 32
POOL_HALO = 16
CONV_RB = 64

VMEM_LIMIT = 56 * 1024 * 1024
BF16 = jnp.bfloat16
F32 = jnp.float32


def _cparams(sem):
    return pltpu.CompilerParams(dimension_semantics=sem, vmem_limit_bytes=VMEM_LIMIT)


def _sigmoid(x):
    return 0.5 * (1.0 + jnp.tanh(0.5 * x))


def _gelu(x):
    c = 0.7978845608028654
    return 0.5 * x * (1.0 + jnp.tanh(c * (x + 0.044715 * (x * x * x))))


def _rms(x, g):
    ms = jnp.mean(x * x, axis=-1, keepdims=True)
    return x * lax.rsqrt(ms + EPS) * g


def _layer_norm(x, g, b):
    mu = jnp.mean(x, axis=-1, keepdims=True)
    xc = x - mu
    var = jnp.mean(xc * xc, axis=-1, keepdims=True)
    return xc * lax.rsqrt(var + EPS) * g + b


def _cast_kernel(w_ref, o_ref):
    o_ref[...] = w_ref[...].astype(o_ref.dtype)


def _cast_bf16(w, rows):
    depth, r, c = w.shape
    spec = pl.BlockSpec((1, rows, c), lambda l, i: (l, i, 0))
    return pl.pallas_call(
        _cast_kernel,
        grid=(depth, r // rows),
        in_specs=[spec],
        out_specs=spec,
        out_shape=jax.ShapeDtypeStruct(w.shape, BF16),
        compiler_params=_cparams(("parallel", "parallel")),
        name="cast_bf16",
    )(w)


def _proj_mix_kernel(x_ref, g_ref, w_ref, poolw_ref, pscale_ref, convw_ref, convb_ref, clg_ref, clb_ref,
                     glg_ref, glb_ref, ws_ref, bst_ref,
                     h_ref, u_ref, ypool_ref, yconv_ref, ygmlp_ref, vbuf, pbuf, sbuf, *, rows, tiles_per_seq):
    c = pl.program_id(0) % tiles_per_seq
    tc = x_ref.shape[0]
    w = W_BRANCH

    @pl.when(c == 0)
    def _():
        vbuf[0:CONV_HALO, :] = jnp.zeros((CONV_HALO, w), F32)
        pbuf[0:POOL_HALO, :] = jnp.zeros((POOL_HALO, w), F32)

    def body(i, carry):
        r = pl.multiple_of(i * rows, rows)
        h_ref[pl.ds(r, rows), :] = _rms(x_ref[pl.ds(r, rows), :], g_ref[...]).astype(h_ref.dtype)
        return carry
    lax.fori_loop(0, tc // rows, body, 0)

    h = h_ref[...]
    proj = lambda off, width: jnp.dot(h, w_ref[:, off:off + width], preferred_element_type=F32)
    u_ref[...] = proj(OFF_SSM, w)

    pbuf[POOL_HALO:POOL_HALO + tc, :] = proj(OFF_POOL, w)
    t = c * tc + lax.broadcasted_iota(jnp.int32, (tc, 1), 0)
    for gi, win in enumerate(POOL_WINDOWS):
        cols = slice(gi * POOL_GW, (gi + 1) * POOL_GW)
        tok = pbuf[POOL_HALO:POOL_HALO + tc, cols]
        s = tok
        for j in range(1, win):
            s = s + pbuf[POOL_HALO - j:POOL_HALO - j + tc, cols]
        count = jnp.minimum(t + 1, win).astype(F32)
        pooled = s / count - tok
        mixed = jnp.dot(pooled.astype(BF16), poolw_ref[gi], preferred_element_type=F32)
        ypool_ref[:, cols] = (mixed * pscale_ref[:, cols]).astype(ypool_ref.dtype)
    pbuf[0:POOL_HALO, :] = pbuf[tc:tc + POOL_HALO, :]

    cv = proj(OFF_CONV, 2 * w)
    vbuf[CONV_HALO:CONV_HALO + tc, :] = cv[:, :w] * _sigmoid(cv[:, w:])
    for rb in range(tc // CONV_RB):
        r0 = rb * CONV_RB
        acc = jnp.broadcast_to(convb_ref[...], (CONV_RB, w))
        for b in range(8):
            na = (CONV_WIDTH - 1 - b) // 8 + 1
            span = 8 * (na - 1)
            lo = CONV_HALO + r0 - b - span
            if b:
                sbuf[b, 0:CONV_RB + span, :] = vbuf[lo:lo + CONV_RB + span, :]
            for a in range(na):
                k = CONV_WIDTH - 1 - (8 * a + b)
                off = span - 8 * a
                tap = (sbuf[b, off:off + CONV_RB, :] if b else vbuf[lo + off:lo + off + CONV_RB, :])
                acc = acc + convw_ref[k:k + 1, :] * tap
        yn = _layer_norm(acc, clg_ref[...], clb_ref[...])
        yconv_ref[r0:r0 + CONV_RB, :] = (yn * _sigmoid(yn)).astype(yconv_ref.dtype)
    vbuf[0:CONV_HALO, :] = vbuf[tc:tc + CONV_HALO, :]

    gm = proj(OFF_GMLP, 2 * w)
    ug = _gelu(gm[:, :w])
    vg = _layer_norm(_gelu(gm[:, w:]), glg_ref[...], glb_ref[...]).astype(BF16)
    ri = lax.broadcasted_iota(jnp.int32, (GMLP_CHUNK, GMLP_CHUNK), 0)
    ci = lax.broadcasted_iota(jnp.int32, (GMLP_CHUNK, GMLP_CHUNK), 1)
    causal = ri >= ci
    for hd in range(GMLP_HEADS):
        wsm = jnp.where(causal, ws_ref[hd], 0.0).astype(BF16)
        cols = slice(hd * GMLP_HD, (hd + 1) * GMLP_HD)
        for ch in range(tc // GMLP_CHUNK):
            rws = slice(ch * GMLP_CHUNK, (ch + 1) * GMLP_CHUNK)
            sv = jnp.dot(wsm, vg[rws, cols], preferred_element_type=F32) + bst_ref[:, hd:hd + 1]
            ygmlp_ref[rws, cols] = (ug[rws, cols] * sv).astype(ygmlp_ref.dtype)


def _proj_mix(x2, g, w_in, layer, poolw, pscale, convw, convb, clg, clb, glg, glb, ws, bst, *, seq, tm=512):
    m, d = x2.shape
    n = OFF_GATE
    w = W_BRANCH
    assert seq % tm == 0 and tm % GMLP_CHUNK == 0 and tm % CONV_RB == 0
    row = lambda width: pl.BlockSpec((tm, width), lambda i: (i, 0))
    full = lambda a: pl.BlockSpec(a.shape, lambda i: (0,) * a.ndim)
    args = (poolw, pscale, convw, convb, clg, clb, glg, glb, ws, bst)
    return pl.pallas_call(
        functools.partial(_proj_mix_kernel, rows=64, tiles_per_seq=seq // tm),
        grid=(m // tm,),
        in_specs=[row(d), full(g),
                  pl.BlockSpec((None, d, n), lambda i: (layer, 0, 0), pipeline_mode=pl.Buffered(1))]
        + [full(a) for a in args],
        out_specs=[row(d), row(w), row(w), row(w), row(w)],
        out_shape=[jax.ShapeDtypeStruct((m, d), BF16), jax.ShapeDtypeStruct((m, w), F32)]
        + [jax.ShapeDtypeStruct((m, w), BF16)] * 3,
        scratch_shapes=[pltpu.VMEM((CONV_HALO + tm, w), F32), pltpu.VMEM((POOL_HALO + tm, w), F32),
                        pltpu.VMEM((8, CONV_RB + 24, w), F32)],
        compiler_params=_cparams(("arbitrary",)),
        name="proj_mix",
    )(x2, g, w_in, *args)


PREP_GB = 8


def _ssm_prep_kernel(ar_r, ai_r, ldt_r, ar_c, ai_c, ldt_c, cre_ref, cim_ref, bre_ref, bim_ref,
                     cl_ref, bp_ref, k_ref, lama_ref, lamb_ref):
    n = SSM_STATE

    def lam(ar, ai, ldt):
        dt = jnp.exp(ldt)
        mag = jnp.exp(ar * dt)
        ang = ai * dt
        return mag * jnp.cos(ang), mag * jnp.sin(ang)

    for e in range(PREP_GB):
        lr, li = lam(ar_r[e], ai_r[e], ldt_r[e])
        cre = cre_ref[e]
        cim = cim_ref[e]
        pr = jnp.ones_like(lr)
        pi = jnp.zeros_like(lr)
        for k in range(SSM_T + 1):
            cl_ref[e, k, :, 0:n] = cre * pr - cim * pi
            cl_ref[e, k, :, n:2 * n] = -(cre * pi + cim * pr)
            if k < SSM_T:
                pr, pi = pr * lr - pi * li, pr * li + pi * lr
        qr, qi = pr, pi
        for j in range(8):
            if j < SSM_SCAN_STEPS:
                lama_ref[e, j:j + 1, 0:n] = qr
                lama_ref[e, j:j + 1, n:2 * n] = qr
                lamb_ref[e, j:j + 1, 0:n] = -qi
                lamb_ref[e, j:j + 1, n:2 * n] = qi
                qr, qi = qr * qr - qi * qi, 2.0 * (qr * qi)
            else:
                lama_ref[e, j:j + 1, :] = jnp.zeros((1, 2 * n), F32)
                lamb_ref[e, j:j + 1, :] = jnp.zeros((1, 2 * n), F32)

        ar = ar_c[e]
        ai = ai_c[e]
        lrc, lic = lam(ar, ai, ldt_c[e])
        den = ar * ar + ai * ai
        fr = ((lrc - 1.0) * ar + lic * ai) / den
        fi = (lic * ar - (lrc - 1.0) * ai) / den
        br = bre_ref[e]
        bi = bim_ref[e]
        bbr = fr * br - fi * bi
        bbi = fr * bi + fi * br
        pr = jnp.ones_like(lrc)
        pi = jnp.zeros_like(lrc)
        for k in range(SSM_T):
            s = SSM_T - 1 - k
            bp_ref[e, s, 0:n, :] = pr * bbr - pi * bbi
            bp_ref[e, s, n:2 * n, :] = pr * bbi + pi * bbr
            pr, pi = pr * lrc - pi * lic, pr * lic + pi * lrc

        clmat = cl_ref[e, 0:SSM_T].reshape(SSM_T * SSM_GROUP, 2 * n)
        k_ref[e] = jnp.dot(clmat, bp_ref[e, SSM_T - 1], preferred_element_type=F32,
                           precision=lax.Precision.HIGHEST)


def _ssm_prep(a_re, a_im, log_dt, b_re, b_im, c_re, c_im):
    n, p = SSM_STATE, SSM_GROUP
    gg = a_re.shape[0] * a_re.shape[1]
    gb = PREP_GB
    ar = a_re.reshape(gg, n)
    ai = a_im.reshape(gg, n)
    ldt = jnp.broadcast_to(log_dt.reshape(gg, 1), (gg, n))
    row = lambda a: a.reshape(gg, 1, n)
    col = lambda a: a.reshape(gg, n, 1)
    rspec = pl.BlockSpec((gb, 1, n), lambda g: (g, 0, 0))
    cspec = pl.BlockSpec((gb, n, 1), lambda g: (g, 0, 0))
    return pl.pallas_call(
        _ssm_prep_kernel,
        grid=(gg // gb,),
        in_specs=[rspec, rspec, rspec, cspec, cspec, cspec,
                  pl.BlockSpec((gb, p, n), lambda g: (g, 0, 0)),
                  pl.BlockSpec((gb, p, n), lambda g: (g, 0, 0)),
                  pl.BlockSpec((gb, n, p), lambda g: (g, 0, 0)),
                  pl.BlockSpec((gb, n, p), lambda g: (g, 0, 0))],
        out_specs=[
            pl.BlockSpec((gb, SSM_T + 1, p, 2 * n), lambda g: (g, 0, 0, 0)),
            pl.BlockSpec((gb, SSM_T, 2 * n, p), lambda g: (g, 0, 0, 0)),
            pl.BlockSpec((gb, SSM_T * p, p), lambda g: (g, 0, 0)),
            pl.BlockSpec((gb, 8, 2 * n), lambda g: (g, 0, 0)),
            pl.BlockSpec((gb, 8, 2 * n), lambda g: (g, 0, 0)),
        ],
        out_shape=[
            jax.ShapeDtypeStruct((gg, SSM_T + 1, p, 2 * n), F32),
            jax.ShapeDtypeStruct((gg, SSM_T, 2 * n, p), F32),
            jax.ShapeDtypeStruct((gg, SSM_T * p, p), F32),
            jax.ShapeDtypeStruct((gg, 8, 2 * n), F32),
            jax.ShapeDtypeStruct((gg, 8, 2 * n), F32),
        ],
        compiler_params=_cparams(("arbitrary",)),
        name="ssm_prep",
    )(row(ar), row(ai), row(ldt), col(ar), col(ai), col(ldt),
      c_re.reshape(gg, p, n), c_im.reshape(gg, p, n), b_re.reshape(gg, n, p), b_im.reshape(gg, n, p))


def _ssm_operators(cl, bp, kall):
    gg = cl.shape[0]
    n2, p, t = 2 * SSM_STATE, SSM_GROUP, SSM_T
    cpow = jnp.transpose(cl[:, 1:], (0, 3, 1, 2)).reshape(gg, n2, t * p)
    bpow = jnp.transpose(bp, (0, 1, 3, 2)).reshape(gg, t * p, n2)
    kt = kall.reshape(gg, t, p, p)
    s_idx = jnp.arange(t)[:, None]
    t_idx = jnp.arange(t)[None, :]
    lag = jnp.clip(t_idx - s_idx, 0, t - 1)
    k4 = kt[:, lag]
    k4 = jnp.where((t_idx >= s_idx)[None, :, :, None, None], k4, 0.0)
    toep = jnp.transpose(k4, (0, 1, 4, 2, 3)).reshape(gg, t * p, t * p)
    return toep.astype(BF16), bpow.astype(BF16), cpow.astype(BF16)


SSM_GB = 128 // SSM_GROUP
SSM_RB = 16


def _piece_transpose(xs, masks):
    n = len(xs)
    w = 128 // n
    rolled = []
    for k in range(n):
        wk = xs[k % n]
        for a in range(1, n):
            wk = jnp.where(masks[a], xs[(a + k) % n], wk)
        rolled.append(pltpu.roll(wk, w * k, axis=1) if k else wk)
    ys = []
    for a in range(n):
        ya = rolled[(0 - a) % n]
        for b in range(1, n):
            ya = jnp.where(masks[b], rolled[(b - a) % n], ya)
        ys.append(ya)
    return ys


def _ssm_kernel(u_ref, toep_ref, bpow_ref, cpow_ref, lama_ref, lamb_ref, y_ref, up_ref, yp_ref, xs_ref,
                *, nchunk, pad):
    rows = up_ref.shape[1]
    n2 = 2 * SSM_STATE
    t = SSM_T
    lane = lax.broadcasted_iota(jnp.int32, (1, 128), 1)
    masks = [lane // SSM_GROUP == a for a in range(SSM_GB)]
    nhalf = t // SSM_GB

    def to_chunks(rb, carry):
        r = pl.multiple_of(rb * SSM_RB, SSM_RB)
        for hf in range(nhalf):
            xs = [u_ref[pl.ds(r * t + hf * SSM_GB + b, SSM_RB, stride=t), :] for b in range(SSM_GB)]
            ys = _piece_transpose(xs, masks)
            for a in range(SSM_GB):
                up_ref[a, pl.ds(r, SSM_RB), hf * 128:(hf + 1) * 128] = ys[a].astype(up_ref.dtype)
        return carry
    lax.fori_loop(0, rows // SSM_RB, to_chunks, 0, unroll=2)

    cidx = lax.broadcasted_iota(jnp.int32, (rows, 1), 0) % nchunk
    xs_ref[0:pad, :] = jnp.zeros((pad, n2), F32)
    for a in range(SSM_GB):
        u = up_ref[a]
        xs_ref[pad:pad + rows, :] = jnp.dot(u, bpow_ref[a], preferred_element_type=F32)
        for j in range(SSM_SCAN_STEPS):
            sh = 1 << j
            cur = xs_ref[pad:pad + rows, :]
            prev = jnp.where(cidx >= sh, xs_ref[pad - sh:pad - sh + rows, :], 0.0)
            la = lama_ref[a, j:j + 1, :]
            lb = lamb_ref[a, j:j + 1, :]
            xs_ref[pad:pad + rows, :] = cur + prev * la + pltpu.roll(prev, SSM_STATE, axis=1) * lb
        xprev = jnp.where(cidx >= 1, xs_ref[pad - 1:pad - 1 + rows, :], 0.0).astype(BF16)
        yp_ref[a] = (jnp.dot(u, toep_ref[a], preferred_element_type=F32)
                     + jnp.dot(xprev, cpow_ref[a], preferred_element_type=F32))

    def to_tokens(rb, carry):
        r = pl.multiple_of(rb * 8, 8)
        for hf in range(nhalf):
            ys = [yp_ref[a, pl.ds(r, 8), hf * 128:(hf + 1) * 128] for a in range(SSM_GB)]
            xs = _piece_transpose(ys, masks)
            for b in range(SSM_GB):
                y_ref[pl.ds(r * t + hf * SSM_GB + b, 8, stride=t), :] = xs[b]
        return carry
    lax.fori_loop(0, rows // 8, to_tokens, 0, unroll=2)


def _ssm_scan(pm, toep, bpow, cpow, lama, lamb, layer, *, nchunk):
    m = pm.shape[0]
    rows = m // SSM_T
    lanes = SSM_LANES
    n2 = 2 * SSM_STATE
    pad = 1 << (SSM_SCAN_STEPS - 1)
    nblk = W_BRANCH // 128
    gspec = lambda a, b: pl.BlockSpec((SSM_GB, a, b), lambda i: (layer * nblk + i, 0, 0))
    return pl.pallas_call(
        functools.partial(_ssm_kernel, nchunk=nchunk, pad=pad),
        grid=(nblk,),
        in_specs=[pl.BlockSpec((m, 128), lambda i: (0, i)),
                  gspec(lanes, lanes), gspec(lanes, n2), gspec(n2, lanes), gspec(8, n2), gspec(8, n2)],
        out_specs=pl.BlockSpec((m, 128), lambda i: (0, i)),
        out_shape=jax.ShapeDtypeStruct((m, W_BRANCH), F32),
        scratch_shapes=[pltpu.VMEM((SSM_GB, rows, lanes), BF16), pltpu.VMEM((SSM_GB, rows, lanes), F32),
                        pltpu.VMEM((pad + rows, n2), F32)],
        compiler_params=_cparams(("parallel",)),
        name="ssm_scan",
    )(pm, toep, bpow, cpow, lama, lamb)


def _ssm_post_kernel(yraw_ref, u_ref, d_ref, wglu_ref, o_ref):
    w = W_BRANCH
    y = yraw_ref[...] + d_ref[...] * u_ref[...]
    zg = jnp.dot(_gelu(y).astype(BF16), wglu_ref[...], preferred_element_type=F32)
    o_ref[...] = (zg[:, :w] * _sigmoid(zg[:, w:])).astype(o_ref.dtype)


def _ssm_post(yraw, u, d, wglu, *, tm=512):
    m, w = yraw.shape
    row = pl.BlockSpec((tm, w), lambda i: (i, 0))
    return pl.pallas_call(
        _ssm_post_kernel,
        grid=(m // tm,),
        in_specs=[row, row, pl.BlockSpec((1, w), lambda i: (0, 0)), pl.BlockSpec(wglu.shape, lambda i: (0, 0))],
        out_specs=row,
        out_shape=jax.ShapeDtypeStruct((m, w), BF16),
        compiler_params=_cparams(("parallel",)),
        name="ssm_post",
    )(yraw, u, d, wglu)


def _merge_kernel(h_ref, y0, y1, y2, y3, g0, g1, g2, g3, b0, b1, b2, b3, o_ref):
    ys = (y0, y1, y2, y3)
    wg = (g0, g1, g2, g3)
    wb = (b0, b1, b2, b3)
    h = h_ref[...]
    acc = None
    for k in range(N_BRANCH):
        gate = jnp.dot(h, wg[k][...], preferred_element_type=F32)
        yb = jnp.dot(ys[k][...], wb[k][0], preferred_element_type=F32)
        term = yb * _sigmoid(gate)
        acc = term if acc is None else acc + term
    o_ref[...] = acc.astype(o_ref.dtype)


def _merge(h, ys, wgate, wbranch, layer, *, tm=1024, tn=512):
    m, d = h.shape
    w = W_BRANCH
    nj = d // tn
    j0 = OFF_GATE // tn
    gate_spec = lambda k: pl.BlockSpec((None, d, tn), lambda i, j: (layer, 0, j0 + k * nj + j))
    br_spec = lambda k: pl.BlockSpec((1, w, tn), lambda i, j: (layer * N_BRANCH + k, 0, j))
    return pl.pallas_call(
        _merge_kernel,
        grid=(m // tm, nj),
        in_specs=[pl.BlockSpec((tm, d), lambda i, j: (i, 0))]
        + [pl.BlockSpec((tm, w), lambda i, j: (i, 0))] * N_BRANCH
        + [gate_spec(k) for k in range(N_BRANCH)]
        + [br_spec(k) for k in range(N_BRANCH)],
        out_specs=pl.BlockSpec((tm, tn), lambda i, j: (i, j)),
        out_shape=jax.ShapeDtypeStruct((m, d), BF16),
        compiler_params=_cparams(("parallel", "arbitrary")),
        name="gated_merge",
    )(h, *ys, *([wgate] * N_BRANCH), *([wbranch] * N_BRANCH))


def _out_kernel(m_ref, w_ref, x_ref, g_ref, o_ref):
    mix = jnp.dot(m_ref[...], w_ref[...], preferred_element_type=F32)
    o_ref[...] = x_ref[...] + _rms(mix, g_ref[...])


def _out_proj(merged, w_o, x2, g, layer, *, tm=256):
    m, d = x2.shape
    return pl.pallas_call(
        _out_kernel,
        grid=(m // tm,),
        in_specs=[
            pl.BlockSpec((tm, d), lambda i: (i, 0)),
            pl.BlockSpec((None, d, d), lambda i: (layer, 0, 0), pipeline_mode=pl.Buffered(1)),
            pl.BlockSpec((tm, d), lambda i: (i, 0)),
            pl.BlockSpec((1, d), lambda i: (0, 0)),
        ],
        out_specs=pl.BlockSpec((tm, d), lambda i: (i, 0)),
        out_shape=jax.ShapeDtypeStruct((m, d), F32),
        compiler_params=_cparams(("parallel",)),
        name="out_proj",
    )(merged, w_o, x2, g)


def _ffn_kernel(x_ref, gpre_ref, w1_ref, w2_ref, gpost_ref, o_ref, h_ref, *, rows):
    j = pl.program_id(1)
    nrb = x_ref.shape[0] // rows

    @pl.when(j == 0)
    def _():
        def body(i, carry):
            r = pl.multiple_of(i * rows, rows)
            h_ref[pl.ds(r, rows), :] = _rms(x_ref[pl.ds(r, rows), :], gpre_ref[...]).astype(h_ref.dtype)
            o_ref[pl.ds(r, rows), :] = jnp.zeros((rows, o_ref.shape[1]), o_ref.dtype)
            return carry
        lax.fori_loop(0, nrb, body, 0)

    a = jnp.dot(h_ref[...], w1_ref[...], preferred_element_type=F32)
    a = jnp.maximum(a, 0.0)
    a = (a * a).astype(BF16)
    nw = 512
    for c in range(o_ref.shape[1] // nw):
        o_ref[:, c * nw:(c + 1) * nw] += jnp.dot(a, w2_ref[:, c * nw:(c + 1) * nw],
                                                 preferred_element_type=F32)

    @pl.when(j == pl.num_programs(1) - 1)
    def _():
        def body(i, carry):
            r = pl.multiple_of(i * rows, rows)
            o_ref[pl.ds(r, rows), :] = (x_ref[pl.ds(r, rows), :]
                                        + _rms(o_ref[pl.ds(r, rows), :], gpost_ref[...]))
            return carry
        lax.fori_loop(0, nrb, body, 0)


def _ffn(x2, gpre, w1, w2, gpost, layer, *, tm=1024, tf=512):
    m, d = x2.shape
    f = w1.shape[2]
    return pl.pallas_call(
        functools.partial(_ffn_kernel, rows=64),
        grid=(m // tm, f // tf),
        in_specs=[
            pl.BlockSpec((tm, d), lambda i, j: (i, 0)),
            pl.BlockSpec((1, d), lambda i, j: (0, 0)),
            pl.BlockSpec((None, d, tf), lambda i, j: (layer, 0, j)),
            pl.BlockSpec((None, tf, d), lambda i, j: (layer, j, 0)),
            pl.BlockSpec((1, d), lambda i, j: (0, 0)),
        ],
        out_specs=pl.BlockSpec((tm, d), lambda i, j: (i, 0)),
        out_shape=jax.ShapeDtypeStruct((m, d), F32),
        scratch_shapes=[pltpu.VMEM((tm, d), BF16)],
        compiler_params=_cparams(("parallel", "arbitrary")),
        name="ffn",
    )(x2, gpre, w1, w2, gpost)


def kernel(x, g_pre_mix, w_in, ssm_a_re, ssm_a_im, ssm_log_dt, ssm_b_re, ssm_b_im, ssm_c_re, ssm_c_im,
           ssm_d, ssm_w_glu, pool_w, pool_scale, conv_w, conv_b, conv_ln_g, conv_ln_b, gmlp_ln_g,
           gmlp_ln_b, gmlp_ws, gmlp_bs, w_branch, w_o, g_post_mix, g_pre_mlp, w_ff1, w_ff2, g_post_mlp):
    nb, seq, d = x.shape
    m = nb * seq
    depth = w_in.shape[0]
    g, p, t = SSM_GROUPS, SSM_GROUP, SSM_T
    nchunk = seq // t
    assert d == D_MODEL and nchunk == 1 << SSM_SCAN_STEPS

    cl, bp, kall, lama, lamb = _ssm_prep(ssm_a_re, ssm_a_im, ssm_log_dt, ssm_b_re, ssm_b_im,
                                         ssm_c_re, ssm_c_im)
    toep, bpow, cpow = _ssm_operators(cl, bp, kall)

    w_in_b = _cast_bf16(w_in, 256)
    w_br_b = _cast_bf16(w_branch.reshape(depth, N_BRANCH * W_BRANCH, d), 1024).reshape(
        depth * N_BRANCH, W_BRANCH, d)
    w_o_b = _cast_bf16(w_o, 1024)
    w_ff1_b = _cast_bf16(w_ff1, 256)
    w_ff2_b = _cast_bf16(w_ff2, 1024)

    row = lambda a: a.reshape(1, -1)
    x2 = x.reshape(m, d)
    for l in range(depth):
        h, u, y_pool, y_conv, y_gmlp = _proj_mix(
            x2, row(g_pre_mix[l]), w_in_b, l, pool_w[l].astype(BF16), row(pool_scale[l]), conv_w[l],
            row(conv_b[l]), row(conv_ln_g[l]), row(conv_ln_b[l]), row(gmlp_ln_g[l]), row(gmlp_ln_b[l]),
            gmlp_ws[l], jnp.transpose(gmlp_bs[l]), seq=seq)
        yraw = _ssm_scan(u, toep, bpow, cpow, lama, lamb, l, nchunk=nchunk)
        y_ssm = _ssm_post(yraw, u, row(ssm_d[l]), ssm_w_glu[l].astype(BF16))
        merged = _merge(h, (y_ssm, y_pool, y_conv, y_gmlp), w_in_b, w_br_b, l)
        x2 = _out_proj(merged, w_o_b, x2, row(g_post_mix[l]), l)
        x2 = _ffn(x2, row(g_pre_mlp[l]), w_ff1_b, w_ff2_b, row(g_post_mlp[l]), l)
    return x2.reshape(nb, seq, d)
```

```python
import functools

import jax
import jax.numpy as jnp
from jax import lax
from jax.experimental import pallas as pl
from jax.experimental.pallas import tpu as pltpu

D_MODEL = 2048
DEPTH = 2
N_BRANCH = 4
W_BRANCH = D_MODEL // 4
SSM_GROUP = 16
SSM_GROUPS = W_BRANCH // SSM_GROUP
SSM_STATE = 64
POOL_WINDOWS = (2, 4, 8, 16)
POOL_GW = W_BRANCH // len(POOL_WINDOWS)
CONV_WIDTH = 31
GMLP_CHUNK = 128
GMLP_HEADS = 4
GMLP_HD = W_BRANCH // GMLP_HEADS
D_FF = 4 * D_MODEL
EPS = 1e-6

OFF_SSM = 0
OFF_POOL = OFF_SSM + W_BRANCH
OFF_CONV = OFF_POOL + W_BRANCH
OFF_GMLP = OFF_CONV + 2 * W_BRANCH
OFF_GATE = OFF_GMLP + 2 * W_BRANCH

SSM_T = 16
SSM_LANES = SSM_T * SSM_GROUP
SSM_SCAN_STEPS = 7
CONV_HALO = 32
POOL_HALO = 16
CONV_RB = 64

VMEM_LIMIT = 56 * 1024 * 1024
BF16 = jnp.bfloat16
F32 = jnp.float32


def _cparams(sem):
    return pltpu.CompilerParams(dimension_semantics=sem, vmem_limit_bytes=VMEM_LIMIT)


def _sigmoid(x):
    return 0.5 * (1.0 + jnp.tanh(0.5 * x))


def _gelu(x):
    c = 0.7978845608028654
    return 0.5 * x * (1.0 + jnp.tanh(c * (x + 0.044715 * (x * x * x))))


def _rms(x, g):
    ms = jnp.mean(x * x, axis=-1, keepdims=True)
    return x * lax.rsqrt(ms + EPS) * g


def _layer_norm(x, g, b):
    mu = jnp.mean(x, axis=-1, keepdims=True)
    xc = x - mu
    var = jnp.mean(xc * xc, axis=-1, keepdims=True)
    return xc * lax.rsqrt(var + EPS) * g + b


def _cast_kernel(w_ref, o_ref):
    o_ref[...] = w_ref[...].astype(o_ref.dtype)


def _cast_bf16(w, rows):
    depth, r, c = w.shape
    spec = pl.BlockSpec((1, rows, c), lambda l, i: (l, i, 0))
    return pl.pallas_call(
        _cast_kernel,
        grid=(depth, r // rows),
        in_specs=[spec],
        out_specs=spec,
        out_shape=jax.ShapeDtypeStruct(w.shape, BF16),
        compiler_params=_cparams(("parallel", "parallel")),
        name="cast_bf16",
    )(w)


def _proj_mix_kernel(x_ref, g_ref, w_ref, poolw_ref, pscale_ref, convw_ref, convb_ref, clg_ref, clb_ref,
                     glg_ref, glb_ref, ws_ref, bst_ref,
                     h_ref, u_ref, ypool_ref, yconv_ref, ygmlp_ref, vbuf, pbuf, sbuf, *, rows, tiles_per_seq):
    c = pl.program_id(0) % tiles_per_seq
    tc = x_ref.shape[0]
    w = W_BRANCH

    @pl.when(c == 0)
    def _():
        vbuf[0:CONV_HALO, :] = jnp.zeros((CONV_HALO, w), F32)
        pbuf[0:POOL_HALO, :] = jnp.zeros((POOL_HALO, w), F32)

    def body(i, carry):
        r = pl.multiple_of(i * rows, rows)
        h_ref[pl.ds(r, rows), :] = _rms(x_ref[pl.ds(r, rows), :], g_ref[...]).astype(h_ref.dtype)
        return carry
    lax.fori_loop(0, tc // rows, body, 0)

    h = h_ref[...]
    proj = lambda off, width: jnp.dot(h, w_ref[:, off:off + width], preferred_element_type=F32)
    u_ref[...] = proj(OFF_SSM, w)

    pbuf[POOL_HALO:POOL_HALO + tc, :] = proj(OFF_POOL, w)
    t = c * tc + lax.broadcasted_iota(jnp.int32, (tc, 1), 0)
    for gi, win in enumerate(POOL_WINDOWS):
        cols = slice(gi * POOL_GW, (gi + 1) * POOL_GW)
        tok = pbuf[POOL_HALO:POOL_HALO + tc, cols]
        s = tok
        for j in range(1, win):
            s = s + pbuf[POOL_HALO - j:POOL_HALO - j + tc, cols]
        count = jnp.minimum(t + 1, win).astype(F32)
        pooled = s / count - tok
        mixed = jnp.dot(pooled.astype(BF16), poolw_ref[gi], preferred_element_type=F32)
        ypool_ref[:, cols] = (mixed * pscale_ref[:, cols]).astype(ypool_ref.dtype)
    pbuf[0:POOL_HALO, :] = pbuf[tc:tc + POOL_HALO, :]

    cv = proj(OFF_CONV, 2 * w)
    vbuf[CONV_HALO:CONV_HALO + tc, :] = cv[:, :w] * _sigmoid(cv[:, w:])
    span = CONV_HALO - 8
    for b in range(1, 8):
        sbuf[b - 1] = vbuf[8 - b:8 - b + span + tc, :]
    for rb in range(tc // CONV_RB):
        r0 = rb * CONV_RB
        acc = jnp.broadcast_to(convb_ref[...], (CONV_RB, w))
        for b in range(8):
            for a in range((CONV_WIDTH - 1 - b) // 8 + 1):
                k = CONV_WIDTH - 1 - (8 * a + b)
                lo = span - 8 * a + r0
                tap = sbuf[b - 1, lo:lo + CONV_RB, :] if b else vbuf[8 + lo:8 + lo + CONV_RB, :]
                wk = convw_ref[k]
                acc = acc + (tap.reshape(CONV_RB // 8, 8, w) * wk[None]).reshape(CONV_RB, w)
        yn = _layer_norm(acc, clg_ref[...], clb_ref[...])
        yconv_ref[r0:r0 + CONV_RB, :] = (yn * _sigmoid(yn)).astype(yconv_ref.dtype)
    vbuf[0:CONV_HALO, :] = vbuf[tc:tc + CONV_HALO, :]

    gm = proj(OFF_GMLP, 2 * w)
    ug = _gelu(gm[:, :w])
    vg = _layer_norm(_gelu(gm[:, w:]), glg_ref[...], glb_ref[...]).astype(BF16)
    ri = lax.broadcasted_iota(jnp.int32, (GMLP_CHUNK, GMLP_CHUNK), 0)
    ci = lax.broadcasted_iota(jnp.int32, (GMLP_CHUNK, GMLP_CHUNK), 1)
    causal = ri >= ci
    for hd in range(GMLP_HEADS):
        wsm = jnp.where(causal, ws_ref[hd], 0.0).astype(BF16)
        cols = slice(hd * GMLP_HD, (hd + 1) * GMLP_HD)
        for ch in range(tc // GMLP_CHUNK):
            rws = slice(ch * GMLP_CHUNK, (ch + 1) * GMLP_CHUNK)
            sv = jnp.dot(wsm, vg[rws, cols], preferred_element_type=F32) + bst_ref[:, hd:hd + 1]
            ygmlp_ref[rws, cols] = (ug[rws, cols] * sv).astype(ygmlp_ref.dtype)


def _proj_mix(x2, g, w_in, layer, poolw, pscale, convw, convb, clg, clb, glg, glb, ws, bst, *, seq, tm=512):
    m, d = x2.shape
    n = OFF_GATE
    w = W_BRANCH
    assert seq % tm == 0 and tm % GMLP_CHUNK == 0 and tm % CONV_RB == 0
    row = lambda width: pl.BlockSpec((tm, width), lambda i: (i, 0))
    full = lambda a: pl.BlockSpec(a.shape, lambda i: (0,) * a.ndim)
    args = (poolw, pscale, convw, convb, clg, clb, glg, glb, ws, bst)
    return pl.pallas_call(
        functools.partial(_proj_mix_kernel, rows=64, tiles_per_seq=seq // tm),
        grid=(m // tm,),
        in_specs=[row(d), full(g),
                  pl.BlockSpec((None, d, n), lambda i: (layer, 0, 0), pipeline_mode=pl.Buffered(1))]
        + [full(a) for a in args],
        out_specs=[row(d), row(w), row(w), row(w), row(w)],
        out_shape=[jax.ShapeDtypeStruct((m, d), BF16), jax.ShapeDtypeStruct((m, w), F32)]
        + [jax.ShapeDtypeStruct((m, w), BF16)] * 3,
        scratch_shapes=[pltpu.VMEM((CONV_HALO + tm, w), F32), pltpu.VMEM((POOL_HALO + tm, w), F32),
                        pltpu.VMEM((7, CONV_HALO - 8 + tm, w), F32)],
        compiler_params=_cparams(("arbitrary",)),
        name="proj_mix",
    )(x2, g, w_in, *args)


PREP_GB = 8


def _ssm_prep_kernel(ar_ref, ai_ref, ldt_ref, cre_ref, cim_ref, bre_ref, bim_ref,
                     toep_ref, bpow_ref, cpow_ref, lama_ref, lamb_ref, cl_s, bl_s):
    n = SSM_STATE
    p = SSM_GROUP
    lanes = SSM_LANES
    lane = lax.broadcasted_iota(jnp.int32, (1, lanes), 1)
    for e in range(PREP_GB):
        ar = ar_ref[e]
        ai = ai_ref[e]
        dt = jnp.exp(ldt_ref[e])
        mag = jnp.exp(ar * dt)
        ang = ai * dt
        lr = mag * jnp.cos(ang)
        li = mag * jnp.sin(ang)
        den = ar * ar + ai * ai
        fr = ((lr - 1.0) * ar + li * ai) / den
        fi = (li * ar - (lr - 1.0) * ai) / den
        cre = cre_ref[e]
        cim = cim_ref[e]
        br = bre_ref[e]
        bi = bim_ref[e]
        bbr = fr * br - fi * bi
        bbi = fr * bi + fi * br
        pr = jnp.ones_like(lr)
        pi = jnp.zeros_like(lr)
        for k in range(SSM_T + 1):
            cl_s[k * p:(k + 1) * p, 0:n] = cre * pr - cim * pi
            cl_s[k * p:(k + 1) * p, n:2 * n] = -(cre * pi + cim * pr)
            if k < SSM_T:
                s = SSM_T - 1 - k
                bl_s[s * p:(s + 1) * p, 0:n] = pr * bbr - pi * bbi
                bl_s[s * p:(s + 1) * p, n:2 * n] = pr * bbi + pi * bbr
                pr, pi = pr * lr - pi * li, pr * li + pi * lr
        qr, qi = pr, pi
        for j in range(8):
            if j < SSM_SCAN_STEPS:
                lama_ref[e, j:j + 1, 0:n] = qr
                lama_ref[e, j:j + 1, n:2 * n] = qr
                lamb_ref[e, j:j + 1, 0:n] = -qi
                lamb_ref[e, j:j + 1, n:2 * n] = qi
                qr, qi = qr * qr - qi * qi, 2.0 * (qr * qi)
            else:
                lama_ref[e, j:j + 1, :] = jnp.zeros((1, 2 * n), F32)
                lamb_ref[e, j:j + 1, :] = jnp.zeros((1, 2 * n), F32)
        cpow_ref[e] = cl_s[p:(SSM_T + 1) * p, :].astype(cpow_ref.dtype)
        bpow_ref[e] = bl_s[...].astype(bpow_ref.dtype)
        bbcat = bl_s[(SSM_T - 1) * p:SSM_T * p, :]
        kt = lax.dot_general(bbcat, cl_s[0:SSM_T * p, :], (((1,), (1,)), ((), ())),
                             preferred_element_type=F32, precision=lax.Precision.HIGHEST)
        for s in range(SSM_T):
            blk = pltpu.roll(kt, s * p, axis=1) if s else kt
            toep_ref[e, s * p:(s + 1) * p, :] = jnp.where(lane >= s * p, blk, 0.0).astype(toep_ref.dtype)


def _ssm_prep(a_re, a_im, log_dt, b_re, b_im, c_re, c_im):
    n, p = SSM_STATE, SSM_GROUP
    gg = a_re.shape[0] * a_re.shape[1]
    gb = PREP_GB
    lanes = SSM_LANES
    row = lambda a: a.reshape(gg, 1, n)
    ldt = jnp.broadcast_to(log_dt.reshape(gg, 1), (gg, n))
    bt = lambda b: jnp.transpose(b.reshape(gg, n, p), (0, 2, 1))
    rspec = pl.BlockSpec((gb, 1, n), lambda g: (g, 0, 0))
    pspec = pl.BlockSpec((gb, p, n), lambda g: (g, 0, 0))
    ospec = lambda a, b: pl.BlockSpec((gb, a, b), lambda g: (g, 0, 0))
    return pl.pallas_call(
        _ssm_prep_kernel,
        grid=(gg // gb,),
        in_specs=[rspec, rspec, rspec, pspec, pspec, pspec, pspec],
        out_specs=[ospec(lanes, lanes), ospec(lanes, 2 * n), ospec(lanes, 2 * n), ospec(8, 2 * n), ospec(8, 2 * n)],
        out_shape=[
            jax.ShapeDtypeStruct((gg, lanes, lanes), BF16),
            jax.ShapeDtypeStruct((gg, lanes, 2 * n), BF16),
            jax.ShapeDtypeStruct((gg, lanes, 2 * n), BF16),
            jax.ShapeDtypeStruct((gg, 8, 2 * n), F32),
            jax.ShapeDtypeStruct((gg, 8, 2 * n), F32),
        ],
        scratch_shapes=[pltpu.VMEM(((SSM_T + 1) * p, 2 * n), F32), pltpu.VMEM((SSM_T * p, 2 * n), F32)],
        compiler_params=_cparams(("arbitrary",)),
        name="ssm_prep",
    )(row(a_re), row(a_im), row(ldt), c_re.reshape(gg, p, n), c_im.reshape(gg, p, n), bt(b_re), bt(b_im))


SSM_GB = 128 // SSM_GROUP
SSM_RB = 16


def _piece_transpose(xs, masks):
    n = len(xs)
    w = 128 // n
    rolled = []
    for k in range(n):
        wk = xs[k % n]
        for a in range(1, n):
            wk = jnp.where(masks[a], xs[(a + k) % n], wk)
        rolled.append(pltpu.roll(wk, w * k, axis=1) if k else wk)
    ys = []
    for a in range(n):
        ya = rolled[(0 - a) % n]
        for b in range(1, n):
            ya = jnp.where(masks[b], rolled[(b - a) % n], ya)
        ys.append(ya)
    return ys


def _ssm_kernel(u_ref, toep_ref, bpow_ref, cpow_ref, lama_ref, lamb_ref, y_ref, up_ref, yp_ref, xs_ref,
                *, nchunk, pad):
    rows = up_ref.shape[1]
    n2 = 2 * SSM_STATE
    t = SSM_T
    lane = lax.broadcasted_iota(jnp.int32, (1, 128), 1)
    masks = [lane // SSM_GROUP == a for a in range(SSM_GB)]
    nhalf = t // SSM_GB

    def to_chunks(rb, carry):
        r = pl.multiple_of(rb * SSM_RB, SSM_RB)
        for hf in range(nhalf):
            xs = [u_ref[pl.ds(r * t + hf * SSM_GB + b, SSM_RB, stride=t), :] for b in range(SSM_GB)]
            ys = _piece_transpose(xs, masks)
            for a in range(SSM_GB):
                up_ref[a, pl.ds(r, SSM_RB), hf * 128:(hf + 1) * 128] = ys[a].astype(up_ref.dtype)
        return carry
    lax.fori_loop(0, rows // SSM_RB, to_chunks, 0, unroll=2)

    cidx = lax.broadcasted_iota(jnp.int32, (rows, 1), 0) % nchunk
    xs_ref[0:pad, :] = jnp.zeros((pad, n2), F32)
    for a in range(SSM_GB):
        u = up_ref[a]
        xs_ref[pad:pad + rows, :] = jnp.dot(u, bpow_ref[a], preferred_element_type=F32)
        for j in range(SSM_SCAN_STEPS):
            sh = 1 << j
            cur = xs_ref[pad:pad + rows, :]
            prev = jnp.where(cidx >= sh, xs_ref[pad - sh:pad - sh + rows, :], 0.0)
            la = lama_ref[a, j:j + 1, :]
            lb = lamb_ref[a, j:j + 1, :]
            xs_ref[pad:pad + rows, :] = cur + prev * la + pltpu.roll(prev, SSM_STATE, axis=1) * lb
        xprev = jnp.where(cidx >= 1, xs_ref[pad - 1:pad - 1 + rows, :], 0.0).astype(BF16)
        yp_ref[a] = (jnp.dot(u, toep_ref[a], preferred_element_type=F32)
                     + lax.dot_general(xprev, cpow_ref[a], (((1,), (1,)), ((), ())),
                                       preferred_element_type=F32))

    def to_tokens(rb, carry):
        r = pl.multiple_of(rb * 8, 8)
        for hf in range(nhalf):
            ys = [yp_ref[a, pl.ds(r, 8), hf * 128:(hf + 1) * 128] for a in range(SSM_GB)]
            xs = _piece_transpose(ys, masks)
            for b in range(SSM_GB):
                y_ref[pl.ds(r * t + hf * SSM_GB + b, 8, stride=t), :] = xs[b]
        return carry
    lax.fori_loop(0, rows // 8, to_tokens, 0, unroll=2)


def _ssm_scan(pm, toep, bpow, cpow, lama, lamb, layer, *, nchunk):
    m = pm.shape[0]
    rows = m // SSM_T
    lanes = SSM_LANES
    n2 = 2 * SSM_STATE
    pad = 1 << (SSM_SCAN_STEPS - 1)
    nblk = W_BRANCH // 128
    gspec = lambda a, b: pl.BlockSpec((SSM_GB, a, b), lambda i: (layer * nblk + i, 0, 0))
    return pl.pallas_call(
        functools.partial(_ssm_kernel, nchunk=nchunk, pad=pad),
        grid=(nblk,),
        in_specs=[pl.BlockSpec((m, 128), lambda i: (0, i)),
                  gspec(lanes, lanes), gspec(lanes, n2), gspec(lanes, n2), gspec(8, n2), gspec(8, n2)],
        out_specs=pl.BlockSpec((m, 128), lambda i: (0, i)),
        out_shape=jax.ShapeDtypeStruct((m, W_BRANCH), F32),
        scratch_shapes=[pltpu.VMEM((SSM_GB, rows, lanes), BF16), pltpu.VMEM((SSM_GB, rows, lanes), F32),
                        pltpu.VMEM((pad + rows, n2), F32)],
        compiler_params=_cparams(("parallel",)),
        name="ssm_scan",
    )(pm, toep, bpow, cpow, lama, lamb)


def _ssm_post_kernel(yraw_ref, u_ref, d_ref, wglu_ref, o_ref):
    w = W_BRANCH
    y = yraw_ref[...] + d_ref[...] * u_ref[...]
    zg = jnp.dot(_gelu(y).astype(BF16), wglu_ref[...], preferred_element_type=F32)
    o_ref[...] = (zg[:, :w] * _sigmoid(zg[:, w:])).astype(o_ref.dtype)


def _ssm_post(yraw, u, d, wglu, *, tm=512):
    m, w = yraw.shape
    row = pl.BlockSpec((tm, w), lambda i: (i, 0))
    return pl.pallas_call(
        _ssm_post_kernel,
        grid=(m // tm,),
        in_specs=[row, row, pl.BlockSpec((1, w), lambda i: (0, 0)), pl.BlockSpec(wglu.shape, lambda i: (0, 0))],
        out_specs=row,
        out_shape=jax.ShapeDtypeStruct((m, w), BF16),
        compiler_params=_cparams(("parallel",)),
        name="ssm_post",
    )(yraw, u, d, wglu)


def _merge_kernel(h_ref, y0, y1, y2, y3, g0, g1, g2, g3, b0, b1, b2, b3, o_ref):
    ys = (y0, y1, y2, y3)
    wg = (g0, g1, g2, g3)
    wb = (b0, b1, b2, b3)
    h = h_ref[...]
    acc = None
    for k in range(N_BRANCH):
        gate = jnp.dot(h, wg[k][...], preferred_element_type=F32)
        yb = jnp.dot(ys[k][...], wb[k][0], preferred_element_type=F32)
        term = yb * _sigmoid(gate)
        acc = term if acc is None else acc + term
    o_ref[...] = acc.astype(o_ref.dtype)


def _merge(h, ys, wgate, wbranch, layer, *, tm=1024, tn=512):
    m, d = h.shape
    w = W_BRANCH
    nj = d // tn
    j0 = OFF_GATE // tn
    gate_spec = lambda k: pl.BlockSpec((None, d, tn), lambda i, j: (layer, 0, j0 + k * nj + j))
    br_spec = lambda k: pl.BlockSpec((1, w, tn), lambda i, j: (layer * N_BRANCH + k, 0, j))
    return pl.pallas_call(
        _merge_kernel,
        grid=(m // tm, nj),
        in_specs=[pl.BlockSpec((tm, d), lambda i, j: (i, 0))]
        + [pl.BlockSpec((tm, w), lambda i, j: (i, 0))] * N_BRANCH
        + [gate_spec(k) for k in range(N_BRANCH)]
        + [br_spec(k) for k in range(N_BRANCH)],
        out_specs=pl.BlockSpec((tm, tn), lambda i, j: (i, j)),
        out_shape=jax.ShapeDtypeStruct((m, d), BF16),
        compiler_params=_cparams(("parallel", "arbitrary")),
        name="gated_merge",
    )(h, *ys, *([wgate] * N_BRANCH), *([wbranch] * N_BRANCH))


def _out_kernel(m_ref, w_ref, x_ref, g_ref, o_ref):
    mix = jnp.dot(m_ref[...], w_ref[...], preferred_element_type=F32)
    o_ref[...] = x_ref[...] + _rms(mix, g_ref[...])


def _out_proj(merged, w_o, x2, g, layer, *, tm=256):
    m, d = x2.shape
    return pl.pallas_call(
        _out_kernel,
        grid=(m // tm,),
        in_specs=[
            pl.BlockSpec((tm, d), lambda i: (i, 0)),
            pl.BlockSpec((None, d, d), lambda i: (layer, 0, 0), pipeline_mode=pl.Buffered(1)),
            pl.BlockSpec((tm, d), lambda i: (i, 0)),
            pl.BlockSpec((1, d), lambda i: (0, 0)),
        ],
        out_specs=pl.BlockSpec((tm, d), lambda i: (i, 0)),
        out_shape=jax.ShapeDtypeStruct((m, d), F32),
        compiler_params=_cparams(("parallel",)),
        name="out_proj",
    )(merged, w_o, x2, g)


def _ffn_kernel(x_ref, gpre_ref, w1_ref, w2_ref, gpost_ref, o_ref, h_ref, *, rows):
    j = pl.program_id(1)
    nrb = x_ref.shape[0] // rows

    @pl.when(j == 0)
    def _():
        def body(i, carry):
            r = pl.multiple_of(i * rows, rows)
            h_ref[pl.ds(r, rows), :] = _rms(x_ref[pl.ds(r, rows), :], gpre_ref[...]).astype(h_ref.dtype)
            o_ref[pl.ds(r, rows), :] = jnp.zeros((rows, o_ref.shape[1]), o_ref.dtype)
            return carry
        lax.fori_loop(0, nrb, body, 0)

    a = jnp.dot(h_ref[...], w1_ref[...], preferred_element_type=F32)
    a = jnp.maximum(a, 0.0)
    a = (a * a).astype(BF16)
    nw = 512
    for c in range(o_ref.shape[1] // nw):
        o_ref[:, c * nw:(c + 1) * nw] += jnp.dot(a, w2_ref[:, c * nw:(c + 1) * nw],
                                                 preferred_element_type=F32)

    @pl.when(j == pl.num_programs(1) - 1)
    def _():
        def body(i, carry):
            r = pl.multiple_of(i * rows, rows)
            o_ref[pl.ds(r, rows), :] = (x_ref[pl.ds(r, rows), :]
                                        + _rms(o_ref[pl.ds(r, rows), :], gpost_ref[...]))
            return carry
        lax.fori_loop(0, nrb, body, 0)


def _ffn(x2, gpre, w1, w2, gpost, layer, *, tm=1024, tf=512):
    m, d = x2.shape
    f = w1.shape[2]
    return pl.pallas_call(
        functools.partial(_ffn_kernel, rows=64),
        grid=(m // tm, f // tf),
        in_specs=[
            pl.BlockSpec((tm, d), lambda i, j: (i, 0)),
            pl.BlockSpec((1, d), lambda i, j: (0, 0)),
            pl.BlockSpec((None, d, tf), lambda i, j: (layer, 0, j)),
            pl.BlockSpec((None, tf, d), lambda i, j: (layer, j, 0)),
            pl.BlockSpec((1, d), lambda i, j: (0, 0)),
        ],
        out_specs=pl.BlockSpec((tm, d), lambda i, j: (i, 0)),
        out_shape=jax.ShapeDtypeStruct((m, d), F32),
        scratch_shapes=[pltpu.VMEM((tm, d), BF16)],
        compiler_params=_cparams(("parallel", "arbitrary")),
        name="ffn",
    )(x2, gpre, w1, w2, gpost)


def kernel(x, g_pre_mix, w_in, ssm_a_re, ssm_a_im, ssm_log_dt, ssm_b_re, ssm_b_im, ssm_c_re, ssm_c_im,
           ssm_d, ssm_w_glu, pool_w, pool_scale, conv_w, conv_b, conv_ln_g, conv_ln_b, gmlp_ln_g,
           gmlp_ln_b, gmlp_ws, gmlp_bs, w_branch, w_o, g_post_mix, g_pre_mlp, w_ff1, w_ff2, g_post_mlp):
    nb, seq, d = x.shape
    m = nb * seq
    depth = w_in.shape[0]
    g, p, t = SSM_GROUPS, SSM_GROUP, SSM_T
    nchunk = seq // t
    assert d == D_MODEL and nchunk == 1 << SSM_SCAN_STEPS

    toep, bpow, cpow, lama, lamb = _ssm_prep(ssm_a_re, ssm_a_im, ssm_log_dt, ssm_b_re, ssm_b_im,
                                             ssm_c_re, ssm_c_im)

    w_in_b = _cast_bf16(w_in, 256)
    w_br_b = _cast_bf16(w_branch.reshape(depth, N_BRANCH * W_BRANCH, d), 1024).reshape(
        depth * N_BRANCH, W_BRANCH, d)
    w_o_b = _cast_bf16(w_o, 1024)
    w_ff1_b = _cast_bf16(w_ff1, 256)
    w_ff2_b = _cast_bf16(w_ff2, 1024)

    row = lambda a: a.reshape(1, -1)
    x2 = x.reshape(m, d)
    for l in range(depth):
        h, u, y_pool, y_conv, y_gmlp = _proj_mix(
            x2, row(g_pre_mix[l]), w_in_b, l, pool_w[l].astype(BF16), row(pool_scale[l]),
            jnp.broadcast_to(conv_w[l][:, None, :], (CONV_WIDTH, 8, W_BRANCH)),
            row(conv_b[l]), row(conv_ln_g[l]), row(conv_ln_b[l]), row(gmlp_ln_g[l]), row(gmlp_ln_b[l]),
            gmlp_ws[l], jnp.transpose(gmlp_bs[l]), seq=seq)
        yraw = _ssm_scan(u, toep, bpow, cpow, lama, lamb, l, nchunk=nchunk)
        y_ssm = _ssm_post(yraw, u, row(ssm_d[l]), ssm_w_glu[l].astype(BF16))
        merged = _merge(h, (y_ssm, y_pool, y_conv, y_gmlp), w_in_b, w_br_b, l)
        x2 = _out_proj(merged, w_o_b, x2, row(g_post_mix[l]), l)
        x2 = _ffn(x2, row(g_pre_mlp[l]), w_ff1_b, w_ff2_b, row(g_post_mlp[l]), l)
    return x2.reshape(nb, seq, d)
```

```python
import functools

import jax
import jax.numpy as jnp
from jax import lax
from jax.experimental import pallas as pl
from jax.experimental.pallas import tpu as pltpu

D_MODEL = 2048
DEPTH = 2
N_BRANCH = 4
W_BRANCH = D_MODEL // 4
SSM_GROUP = 16
SSM_GROUPS = W_BRANCH // SSM_GROUP
SSM_STATE = 64
POOL_WINDOWS = (2, 4, 8, 16)
POOL_GW = W_BRANCH // len(POOL_WINDOWS)
CONV_WIDTH = 31
GMLP_CHUNK = 128
GMLP_HEADS = 4
GMLP_HD = W_BRANCH // GMLP_HEADS
D_FF = 4 * D_MODEL
EPS = 1e-6

OFF_SSM = 0
OFF_POOL = OFF_SSM + W_BRANCH
OFF_CONV = OFF_POOL + W_BRANCH
OFF_GMLP = OFF_CONV + 2 * W_BRANCH
OFF_GATE = OFF_GMLP + 2 * W_BRANCH

SSM_T = 16
SSM_LANES = SSM_T * SSM_GROUP
SSM_SCAN_STEPS = 7
CONV_HALO = 32
POOL_HALO = 16
CONV_RB = 64

VMEM_LIMIT = 56 * 1024 * 1024
BF16 = jnp.bfloat16
F32 = jnp.float32


def _cparams(sem):
    return pltpu.CompilerParams(dimension_semantics=sem, vmem_limit_bytes=VMEM_LIMIT)


def _sigmoid(x):
    return 0.5 * (1.0 + jnp.tanh(0.5 * x))


def _gelu(x):
    c = 0.7978845608028654
    return 0.5 * x * (1.0 + jnp.tanh(c * (x + 0.044715 * (x * x * x))))


def _rms(x, g):
    ms = jnp.mean(x * x, axis=-1, keepdims=True)
    return x * lax.rsqrt(ms + EPS) * g


def _layer_norm(x, g, b):
    mu = jnp.mean(x, axis=-1, keepdims=True)
    xc = x - mu
    var = jnp.mean(xc * xc, axis=-1, keepdims=True)
    return xc * lax.rsqrt(var + EPS) * g + b


def _cast_kernel(w_ref, o_ref):
    o_ref[...] = w_ref[...].astype(o_ref.dtype)


def _cast_bf16(w, layer, rows):
    _, r, c = w.shape
    return pl.pallas_call(
        _cast_kernel,
        grid=(r // rows,),
        in_specs=[pl.BlockSpec((1, rows, c), lambda i: (layer, i, 0))],
        out_specs=pl.BlockSpec((1, rows, c), lambda i: (0, i, 0)),
        out_shape=jax.ShapeDtypeStruct((1, r, c), BF16),
        compiler_params=_cparams(("parallel",)),
        name="cast_bf16",
    )(w)


def _side_cast_specs(weights, layer, nsteps, step_of):
    in_specs, out_specs, out_shapes = [], [], []
    for w in weights:
        _, r, c = w.shape
        assert r % nsteps == 0 and (r // nsteps) % 16 == 0
        rows = r // nsteps
        in_specs.append(pl.BlockSpec((None, rows, c), lambda *ids: (layer, step_of(*ids), 0)))
        out_specs.append(pl.BlockSpec((None, rows, c), lambda *ids: (0, step_of(*ids), 0)))
        out_shapes.append(jax.ShapeDtypeStruct((1, r, c), BF16))
    return in_specs, out_specs, out_shapes


def _side_cast(in_refs, out_refs):
    for wi, wo in zip(in_refs, out_refs):
        wo[...] = wi[...].astype(wo.dtype)


def _proj_mix_kernel(x_ref, g_ref, w_ref, poolw_ref, pscale_ref, convw_ref, convb_ref, clg_ref, clb_ref,
                     glg_ref, glb_ref, ws_ref, bst_ref,
                     h_ref, u_ref, ypool_ref, yconv_ref, ygmlp_ref, vbuf, pbuf, sbuf, *, rows, tiles_per_seq):
    c = pl.program_id(0) % tiles_per_seq
    tc = x_ref.shape[0]
    w = W_BRANCH

    @pl.when(c == 0)
    def _():
        vbuf[0:CONV_HALO, :] = jnp.zeros((CONV_HALO, w), F32)
        pbuf[0:POOL_HALO, :] = jnp.zeros((POOL_HALO, w), F32)

    def body(i, carry):
        r = pl.multiple_of(i * rows, rows)
        h_ref[pl.ds(r, rows), :] = _rms(x_ref[pl.ds(r, rows), :], g_ref[...]).astype(h_ref.dtype)
        return carry
    lax.fori_loop(0, tc // rows, body, 0)

    h = h_ref[...]
    proj = lambda off, width: jnp.dot(h, w_ref[:, off:off + width], preferred_element_type=F32)
    u_ref[...] = proj(OFF_SSM, w)

    pbuf[POOL_HALO:POOL_HALO + tc, :] = proj(OFF_POOL, w)
    t = c * tc + lax.broadcasted_iota(jnp.int32, (tc, 1), 0)
    for gi, win in enumerate(POOL_WINDOWS):
        cols = slice(gi * POOL_GW, (gi + 1) * POOL_GW)
        tok = pbuf[POOL_HALO:POOL_HALO + tc, cols]
        s = tok
        for j in range(1, win):
            s = s + pbuf[POOL_HALO - j:POOL_HALO - j + tc, cols]
        count = jnp.minimum(t + 1, win).astype(F32)
        pooled = s / count - tok
        mixed = jnp.dot(pooled.astype(BF16), poolw_ref[gi], preferred_element_type=F32)
        ypool_ref[:, cols] = (mixed * pscale_ref[:, cols]).astype(ypool_ref.dtype)
    pbuf[0:POOL_HALO, :] = pbuf[tc:tc + POOL_HALO, :]

    cv = proj(OFF_CONV, 2 * w)
    vbuf[CONV_HALO:CONV_HALO + tc, :] = cv[:, :w] * _sigmoid(cv[:, w:])
    span = CONV_HALO - 8
    for b in range(1, 8):
        sbuf[b - 1] = vbuf[8 - b:8 - b + span + tc, :]
    for rb in range(tc // CONV_RB):
        r0 = rb * CONV_RB
        acc = jnp.broadcast_to(convb_ref[...], (CONV_RB, w))
        for b in range(8):
            for a in range((CONV_WIDTH - 1 - b) // 8 + 1):
                k = CONV_WIDTH - 1 - (8 * a + b)
                lo = span - 8 * a + r0
                tap = sbuf[b - 1, lo:lo + CONV_RB, :] if b else vbuf[8 + lo:8 + lo + CONV_RB, :]
                wk = convw_ref[k]
                acc = acc + (tap.reshape(CONV_RB // 8, 8, w) * wk[None]).reshape(CONV_RB, w)
        yn = _layer_norm(acc, clg_ref[...], clb_ref[...])
        yconv_ref[r0:r0 + CONV_RB, :] = (yn * _sigmoid(yn)).astype(yconv_ref.dtype)
    vbuf[0:CONV_HALO, :] = vbuf[tc:tc + CONV_HALO, :]

    gm = proj(OFF_GMLP, 2 * w)
    ug = _gelu(gm[:, :w])
    vg = _layer_norm(_gelu(gm[:, w:]), glg_ref[...], glb_ref[...]).astype(BF16)
    ri = lax.broadcasted_iota(jnp.int32, (GMLP_CHUNK, GMLP_CHUNK), 0)
    ci = lax.broadcasted_iota(jnp.int32, (GMLP_CHUNK, GMLP_CHUNK), 1)
    causal = ri >= ci
    for hd in range(GMLP_HEADS):
        wsm = jnp.where(causal, ws_ref[hd], 0.0).astype(BF16)
        cols = slice(hd * GMLP_HD, (hd + 1) * GMLP_HD)
        for ch in range(tc // GMLP_CHUNK):
            rws = slice(ch * GMLP_CHUNK, (ch + 1) * GMLP_CHUNK)
            sv = jnp.dot(wsm, vg[rws, cols], preferred_element_type=F32) + bst_ref[:, hd:hd + 1]
            ygmlp_ref[rws, cols] = (ug[rws, cols] * sv).astype(ygmlp_ref.dtype)


def _proj_mix(x2, g, w_in, layer, poolw, pscale, convw, convb, clg, clb, glg, glb, ws, bst, *, seq, tm=512):
    m, d = x2.shape
    n = OFF_GATE
    w = W_BRANCH
    assert seq % tm == 0 and tm % GMLP_CHUNK == 0 and tm % CONV_RB == 0
    row = lambda width: pl.BlockSpec((tm, width), lambda i: (i, 0))
    full = lambda a: pl.BlockSpec(a.shape, lambda i: (0,) * a.ndim)
    args = (poolw, pscale, convw, convb, clg, clb, glg, glb, ws, bst)
    return pl.pallas_call(
        functools.partial(_proj_mix_kernel, rows=64, tiles_per_seq=seq // tm),
        grid=(m // tm,),
        in_specs=[row(d), full(g),
                  pl.BlockSpec((None, d, n), lambda i: (layer, 0, 0), pipeline_mode=pl.Buffered(1))]
        + [full(a) for a in args],
        out_specs=[row(d), row(w), row(w), row(w), row(w)],
        out_shape=[jax.ShapeDtypeStruct((m, d), BF16), jax.ShapeDtypeStruct((m, w), F32)]
        + [jax.ShapeDtypeStruct((m, w), BF16)] * 3,
        scratch_shapes=[pltpu.VMEM((CONV_HALO + tm, w), F32), pltpu.VMEM((POOL_HALO + tm, w), F32),
                        pltpu.VMEM((7, CONV_HALO - 8 + tm, w), F32)],
        compiler_params=_cparams(("arbitrary",)),
        name="proj_mix",
    )(x2, g, w_in, *args)


PREP_GB = 8


def _ssm_prep_kernel(ar_ref, ai_ref, ldt_ref, cre_ref, cim_ref, bre_ref, bim_ref,
                     toep_ref, bpow_ref, cpow_ref, lama_ref, lamb_ref, cl_s, bl_s):
    n = SSM_STATE
    p = SSM_GROUP
    lanes = SSM_LANES
    lane = lax.broadcasted_iota(jnp.int32, (1, lanes), 1)
    for e in range(PREP_GB):
        ar = ar_ref[e]
        ai = ai_ref[e]
        dt = jnp.exp(ldt_ref[e])
        mag = jnp.exp(ar * dt)
        ang = ai * dt
        lr = mag * jnp.cos(ang)
        li = mag * jnp.sin(ang)
        den = ar * ar + ai * ai
        fr = ((lr - 1.0) * ar + li * ai) / den
        fi = (li * ar - (lr - 1.0) * ai) / den
        cre = cre_ref[e]
        cim = cim_ref[e]
        br = bre_ref[e]
        bi = bim_ref[e]
        bbr = fr * br - fi * bi
        bbi = fr * bi + fi * br
        pr = jnp.ones_like(lr)
        pi = jnp.zeros_like(lr)
        for k in range(SSM_T + 1):
            cl_s[k * p:(k + 1) * p, 0:n] = cre * pr - cim * pi
            cl_s[k * p:(k + 1) * p, n:2 * n] = -(cre * pi + cim * pr)
            if k < SSM_T:
                s = SSM_T - 1 - k
                bl_s[s * p:(s + 1) * p, 0:n] = pr * bbr - pi * bbi
                bl_s[s * p:(s + 1) * p, n:2 * n] = pr * bbi + pi * bbr
                pr, pi = pr * lr - pi * li, pr * li + pi * lr
        qr, qi = pr, pi
        for j in range(8):
            if j < SSM_SCAN_STEPS:
                lama_ref[e, j:j + 1, 0:n] = qr
                lama_ref[e, j:j + 1, n:2 * n] = qr
                lamb_ref[e, j:j + 1, 0:n] = -qi
                lamb_ref[e, j:j + 1, n:2 * n] = qi
                qr, qi = qr * qr - qi * qi, 2.0 * (qr * qi)
            else:
                lama_ref[e, j:j + 1, :] = jnp.zeros((1, 2 * n), F32)
                lamb_ref[e, j:j + 1, :] = jnp.zeros((1, 2 * n), F32)
        cpow_ref[e] = cl_s[p:(SSM_T + 1) * p, :].astype(cpow_ref.dtype)
        bpow_ref[e] = bl_s[...].astype(bpow_ref.dtype)
        bbcat = bl_s[(SSM_T - 1) * p:SSM_T * p, :]
        kt = lax.dot_general(bbcat, cl_s[0:SSM_T * p, :], (((1,), (1,)), ((), ())),
                             preferred_element_type=F32, precision=lax.Precision.HIGHEST)
        for s in range(SSM_T):
            blk = pltpu.roll(kt, s * p, axis=1) if s else kt
            toep_ref[e, s * p:(s + 1) * p, :] = jnp.where(lane >= s * p, blk, 0.0).astype(toep_ref.dtype)


def _ssm_prep(a_re, a_im, log_dt, b_re, b_im, c_re, c_im):
    n, p = SSM_STATE, SSM_GROUP
    gg = a_re.shape[0] * a_re.shape[1]
    gb = PREP_GB
    lanes = SSM_LANES
    row = lambda a: a.reshape(gg, 1, n)
    ldt = jnp.broadcast_to(log_dt.reshape(gg, 1), (gg, n))
    bt = lambda b: jnp.transpose(b.reshape(gg, n, p), (0, 2, 1))
    rspec = pl.BlockSpec((gb, 1, n), lambda g: (g, 0, 0))
    pspec = pl.BlockSpec((gb, p, n), lambda g: (g, 0, 0))
    ospec = lambda a, b: pl.BlockSpec((gb, a, b), lambda g: (g, 0, 0))
    return pl.pallas_call(
        _ssm_prep_kernel,
        grid=(gg // gb,),
        in_specs=[rspec, rspec, rspec, pspec, pspec, pspec, pspec],
        out_specs=[ospec(lanes, lanes), ospec(lanes, 2 * n), ospec(lanes, 2 * n), ospec(8, 2 * n), ospec(8, 2 * n)],
        out_shape=[
            jax.ShapeDtypeStruct((gg, lanes, lanes), BF16),
            jax.ShapeDtypeStruct((gg, lanes, 2 * n), BF16),
            jax.ShapeDtypeStruct((gg, lanes, 2 * n), BF16),
            jax.ShapeDtypeStruct((gg, 8, 2 * n), F32),
            jax.ShapeDtypeStruct((gg, 8, 2 * n), F32),
        ],
        scratch_shapes=[pltpu.VMEM(((SSM_T + 1) * p, 2 * n), F32), pltpu.VMEM((SSM_T * p, 2 * n), F32)],
        compiler_params=_cparams(("arbitrary",)),
        name="ssm_prep",
    )(row(a_re), row(a_im), row(ldt), c_re.reshape(gg, p, n), c_im.reshape(gg, p, n), bt(b_re), bt(b_im))


SSM_GB = 128 // SSM_GROUP
SSM_RB = 16


def _piece_transpose(xs, masks):
    n = len(xs)
    w = 128 // n
    rolled = []
    for k in range(n):
        wk = xs[k % n]
        for a in range(1, n):
            wk = jnp.where(masks[a], xs[(a + k) % n], wk)
        rolled.append(pltpu.roll(wk, w * k, axis=1) if k else wk)
    ys = []
    for a in range(n):
        ya = rolled[(0 - a) % n]
        for b in range(1, n):
            ya = jnp.where(masks[b], rolled[(b - a) % n], ya)
        ys.append(ya)
    return ys


def _ssm_kernel(u_ref, toep_ref, bpow_ref, cpow_ref, lama_ref, lamb_ref, y_ref, up_ref, yp_ref, xs_ref,
                *, nchunk, pad):
    rows = up_ref.shape[1]
    n2 = 2 * SSM_STATE
    t = SSM_T
    lane = lax.broadcasted_iota(jnp.int32, (1, 128), 1)
    masks = [lane // SSM_GROUP == a for a in range(SSM_GB)]
    nhalf = t // SSM_GB

    def to_chunks(rb, carry):
        r = pl.multiple_of(rb * SSM_RB, SSM_RB)
        for hf in range(nhalf):
            xs = [u_ref[pl.ds(r * t + hf * SSM_GB + b, SSM_RB, stride=t), :] for b in range(SSM_GB)]
            ys = _piece_transpose(xs, masks)
            for a in range(SSM_GB):
                up_ref[a, pl.ds(r, SSM_RB), hf * 128:(hf + 1) * 128] = ys[a].astype(up_ref.dtype)
        return carry
    lax.fori_loop(0, rows // SSM_RB, to_chunks, 0, unroll=2)

    cidx = lax.broadcasted_iota(jnp.int32, (rows, 1), 0) % nchunk
    xs_ref[0:pad, :] = jnp.zeros((pad, n2), F32)
    for a in range(SSM_GB):
        u = up_ref[a]
        xs_ref[pad:pad + rows, :] = jnp.dot(u, bpow_ref[a], preferred_element_type=F32)
        for j in range(SSM_SCAN_STEPS):
            sh = 1 << j
            cur = xs_ref[pad:pad + rows, :]
            prev = jnp.where(cidx >= sh, xs_ref[pad - sh:pad - sh + rows, :], 0.0)
            la = lama_ref[a, j:j + 1, :]
            lb = lamb_ref[a, j:j + 1, :]
            xs_ref[pad:pad + rows, :] = cur + prev * la + pltpu.roll(prev, SSM_STATE, axis=1) * lb
        xprev = jnp.where(cidx >= 1, xs_ref[pad - 1:pad - 1 + rows, :], 0.0).astype(BF16)
        yp_ref[a] = (jnp.dot(u, toep_ref[a], preferred_element_type=F32)
                     + lax.dot_general(xprev, cpow_ref[a], (((1,), (1,)), ((), ())),
                                       preferred_element_type=F32))

    def to_tokens(rb, carry):
        r = pl.multiple_of(rb * 8, 8)
        for hf in range(nhalf):
            ys = [yp_ref[a, pl.ds(r, 8), hf * 128:(hf + 1) * 128] for a in range(SSM_GB)]
            xs = _piece_transpose(ys, masks)
            for b in range(SSM_GB):
                y_ref[pl.ds(r * t + hf * SSM_GB + b, 8, stride=t), :] = xs[b]
        return carry
    lax.fori_loop(0, rows // 8, to_tokens, 0, unroll=2)


def _ssm_scan(pm, toep, bpow, cpow, lama, lamb, layer, *, nchunk):
    m = pm.shape[0]
    rows = m // SSM_T
    lanes = SSM_LANES
    n2 = 2 * SSM_STATE
    pad = 1 << (SSM_SCAN_STEPS - 1)
    nblk = W_BRANCH // 128
    gspec = lambda a, b: pl.BlockSpec((SSM_GB, a, b), lambda i: (layer * nblk + i, 0, 0))
    return pl.pallas_call(
        functools.partial(_ssm_kernel, nchunk=nchunk, pad=pad),
        grid=(nblk,),
        in_specs=[pl.BlockSpec((m, 128), lambda i: (0, i)),
                  gspec(lanes, lanes), gspec(lanes, n2), gspec(lanes, n2), gspec(8, n2), gspec(8, n2)],
        out_specs=pl.BlockSpec((m, 128), lambda i: (0, i)),
        out_shape=jax.ShapeDtypeStruct((m, W_BRANCH), F32),
        scratch_shapes=[pltpu.VMEM((SSM_GB, rows, lanes), BF16), pltpu.VMEM((SSM_GB, rows, lanes), F32),
                        pltpu.VMEM((pad + rows, n2), F32)],
        compiler_params=_cparams(("parallel",)),
        name="ssm_scan",
    )(pm, toep, bpow, cpow, lama, lamb)


def _ssm_post_kernel(yraw_ref, u_ref, d_ref, wglu_ref, o_ref):
    w = W_BRANCH
    y = yraw_ref[...] + d_ref[...] * u_ref[...]
    zg = jnp.dot(_gelu(y).astype(BF16), wglu_ref[...], preferred_element_type=F32)
    o_ref[...] = (zg[:, :w] * _sigmoid(zg[:, w:])).astype(o_ref.dtype)


def _ssm_post(yraw, u, d, wglu, *, tm=512):
    m, w = yraw.shape
    row = pl.BlockSpec((tm, w), lambda i: (i, 0))
    return pl.pallas_call(
        _ssm_post_kernel,
        grid=(m // tm,),
        in_specs=[row, row, pl.BlockSpec((1, w), lambda i: (0, 0)), pl.BlockSpec(wglu.shape, lambda i: (0, 0))],
        out_specs=row,
        out_shape=jax.ShapeDtypeStruct((m, w), BF16),
        compiler_params=_cparams(("parallel",)),
        name="ssm_post",
    )(yraw, u, d, wglu)


def _merge_kernel(*refs, n_side):
    nb = N_BRANCH
    h_ref = refs[0]
    ys = refs[1:1 + nb]
    wg = refs[1 + nb:1 + 2 * nb]
    wb = refs[1 + 2 * nb:1 + 3 * nb]
    side_in = refs[1 + 3 * nb:1 + 3 * nb + n_side]
    o_ref = refs[1 + 3 * nb + n_side]
    side_out = refs[2 + 3 * nb + n_side:2 + 3 * nb + 2 * n_side]
    h = h_ref[...]
    acc = None
    for k in range(nb):
        gate = jnp.dot(h, wg[k][...], preferred_element_type=F32)
        yb = jnp.dot(ys[k][...], wb[k][0], preferred_element_type=F32)
        term = yb * _sigmoid(gate)
        acc = term if acc is None else acc + term
    o_ref[...] = acc.astype(o_ref.dtype)
    _side_cast(side_in, side_out)


def _merge(h, ys, wgate, wbranch, side_weights, side_layer, *, tm=1024, tn=256):
    m, d = h.shape
    w = W_BRANCH
    nj = d // tn
    j0 = OFF_GATE // tn
    gate_spec = lambda k: pl.BlockSpec((None, d, tn), lambda i, j: (0, 0, j0 + k * nj + j))
    br_spec = lambda k: pl.BlockSpec((1, w, tn), lambda i, j: (k, 0, j))
    s_in, s_out, s_shape = _side_cast_specs(side_weights, side_layer, (m // tm) * nj, lambda i, j: i * nj + j)
    outs = pl.pallas_call(
        functools.partial(_merge_kernel, n_side=len(side_weights)),
        grid=(m // tm, nj),
        in_specs=[pl.BlockSpec((tm, d), lambda i, j: (i, 0))]
        + [pl.BlockSpec((tm, w), lambda i, j: (i, 0))] * N_BRANCH
        + [gate_spec(k) for k in range(N_BRANCH)]
        + [br_spec(k) for k in range(N_BRANCH)]
        + s_in,
        out_specs=[pl.BlockSpec((tm, tn), lambda i, j: (i, j))] + s_out,
        out_shape=[jax.ShapeDtypeStruct((m, d), BF16)] + s_shape,
        compiler_params=_cparams(("arbitrary", "arbitrary")),
        name="gated_merge",
    )(h, *ys, *([wgate] * N_BRANCH), *([wbranch] * N_BRANCH), *side_weights)
    return outs[0], outs[1:]


def _out_kernel(m_ref, w_ref, x_ref, g_ref, o_ref):
    mix = jnp.dot(m_ref[...], w_ref[...], preferred_element_type=F32)
    o_ref[...] = x_ref[...] + _rms(mix, g_ref[...])


def _out_proj(merged, w_o, x2, g, layer, *, tm=256):
    m, d = x2.shape
    return pl.pallas_call(
        _out_kernel,
        grid=(m // tm,),
        in_specs=[
            pl.BlockSpec((tm, d), lambda i: (i, 0)),
            pl.BlockSpec((None, d, d), lambda i: (layer, 0, 0), pipeline_mode=pl.Buffered(1)),
            pl.BlockSpec((tm, d), lambda i: (i, 0)),
            pl.BlockSpec((1, d), lambda i: (0, 0)),
        ],
        out_specs=pl.BlockSpec((tm, d), lambda i: (i, 0)),
        out_shape=jax.ShapeDtypeStruct((m, d), F32),
        compiler_params=_cparams(("parallel",)),
        name="out_proj",
    )(merged, w_o, x2, g)


def _ffn_kernel(*refs, rows, n_side):
    x_ref, gpre_ref, w1_ref, w2_ref, gpost_ref = refs[:5]
    side_in = refs[5:5 + n_side]
    o_ref = refs[5 + n_side]
    side_out = refs[6 + n_side:6 + 2 * n_side]
    h_ref = refs[6 + 2 * n_side]
    j = pl.program_id(1)
    nrb = x_ref.shape[0] // rows

    @pl.when(j == 0)
    def _():
        def body(i, carry):
            r = pl.multiple_of(i * rows, rows)
            h_ref[pl.ds(r, rows), :] = _rms(x_ref[pl.ds(r, rows), :], gpre_ref[...]).astype(h_ref.dtype)
            o_ref[pl.ds(r, rows), :] = jnp.zeros((rows, o_ref.shape[1]), o_ref.dtype)
            return carry
        lax.fori_loop(0, nrb, body, 0)

    a = jnp.dot(h_ref[...], w1_ref[...], preferred_element_type=F32)
    a = jnp.maximum(a, 0.0)
    a = (a * a).astype(BF16)
    nw = 512
    for c in range(o_ref.shape[1] // nw):
        o_ref[:, c * nw:(c + 1) * nw] += jnp.dot(a, w2_ref[:, c * nw:(c + 1) * nw],
                                                 preferred_element_type=F32)
    _side_cast(side_in, side_out)

    @pl.when(j == pl.num_programs(1) - 1)
    def _():
        def body(i, carry):
            r = pl.multiple_of(i * rows, rows)
            o_ref[pl.ds(r, rows), :] = (x_ref[pl.ds(r, rows), :]
                                        + _rms(o_ref[pl.ds(r, rows), :], gpost_ref[...]))
            return carry
        lax.fori_loop(0, nrb, body, 0)


def _ffn(x2, gpre, w1, w2, gpost, side_weights, side_layer, *, tm=1024, tf=512):
    m, d = x2.shape
    f = w1.shape[2]
    nj = f // tf
    s_in, s_out, s_shape = _side_cast_specs(side_weights, side_layer, (m // tm) * nj, lambda i, j: i * nj + j)
    outs = pl.pallas_call(
        functools.partial(_ffn_kernel, rows=64, n_side=len(side_weights)),
        grid=(m // tm, nj),
        in_specs=[
            pl.BlockSpec((tm, d), lambda i, j: (i, 0)),
            pl.BlockSpec((1, d), lambda i, j: (0, 0)),
            pl.BlockSpec((None, d, tf), lambda i, j: (0, 0, j)),
            pl.BlockSpec((None, tf, d), lambda i, j: (0, j, 0)),
            pl.BlockSpec((1, d), lambda i, j: (0, 0)),
        ] + s_in,
        out_specs=[pl.BlockSpec((tm, d), lambda i, j: (i, 0))] + s_out,
        out_shape=[jax.ShapeDtypeStruct((m, d), F32)] + s_shape,
        scratch_shapes=[pltpu.VMEM((tm, d), BF16)],
        compiler_params=_cparams(("arbitrary", "arbitrary")),
        name="ffn",
    )(x2, gpre, w1, w2, gpost, *side_weights)
    return outs[0], outs[1:]


def kernel(x, g_pre_mix, w_in, ssm_a_re, ssm_a_im, ssm_log_dt, ssm_b_re, ssm_b_im, ssm_c_re, ssm_c_im,
           ssm_d, ssm_w_glu, pool_w, pool_scale, conv_w, conv_b, conv_ln_g, conv_ln_b, gmlp_ln_g,
           gmlp_ln_b, gmlp_ws, gmlp_bs, w_branch, w_o, g_post_mix, g_pre_mlp, w_ff1, w_ff2, g_post_mlp):
    nb, seq, d = x.shape
    m = nb * seq
    depth = w_in.shape[0]
    g, p, t = SSM_GROUPS, SSM_GROUP, SSM_T
    nchunk = seq // t
    assert d == D_MODEL and nchunk == 1 << SSM_SCAN_STEPS

    toep, bpow, cpow, lama, lamb = _ssm_prep(ssm_a_re, ssm_a_im, ssm_log_dt, ssm_b_re, ssm_b_im,
                                             ssm_c_re, ssm_c_im)

    w_br2 = w_branch.reshape(depth, N_BRANCH * W_BRANCH, d)
    w_in_b = _cast_bf16(w_in, 0, 256)
    w_br_b = _cast_bf16(w_br2, 0, 1024)
    w_o_b = _cast_bf16(w_o, 0, 1024)

    row = lambda a: a.reshape(1, -1)
    x2 = x.reshape(m, d)
    for l in range(depth):
        h, u, y_pool, y_conv, y_gmlp = _proj_mix(
            x2, row(g_pre_mix[l]), w_in_b, 0, pool_w[l].astype(BF16), row(pool_scale[l]),
            jnp.broadcast_to(conv_w[l][:, None, :], (CONV_WIDTH, 8, W_BRANCH)),
            row(conv_b[l]), row(conv_ln_g[l]), row(conv_ln_b[l]), row(gmlp_ln_g[l]), row(gmlp_ln_b[l]),
            gmlp_ws[l], jnp.transpose(gmlp_bs[l]), seq=seq)
        yraw = _ssm_scan(u, toep, bpow, cpow, lama, lamb, l, nchunk=nchunk)
        y_ssm = _ssm_post(yraw, u, row(ssm_d[l]), ssm_w_glu[l].astype(BF16))
        merged, (w_ff1_b, w_ff2_b) = _merge(h, (y_ssm, y_pool, y_conv, y_gmlp), w_in_b,
                                            w_br_b.reshape(N_BRANCH, W_BRANCH, d), (w_ff1, w_ff2), l)
        x2 = _out_proj(merged, w_o_b, x2, row(g_post_mix[l]), 0)
        nxt = (w_in, w_br2, w_o) if l + 1 < depth else ()
        x2, nxt_b = _ffn(x2, row(g_pre_mlp[l]), w_ff1_b, w_ff2_b, row(g_post_mlp[l]), nxt, l + 1)
        if nxt:
            w_in_b, w_br_b, w_o_b = nxt_b
    return x2.reshape(nb, seq, d)
```

```python
import functools

import jax
import jax.numpy as jnp
from jax import lax
from jax.experimental import pallas as pl
from jax.experimental.pallas import tpu as pltpu

D_MODEL = 2048
DEPTH = 2
N_BRANCH = 4
W_BRANCH = D_MODEL // 4
SSM_GROUP = 16
SSM_GROUPS = W_BRANCH // SSM_GROUP
SSM_STATE = 64
POOL_WINDOWS = (2, 4, 8, 16)
POOL_GW = W_BRANCH // len(POOL_WINDOWS)
CONV_WIDTH = 31
GMLP_CHUNK = 128
GMLP_HEADS = 4
GMLP_HD = W_BRANCH // GMLP_HEADS
D_FF = 4 * D_MODEL
EPS = 1e-6

OFF_SSM = 0
OFF_POOL = OFF_SSM + W_BRANCH
OFF_CONV = OFF_POOL + W_BRANCH
OFF_GMLP = OFF_CONV + 2 * W_BRANCH
OFF_GATE = OFF_GMLP + 2 * W_BRANCH

SSM_T = 16
SSM_LANES = SSM_T * SSM_GROUP
SSM_SCAN_STEPS = 7
CONV_HALO = 32
POOL_HALO = 16
CONV_RB = 64

VMEM_LIMIT = 56 * 1024 * 1024
BF16 = jnp.bfloat16
F32 = jnp.float32


def _cparams(sem):
    return pltpu.CompilerParams(dimension_semantics=sem, vmem_limit_bytes=VMEM_LIMIT)


def _sigmoid(x):
    return 0.5 * (1.0 + jnp.tanh(0.5 * x))


def _gelu(x):
    c = 0.7978845608028654
    return 0.5 * x * (1.0 + jnp.tanh(c * (x + 0.044715 * (x * x * x))))


def _rms(x, g):
    ms = jnp.mean(x * x, axis=-1, keepdims=True)
    return x * lax.rsqrt(ms + EPS) * g


def _layer_norm(x, g, b):
    mu = jnp.mean(x, axis=-1, keepdims=True)
    xc = x - mu
    var = jnp.mean(xc * xc, axis=-1, keepdims=True)
    return xc * lax.rsqrt(var + EPS) * g + b


def _cast_kernel(w_ref, o_ref):
    o_ref[...] = w_ref[...].astype(o_ref.dtype)


def _cast_bf16(w, layer, rows):
    _, r, c = w.shape
    return pl.pallas_call(
        _cast_kernel,
        grid=(r // rows,),
        in_specs=[pl.BlockSpec((1, rows, c), lambda i: (layer, i, 0))],
        out_specs=pl.BlockSpec((1, rows, c), lambda i: (0, i, 0)),
        out_shape=jax.ShapeDtypeStruct((1, r, c), BF16),
        compiler_params=_cparams(("parallel",)),
        name="cast_bf16",
    )(w)


def _side_cast_specs(weights, layer, nsteps, step_of):
    in_specs, out_specs, out_shapes = [], [], []
    for w in weights:
        _, r, c = w.shape
        assert r % nsteps == 0 and (r // nsteps) % 16 == 0
        rows = r // nsteps
        in_specs.append(pl.BlockSpec((None, rows, c), lambda *ids: (layer, step_of(*ids), 0)))
        out_specs.append(pl.BlockSpec((None, rows, c), lambda *ids: (0, step_of(*ids), 0)))
        out_shapes.append(jax.ShapeDtypeStruct((1, r, c), BF16))
    return in_specs, out_specs, out_shapes


def _side_cast(in_refs, out_refs):
    for wi, wo in zip(in_refs, out_refs):
        wo[...] = wi[...].astype(wo.dtype)


def _proj_mix_kernel(x_ref, g_ref, w_ref, poolw_ref, pscale_ref, convw_ref, convb_ref, clg_ref, clb_ref,
                     glg_ref, glb_ref, ws_ref, bst_ref,
                     h_ref, u_ref, ypool_ref, yconv_ref, ygmlp_ref, vbuf, pbuf, sbuf, *, rows, tiles_per_seq):
    c = pl.program_id(0) % tiles_per_seq
    tc = x_ref.shape[0]
    w = W_BRANCH

    @pl.when(c == 0)
    def _():
        vbuf[0:CONV_HALO, :] = jnp.zeros((CONV_HALO, w), F32)
        pbuf[0:POOL_HALO, :] = jnp.zeros((POOL_HALO, w), F32)

    def body(i, carry):
        r = pl.multiple_of(i * rows, rows)
        h_ref[pl.ds(r, rows), :] = _rms(x_ref[pl.ds(r, rows), :], g_ref[...]).astype(h_ref.dtype)
        return carry
    lax.fori_loop(0, tc // rows, body, 0, unroll=2)

    h = h_ref[...]
    proj = lambda off, width: jnp.dot(h, w_ref[:, off:off + width], preferred_element_type=F32)
    u_ref[...] = proj(OFF_SSM, w)

    pbuf[POOL_HALO:POOL_HALO + tc, :] = proj(OFF_POOL, w)
    t = c * tc + lax.broadcasted_iota(jnp.int32, (tc, 1), 0)
    for gi, win in enumerate(POOL_WINDOWS):
        cols = slice(gi * POOL_GW, (gi + 1) * POOL_GW)
        tok = pbuf[POOL_HALO:POOL_HALO + tc, cols]
        s = tok
        for j in range(1, win):
            s = s + pbuf[POOL_HALO - j:POOL_HALO - j + tc, cols]
        count = jnp.minimum(t + 1, win).astype(F32)
        pooled = s / count - tok
        mixed = jnp.dot(pooled.astype(BF16), poolw_ref[gi], preferred_element_type=F32)
        ypool_ref[:, cols] = (mixed * pscale_ref[:, cols]).astype(ypool_ref.dtype)
    pbuf[0:POOL_HALO, :] = pbuf[tc:tc + POOL_HALO, :]

    cv = proj(OFF_CONV, 2 * w)
    vbuf[CONV_HALO:CONV_HALO + tc, :] = cv[:, :w] * _sigmoid(cv[:, w:])
    span = CONV_HALO - 8
    for b in range(1, 8):
        sbuf[b - 1] = vbuf[8 - b:8 - b + span + tc, :]
    for rb in range(tc // CONV_RB):
        r0 = rb * CONV_RB
        acc = jnp.broadcast_to(convb_ref[...], (CONV_RB, w))
        for b in range(8):
            for a in range((CONV_WIDTH - 1 - b) // 8 + 1):
                k = CONV_WIDTH - 1 - (8 * a + b)
                lo = span - 8 * a + r0
                tap = sbuf[b - 1, lo:lo + CONV_RB, :] if b else vbuf[8 + lo:8 + lo + CONV_RB, :]
                wk = convw_ref[k]
                acc = acc + (tap.reshape(CONV_RB // 8, 8, w) * wk[None]).reshape(CONV_RB, w)
        yn = _layer_norm(acc, clg_ref[...], clb_ref[...])
        yconv_ref[r0:r0 + CONV_RB, :] = (yn * _sigmoid(yn)).astype(yconv_ref.dtype)
    vbuf[0:CONV_HALO, :] = vbuf[tc:tc + CONV_HALO, :]

    gm = proj(OFF_GMLP, 2 * w)
    ug = _gelu(gm[:, :w])
    vg = _layer_norm(_gelu(gm[:, w:]), glg_ref[...], glb_ref[...]).astype(BF16)
    ri = lax.broadcasted_iota(jnp.int32, (GMLP_CHUNK, GMLP_CHUNK), 0)
    ci = lax.broadcasted_iota(jnp.int32, (GMLP_CHUNK, GMLP_CHUNK), 1)
    causal = ri >= ci
    for hd in range(GMLP_HEADS):
        wsm = jnp.where(causal, ws_ref[hd], 0.0).astype(BF16)
        cols = slice(hd * GMLP_HD, (hd + 1) * GMLP_HD)
        for ch in range(tc // GMLP_CHUNK):
            rws = slice(ch * GMLP_CHUNK, (ch + 1) * GMLP_CHUNK)
            sv = jnp.dot(wsm, vg[rws, cols], preferred_element_type=F32) + bst_ref[:, hd:hd + 1]
            ygmlp_ref[rws, cols] = (ug[rws, cols] * sv).astype(ygmlp_ref.dtype)


def _proj_mix(x2, g, w_in, layer, poolw, pscale, convw, convb, clg, clb, glg, glb, ws, bst, *, seq, tm=512):
    m, d = x2.shape
    n = OFF_GATE
    w = W_BRANCH
    assert seq % tm == 0 and tm % GMLP_CHUNK == 0 and tm % CONV_RB == 0
    row = lambda width: pl.BlockSpec((tm, width), lambda i: (i, 0))
    full = lambda a: pl.BlockSpec(a.shape, lambda i: (0,) * a.ndim)
    args = (poolw, pscale, convw, convb, clg, clb, glg, glb, ws, bst)
    return pl.pallas_call(
        functools.partial(_proj_mix_kernel, rows=64, tiles_per_seq=seq // tm),
        grid=(m // tm,),
        in_specs=[row(d), full(g),
                  pl.BlockSpec((None, d, n), lambda i: (layer, 0, 0), pipeline_mode=pl.Buffered(1))]
        + [full(a) for a in args],
        out_specs=[row(d), row(w), row(w), row(w), row(w)],
        out_shape=[jax.ShapeDtypeStruct((m, d), BF16), jax.ShapeDtypeStruct((m, w), F32)]
        + [jax.ShapeDtypeStruct((m, w), BF16)] * 3,
        scratch_shapes=[pltpu.VMEM((CONV_HALO + tm, w), F32), pltpu.VMEM((POOL_HALO + tm, w), F32),
                        pltpu.VMEM((7, CONV_HALO - 8 + tm, w), F32)],
        compiler_params=_cparams(("arbitrary",)),
        name="proj_mix",
    )(x2, g, w_in, *args)


PREP_GB = 8


def _ssm_prep_kernel(ar_ref, ai_ref, ldt_ref, cre_ref, cim_ref, bre_ref, bim_ref,
                     toep_ref, bpow_ref, cpow_ref, lama_ref, lamb_ref, cl_s, bl_s):
    n = SSM_STATE
    p = SSM_GROUP
    lanes = SSM_LANES
    lane = lax.broadcasted_iota(jnp.int32, (1, lanes), 1)
    for e in range(PREP_GB):
        ar = ar_ref[e]
        ai = ai_ref[e]
        dt = jnp.exp(ldt_ref[e])
        mag = jnp.exp(ar * dt)
        ang = ai * dt
        lr = mag * jnp.cos(ang)
        li = mag * jnp.sin(ang)
        den = ar * ar + ai * ai
        fr = ((lr - 1.0) * ar + li * ai) / den
        fi = (li * ar - (lr - 1.0) * ai) / den
        cre = cre_ref[e]
        cim = cim_ref[e]
        br = bre_ref[e]
        bi = bim_ref[e]
        bbr = fr * br - fi * bi
        bbi = fr * bi + fi * br
        pr = jnp.ones_like(lr)
        pi = jnp.zeros_like(lr)
        for k in range(SSM_T + 1):
            cl_s[k * p:(k + 1) * p, 0:n] = cre * pr - cim * pi
            cl_s[k * p:(k + 1) * p, n:2 * n] = -(cre * pi + cim * pr)
            if k < SSM_T:
                s = SSM_T - 1 - k
                bl_s[s * p:(s + 1) * p, 0:n] = pr * bbr - pi * bbi
                bl_s[s * p:(s + 1) * p, n:2 * n] = pr * bbi + pi * bbr
                pr, pi = pr * lr - pi * li, pr * li + pi * lr
        qr, qi = pr, pi
        for j in range(8):
            if j < SSM_SCAN_STEPS:
                lama_ref[e, j:j + 1, 0:n] = qr
                lama_ref[e, j:j + 1, n:2 * n] = qr
                lamb_ref[e, j:j + 1, 0:n] = -qi
                lamb_ref[e, j:j + 1, n:2 * n] = qi
                qr, qi = qr * qr - qi * qi, 2.0 * (qr * qi)
            else:
                lama_ref[e, j:j + 1, :] = jnp.zeros((1, 2 * n), F32)
                lamb_ref[e, j:j + 1, :] = jnp.zeros((1, 2 * n), F32)
        cpow_ref[e] = cl_s[p:(SSM_T + 1) * p, :].astype(cpow_ref.dtype)
        bpow_ref[e] = bl_s[...].astype(bpow_ref.dtype)
        bbcat = bl_s[(SSM_T - 1) * p:SSM_T * p, :]
        kt = lax.dot_general(bbcat, cl_s[0:SSM_T * p, :], (((1,), (1,)), ((), ())),
                             preferred_element_type=F32, precision=lax.Precision.HIGHEST)
        for s in range(SSM_T):
            blk = pltpu.roll(kt, s * p, axis=1) if s else kt
            toep_ref[e, s * p:(s + 1) * p, :] = jnp.where(lane >= s * p, blk, 0.0).astype(toep_ref.dtype)


def _ssm_prep(a_re, a_im, log_dt, b_re, b_im, c_re, c_im):
    n, p = SSM_STATE, SSM_GROUP
    gg = a_re.shape[0] * a_re.shape[1]
    gb = PREP_GB
    lanes = SSM_LANES
    row = lambda a: a.reshape(gg, 1, n)
    ldt = jnp.broadcast_to(log_dt.reshape(gg, 1), (gg, n))
    bt = lambda b: jnp.transpose(b.reshape(gg, n, p), (0, 2, 1))
    rspec = pl.BlockSpec((gb, 1, n), lambda g: (g, 0, 0))
    pspec = pl.BlockSpec((gb, p, n), lambda g: (g, 0, 0))
    ospec = lambda a, b: pl.BlockSpec((gb, a, b), lambda g: (g, 0, 0))
    return pl.pallas_call(
        _ssm_prep_kernel,
        grid=(gg // gb,),
        in_specs=[rspec, rspec, rspec, pspec, pspec, pspec, pspec],
        out_specs=[ospec(lanes, lanes), ospec(lanes, 2 * n), ospec(lanes, 2 * n), ospec(8, 2 * n), ospec(8, 2 * n)],
        out_shape=[
            jax.ShapeDtypeStruct((gg, lanes, lanes), BF16),
            jax.ShapeDtypeStruct((gg, lanes, 2 * n), BF16),
            jax.ShapeDtypeStruct((gg, lanes, 2 * n), BF16),
            jax.ShapeDtypeStruct((gg, 8, 2 * n), F32),
            jax.ShapeDtypeStruct((gg, 8, 2 * n), F32),
        ],
        scratch_shapes=[pltpu.VMEM(((SSM_T + 1) * p, 2 * n), F32), pltpu.VMEM((SSM_T * p, 2 * n), F32)],
        compiler_params=_cparams(("arbitrary",)),
        name="ssm_prep",
    )(row(a_re), row(a_im), row(ldt), c_re.reshape(gg, p, n), c_im.reshape(gg, p, n), bt(b_re), bt(b_im))


SSM_GB = 128 // SSM_GROUP
SSM_RB = 16


def _piece_transpose(xs, masks):
    n = len(xs)
    w = 128 // n
    rolled = []
    for k in range(n):
        wk = xs[k % n]
        for a in range(1, n):
            wk = jnp.where(masks[a], xs[(a + k) % n], wk)
        rolled.append(pltpu.roll(wk, w * k, axis=1) if k else wk)
    ys = []
    for a in range(n):
        ya = rolled[(0 - a) % n]
        for b in range(1, n):
            ya = jnp.where(masks[b], rolled[(b - a) % n], ya)
        ys.append(ya)
    return ys


def _ssm_kernel(u_ref, toep_ref, bpow_ref, cpow_ref, lama_ref, lamb_ref, y_ref, up_ref, yp_ref, xs_ref,
                *, nchunk, pad):
    rows = up_ref.shape[1]
    n2 = 2 * SSM_STATE
    t = SSM_T
    lane = lax.broadcasted_iota(jnp.int32, (1, 128), 1)
    masks = [lane // SSM_GROUP == a for a in range(SSM_GB)]
    nhalf = t // SSM_GB

    def to_chunks(rb, carry):
        r = pl.multiple_of(rb * SSM_RB, SSM_RB)
        for hf in range(nhalf):
            xs = [u_ref[pl.ds(r * t + hf * SSM_GB + b, SSM_RB, stride=t), :] for b in range(SSM_GB)]
            ys = _piece_transpose(xs, masks)
            for a in range(SSM_GB):
                up_ref[a, pl.ds(r, SSM_RB), hf * 128:(hf + 1) * 128] = ys[a].astype(up_ref.dtype)
        return carry
    lax.fori_loop(0, rows // SSM_RB, to_chunks, 0, unroll=2)

    cidx = lax.broadcasted_iota(jnp.int32, (rows, 1), 0) % nchunk
    xs_ref[0:pad, :] = jnp.zeros((pad, n2), F32)
    for a in range(SSM_GB):
        u = up_ref[a]
        xs_ref[pad:pad + rows, :] = jnp.dot(u, bpow_ref[a], preferred_element_type=F32)
        for j in range(SSM_SCAN_STEPS):
            sh = 1 << j
            cur = xs_ref[pad:pad + rows, :]
            prev = jnp.where(cidx >= sh, xs_ref[pad - sh:pad - sh + rows, :], 0.0)
            la = lama_ref[a, j:j + 1, :]
            lb = lamb_ref[a, j:j + 1, :]
            xs_ref[pad:pad + rows, :] = cur + prev * la + pltpu.roll(prev, SSM_STATE, axis=1) * lb
        xprev = jnp.where(cidx >= 1, xs_ref[pad - 1:pad - 1 + rows, :], 0.0).astype(BF16)
        yp_ref[a] = (jnp.dot(u, toep_ref[a], preferred_element_type=F32)
                     + lax.dot_general(xprev, cpow_ref[a], (((1,), (1,)), ((), ())),
                                       preferred_element_type=F32))

    def to_tokens(rb, carry):
        r = pl.multiple_of(rb * 8, 8)
        for hf in range(nhalf):
            ys = [yp_ref[a, pl.ds(r, 8), hf * 128:(hf + 1) * 128] for a in range(SSM_GB)]
            xs = _piece_transpose(ys, masks)
            for b in range(SSM_GB):
                y_ref[pl.ds(r * t + hf * SSM_GB + b, 8, stride=t), :] = xs[b]
        return carry
    lax.fori_loop(0, rows // 8, to_tokens, 0, unroll=2)


def _ssm_scan(pm, toep, bpow, cpow, lama, lamb, layer, *, nchunk):
    m = pm.shape[0]
    rows = m // SSM_T
    lanes = SSM_LANES
    n2 = 2 * SSM_STATE
    pad = 1 << (SSM_SCAN_STEPS - 1)
    nblk = W_BRANCH // 128
    gspec = lambda a, b: pl.BlockSpec((SSM_GB, a, b), lambda i: (layer * nblk + i, 0, 0))
    return pl.pallas_call(
        functools.partial(_ssm_kernel, nchunk=nchunk, pad=pad),
        grid=(nblk,),
        in_specs=[pl.BlockSpec((m, 128), lambda i: (0, i)),
                  gspec(lanes, lanes), gspec(lanes, n2), gspec(lanes, n2), gspec(8, n2), gspec(8, n2)],
        out_specs=pl.BlockSpec((m, 128), lambda i: (0, i)),
        out_shape=jax.ShapeDtypeStruct((m, W_BRANCH), F32),
        scratch_shapes=[pltpu.VMEM((SSM_GB, rows, lanes), BF16), pltpu.VMEM((SSM_GB, rows, lanes), F32),
                        pltpu.VMEM((pad + rows, n2), F32)],
        compiler_params=_cparams(("parallel",)),
        name="ssm_scan",
    )(pm, toep, bpow, cpow, lama, lamb)


def _ssm_post_kernel(yraw_ref, u_ref, d_ref, wglu_ref, o_ref):
    w = W_BRANCH
    y = yraw_ref[...] + d_ref[...] * u_ref[...]
    zg = jnp.dot(_gelu(y).astype(BF16), wglu_ref[...], preferred_element_type=F32)
    o_ref[...] = (zg[:, :w] * _sigmoid(zg[:, w:])).astype(o_ref.dtype)


def _ssm_post(yraw, u, d, wglu, *, tm=512):
    m, w = yraw.shape
    row = pl.BlockSpec((tm, w), lambda i: (i, 0))
    return pl.pallas_call(
        _ssm_post_kernel,
        grid=(m // tm,),
        in_specs=[row, row, pl.BlockSpec((1, w), lambda i: (0, 0)), pl.BlockSpec(wglu.shape, lambda i: (0, 0))],
        out_specs=row,
        out_shape=jax.ShapeDtypeStruct((m, w), BF16),
        compiler_params=_cparams(("parallel",)),
        name="ssm_post",
    )(yraw, u, d, wglu)


def _merge_kernel(*refs, n_side):
    nb = N_BRANCH
    h_ref = refs[0]
    ys = refs[1:1 + nb]
    wg = refs[1 + nb:1 + 2 * nb]
    wb = refs[1 + 2 * nb:1 + 3 * nb]
    side_in = refs[1 + 3 * nb:1 + 3 * nb + n_side]
    o_ref = refs[1 + 3 * nb + n_side]
    side_out = refs[2 + 3 * nb + n_side:2 + 3 * nb + 2 * n_side]
    h = h_ref[...]
    acc = None
    for k in range(nb):
        gate = jnp.dot(h, wg[k][...], preferred_element_type=F32)
        yb = jnp.dot(ys[k][...], wb[k][0], preferred_element_type=F32)
        term = yb * _sigmoid(gate)
        acc = term if acc is None else acc + term
    o_ref[...] = acc.astype(o_ref.dtype)
    _side_cast(side_in, side_out)


def _merge(h, ys, wgate, wbranch, side_weights, side_layer, *, tm=1024, tn=256):
    m, d = h.shape
    w = W_BRANCH
    nj = d // tn
    j0 = OFF_GATE // tn
    gate_spec = lambda k: pl.BlockSpec((None, d, tn), lambda i, j: (0, 0, j0 + k * nj + j))
    br_spec = lambda k: pl.BlockSpec((1, w, tn), lambda i, j: (k, 0, j))
    s_in, s_out, s_shape = _side_cast_specs(side_weights, side_layer, (m // tm) * nj, lambda i, j: i * nj + j)
    outs = pl.pallas_call(
        functools.partial(_merge_kernel, n_side=len(side_weights)),
        grid=(m // tm, nj),
        in_specs=[pl.BlockSpec((tm, d), lambda i, j: (i, 0))]
        + [pl.BlockSpec((tm, w), lambda i, j: (i, 0))] * N_BRANCH
        + [gate_spec(k) for k in range(N_BRANCH)]
        + [br_spec(k) for k in range(N_BRANCH)]
        + s_in,
        out_specs=[pl.BlockSpec((tm, tn), lambda i, j: (i, j))] + s_out,
        out_shape=[jax.ShapeDtypeStruct((m, d), BF16)] + s_shape,
        compiler_params=_cparams(("arbitrary", "arbitrary")),
        name="gated_merge",
    )(h, *ys, *([wgate] * N_BRANCH), *([wbranch] * N_BRANCH), *side_weights)
    return outs[0], outs[1:]


def _out_kernel(m_ref, w_ref, x_ref, g_ref, gn_ref, o_ref, hn_ref):
    mix = jnp.dot(m_ref[...], w_ref[...], preferred_element_type=F32)
    x1 = x_ref[...] + _rms(mix, g_ref[...])
    o_ref[...] = x1
    hn_ref[...] = _rms(x1, gn_ref[...]).astype(hn_ref.dtype)


def _out_proj(merged, w_o, x2, g, g_next, layer, *, tm=512):
    m, d = x2.shape
    row = pl.BlockSpec((tm, d), lambda i: (i, 0))
    vec = pl.BlockSpec((1, d), lambda i: (0, 0))
    return pl.pallas_call(
        _out_kernel,
        grid=(m // tm,),
        in_specs=[row, pl.BlockSpec((None, d, d), lambda i: (layer, 0, 0), pipeline_mode=pl.Buffered(1)),
                  row, vec, vec],
        out_specs=[row, row],
        out_shape=[jax.ShapeDtypeStruct((m, d), F32), jax.ShapeDtypeStruct((m, d), BF16)],
        compiler_params=_cparams(("parallel",)),
        name="out_proj",
    )(merged, w_o, x2, g, g_next)


def _ffn_kernel(*refs, rows, n_side):
    x_ref, h_ref, w1_ref, w2_ref, gpost_ref = refs[:5]
    side_in = refs[5:5 + n_side]
    o_ref = refs[5 + n_side]
    side_out = refs[6 + n_side:6 + 2 * n_side]
    j = pl.program_id(1)

    @pl.when(j == 0)
    def _():
        o_ref[...] = jnp.zeros(o_ref.shape, o_ref.dtype)

    a = jnp.dot(h_ref[...], w1_ref[...], preferred_element_type=F32)
    a = jnp.maximum(a, 0.0)
    a = (a * a).astype(BF16)
    nw = 512
    for c in range(o_ref.shape[1] // nw):
        o_ref[:, c * nw:(c + 1) * nw] += jnp.dot(a, w2_ref[:, c * nw:(c + 1) * nw],
                                                 preferred_element_type=F32)
    _side_cast(side_in, side_out)

    @pl.when(j == pl.num_programs(1) - 1)
    def _():
        def body(i, carry):
            r = pl.multiple_of(i * rows, rows)
            o_ref[pl.ds(r, rows), :] = (x_ref[pl.ds(r, rows), :]
                                        + _rms(o_ref[pl.ds(r, rows), :], gpost_ref[...]))
            return carry
        lax.fori_loop(0, x_ref.shape[0] // rows, body, 0, unroll=2)


def _ffn(x2, h2, w1, w2, gpost, side_weights, side_layer, *, tm=1024, tf=512):
    m, d = x2.shape
    f = w1.shape[2]
    nj = f // tf
    s_in, s_out, s_shape = _side_cast_specs(side_weights, side_layer, (m // tm) * nj, lambda i, j: i * nj + j)
    outs = pl.pallas_call(
        functools.partial(_ffn_kernel, rows=64, n_side=len(side_weights)),
        grid=(m // tm, nj),
        in_specs=[
            pl.BlockSpec((tm, d), lambda i, j: (i, 0)),
            pl.BlockSpec((tm, d), lambda i, j: (i, 0)),
            pl.BlockSpec((None, d, tf), lambda i, j: (0, 0, j)),
            pl.BlockSpec((None, tf, d), lambda i, j: (0, j, 0)),
            pl.BlockSpec((1, d), lambda i, j: (0, 0)),
        ] + s_in,
        out_specs=[pl.BlockSpec((tm, d), lambda i, j: (i, 0))] + s_out,
        out_shape=[jax.ShapeDtypeStruct((m, d), F32)] + s_shape,
        compiler_params=_cparams(("arbitrary", "arbitrary")),
        name="ffn",
    )(x2, h2, w1, w2, gpost, *side_weights)
    return outs[0], outs[1:]


def kernel(x, g_pre_mix, w_in, ssm_a_re, ssm_a_im, ssm_log_dt, ssm_b_re, ssm_b_im, ssm_c_re, ssm_c_im,
           ssm_d, ssm_w_glu, pool_w, pool_scale, conv_w, conv_b, conv_ln_g, conv_ln_b, gmlp_ln_g,
           gmlp_ln_b, gmlp_ws, gmlp_bs, w_branch, w_o, g_post_mix, g_pre_mlp, w_ff1, w_ff2, g_post_mlp):
    nb, seq, d = x.shape
    m = nb * seq
    depth = w_in.shape[0]
    nchunk = seq // SSM_T
    assert d == D_MODEL and nchunk == 1 << SSM_SCAN_STEPS

    toep, bpow, cpow, lama, lamb = _ssm_prep(ssm_a_re, ssm_a_im, ssm_log_dt, ssm_b_re, ssm_b_im,
                                             ssm_c_re, ssm_c_im)

    w_br2 = w_branch.reshape(depth, N_BRANCH * W_BRANCH, d)
    w_in_b = _cast_bf16(w_in, 0, 256)
    w_br_b = _cast_bf16(w_br2, 0, 1024)
    w_o_b = _cast_bf16(w_o, 0, 1024)

    row = lambda a: a.reshape(1, -1)
    x2 = x.reshape(m, d)
    for l in range(depth):
        h, u, y_pool, y_conv, y_gmlp = _proj_mix(
            x2, row(g_pre_mix[l]), w_in_b, 0, pool_w[l].astype(BF16), row(pool_scale[l]),
            jnp.broadcast_to(conv_w[l][:, None, :], (CONV_WIDTH, 8, W_BRANCH)),
            row(conv_b[l]), row(conv_ln_g[l]), row(conv_ln_b[l]), row(gmlp_ln_g[l]), row(gmlp_ln_b[l]),
            gmlp_ws[l], jnp.transpose(gmlp_bs[l]), seq=seq)
        yraw = _ssm_scan(u, toep, bpow, cpow, lama, lamb, l, nchunk=nchunk)
        y_ssm = _ssm_post(yraw, u, row(ssm_d[l]), ssm_w_glu[l].astype(BF16))
        merged, (w_ff1_b, w_ff2_b) = _merge(h, (y_ssm, y_pool, y_conv, y_gmlp), w_in_b,
                                            w_br_b.reshape(N_BRANCH, W_BRANCH, d), (w_ff1, w_ff2), l)
        x2, h2 = _out_proj(merged, w_o_b, x2, row(g_post_mix[l]), row(g_pre_mlp[l]), 0)
        nxt = (w_in, w_br2, w_o) if l + 1 < depth else ()
        x2, nxt_b = _ffn(x2, h2, w_ff1_b, w_ff2_b, row(g_post_mlp[l]), nxt, l + 1)
        if nxt:
            w_in_b, w_br_b, w_o_b = nxt_b
    return x2.reshape(nb, seq, d)
```

```python
import functools

import jax
import jax.numpy as jnp
from jax import lax
from jax.experimental import pallas as pl
from jax.experimental.pallas import tpu as pltpu

D_MODEL = 2048
DEPTH = 2
N_BRANCH = 4
W_BRANCH = D_MODEL // 4
SSM_GROUP = 16
SSM_GROUPS = W_BRANCH // SSM_GROUP
SSM_STATE = 64
POOL_WINDOWS = (2, 4, 8, 16)
POOL_GW = W_BRANCH // len(POOL_WINDOWS)
CONV_WIDTH = 31
GMLP_CHUNK = 128
GMLP_HEADS = 4
GMLP_HD = W_BRANCH // GMLP_HEADS
D_FF = 4 * D_MODEL
EPS = 1e-6

OFF_SSM = 0
OFF_POOL = OFF_SSM + W_BRANCH
OFF_CONV = OFF_POOL + W_BRANCH
OFF_GMLP = OFF_CONV + 2 * W_BRANCH
OFF_GATE = OFF_GMLP + 2 * W_BRANCH

SSM_T = 16
SSM_LANES = SSM_T * SSM_GROUP
SSM_SCAN_STEPS = 7
CONV_HALO = 32
POOL_HALO = 8 * len(POOL_WINDOWS)
assert all(win == 2 << k for k, win in enumerate(POOL_WINDOWS))
CONV_RB = 64

VMEM_LIMIT = 56 * 1024 * 1024
BF16 = jnp.bfloat16
F32 = jnp.float32


def _cparams(sem):
    return pltpu.CompilerParams(dimension_semantics=sem, vmem_limit_bytes=VMEM_LIMIT)


def _sigmoid(x):
    return 0.5 * (1.0 + jnp.tanh(0.5 * x))


def _gelu(x):
    c = 0.7978845608028654
    return 0.5 * x * (1.0 + jnp.tanh(c * (x + 0.044715 * (x * x * x))))


def _rms(x, g):
    ms = jnp.mean(x * x, axis=-1, keepdims=True)
    return x * lax.rsqrt(ms + EPS) * g


def _layer_norm(x, g, b):
    mu = jnp.mean(x, axis=-1, keepdims=True)
    xc = x - mu
    var = jnp.mean(xc * xc, axis=-1, keepdims=True)
    return xc * lax.rsqrt(var + EPS) * g + b


def _cast_kernel(w_ref, o_ref):
    o_ref[...] = w_ref[...].astype(o_ref.dtype)


def _cast_bf16(w, layer, rows):
    _, r, c = w.shape
    return pl.pallas_call(
        _cast_kernel,
        grid=(r // rows,),
        in_specs=[pl.BlockSpec((1, rows, c), lambda i: (layer, i, 0))],
        out_specs=pl.BlockSpec((1, rows, c), lambda i: (0, i, 0)),
        out_shape=jax.ShapeDtypeStruct((1, r, c), BF16),
        compiler_params=_cparams(("parallel",)),
        name="cast_bf16",
    )(w)


def _side_cast_specs(weights, layer, nsteps, step_of):
    in_specs, out_specs, out_shapes = [], [], []
    for w in weights:
        _, r, c = w.shape
        assert r % nsteps == 0 and (r // nsteps) % 16 == 0
        rows = r // nsteps
        in_specs.append(pl.BlockSpec((None, rows, c), lambda *ids: (layer, step_of(*ids), 0)))
        out_specs.append(pl.BlockSpec((None, rows, c), lambda *ids: (0, step_of(*ids), 0)))
        out_shapes.append(jax.ShapeDtypeStruct((1, r, c), BF16))
    return in_specs, out_specs, out_shapes


def _side_cast(in_refs, out_refs):
    for wi, wo in zip(in_refs, out_refs):
        wo[...] = wi[...].astype(wo.dtype)


def _proj_mix_kernel(x_ref, g_ref, w_ref, poolw_ref, pscale_ref, convw_ref, convb_ref, clg_ref, clb_ref,
                     glg_ref, glb_ref, ws_ref, bst_ref,
                     h_ref, u_ref, ypool_ref, yconv_ref, ygmlp_ref, vbuf, pbuf, sbuf, qbuf, *, rows, tiles_per_seq):
    c = pl.program_id(0) % tiles_per_seq
    tc = x_ref.shape[0]
    w = W_BRANCH

    @pl.when(c == 0)
    def _():
        vbuf[0:CONV_HALO, :] = jnp.zeros((CONV_HALO, w), F32)
        pbuf[0:POOL_HALO, :] = jnp.zeros((POOL_HALO, w), F32)

    def body(i, carry):
        r = pl.multiple_of(i * rows, rows)
        h_ref[pl.ds(r, rows), :] = _rms(x_ref[pl.ds(r, rows), :], g_ref[...]).astype(h_ref.dtype)
        return carry
    lax.fori_loop(0, tc // rows, body, 0, unroll=2)

    h = h_ref[...]
    proj = lambda off, width: jnp.dot(h, w_ref[:, off:off + width], preferred_element_type=F32)
    u_ref[...] = proj(OFF_SSM, w)

    pbuf[POOL_HALO:POOL_HALO + tc, :] = proj(OFF_POOL, w)
    src = pbuf
    for k in range(len(POOL_WINDOWS)):
        lo, sh, c0 = 8 * (k + 1), 1 << k, k * POOL_GW
        n = POOL_HALO + tc - lo
        a_hi = src[lo:lo + n, c0:] if k == 0 else src[k - 1, lo:lo + n, c0:]
        a_lo = src[lo - sh:lo - sh + n, c0:] if k == 0 else src[k - 1, lo - sh:lo - sh + n, c0:]
        qbuf[k, lo:lo + n, c0:] = a_hi + a_lo
        src = qbuf
    t = c * tc + lax.broadcasted_iota(jnp.int32, (tc, 1), 0)
    for gi, win in enumerate(POOL_WINDOWS):
        cols = slice(gi * POOL_GW, (gi + 1) * POOL_GW)
        tok = pbuf[POOL_HALO:POOL_HALO + tc, cols]
        s = qbuf[gi, POOL_HALO:POOL_HALO + tc, cols]
        count = jnp.minimum(t + 1, win).astype(F32)
        pooled = s / count - tok
        mixed = jnp.dot(pooled.astype(BF16), poolw_ref[gi], preferred_element_type=F32)
        ypool_ref[:, cols] = (mixed * pscale_ref[:, cols]).astype(ypool_ref.dtype)
    pbuf[0:POOL_HALO, :] = pbuf[tc:tc + POOL_HALO, :]

    cv = proj(OFF_CONV, 2 * w)
    vbuf[CONV_HALO:CONV_HALO + tc, :] = cv[:, :w] * _sigmoid(cv[:, w:])
    span = CONV_HALO - 8
    for b in range(1, 8):
        sbuf[b - 1] = vbuf[8 - b:8 - b + span + tc, :]
    for rb in range(tc // CONV_RB):
        r0 = rb * CONV_RB
        acc = jnp.broadcast_to(convb_ref[...], (CONV_RB, w))
        for b in range(8):
            for a in range((CONV_WIDTH - 1 - b) // 8 + 1):
                k = CONV_WIDTH - 1 - (8 * a + b)
                lo = span - 8 * a + r0
                tap = sbuf[b - 1, lo:lo + CONV_RB, :] if b else vbuf[8 + lo:8 + lo + CONV_RB, :]
                wk = convw_ref[k]
                acc = acc + (tap.reshape(CONV_RB // 8, 8, w) * wk[None]).reshape(CONV_RB, w)
        yn = _layer_norm(acc, clg_ref[...], clb_ref[...])
        yconv_ref[r0:r0 + CONV_RB, :] = (yn * _sigmoid(yn)).astype(yconv_ref.dtype)
    vbuf[0:CONV_HALO, :] = vbuf[tc:tc + CONV_HALO, :]

    gm = proj(OFF_GMLP, 2 * w)
    ug = _gelu(gm[:, :w])
    vg = _layer_norm(_gelu(gm[:, w:]), glg_ref[...], glb_ref[...]).astype(BF16)
    ri = lax.broadcasted_iota(jnp.int32, (GMLP_CHUNK, GMLP_CHUNK), 0)
    ci = lax.broadcasted_iota(jnp.int32, (GMLP_CHUNK, GMLP_CHUNK), 1)
    causal = ri >= ci
    for hd in range(GMLP_HEADS):
        wsm = jnp.where(causal, ws_ref[hd], 0.0).astype(BF16)
        cols = slice(hd * GMLP_HD, (hd + 1) * GMLP_HD)
        for ch in range(tc // GMLP_CHUNK):
            rws = slice(ch * GMLP_CHUNK, (ch + 1) * GMLP_CHUNK)
            sv = jnp.dot(wsm, vg[rws, cols], preferred_element_type=F32) + bst_ref[:, hd:hd + 1]
            ygmlp_ref[rws, cols] = (ug[rws, cols] * sv).astype(ygmlp_ref.dtype)


def _proj_mix(x2, g, w_in, layer, poolw, pscale, convw, convb, clg, clb, glg, glb, ws, bst, *, seq, tm=512):
    m, d = x2.shape
    n = OFF_GATE
    w = W_BRANCH
    assert seq % tm == 0 and tm % GMLP_CHUNK == 0 and tm % CONV_RB == 0
    row = lambda width: pl.BlockSpec((tm, width), lambda i: (i, 0))
    full = lambda a: pl.BlockSpec(a.shape, lambda i: (0,) * a.ndim)
    args = (poolw, pscale, convw, convb, clg, clb, glg, glb, ws, bst)
    return pl.pallas_call(
        functools.partial(_proj_mix_kernel, rows=128, tiles_per_seq=seq // tm),
        grid=(m // tm,),
        in_specs=[row(d), full(g),
                  pl.BlockSpec((None, d, n), lambda i: (layer, 0, 0), pipeline_mode=pl.Buffered(1))]
        + [full(a) for a in args],
        out_specs=[row(d), row(w), row(w), row(w), row(w)],
        out_shape=[jax.ShapeDtypeStruct((m, d), BF16), jax.ShapeDtypeStruct((m, w), F32)]
        + [jax.ShapeDtypeStruct((m, w), BF16)] * 3,
        scratch_shapes=[pltpu.VMEM((CONV_HALO + tm, w), F32), pltpu.VMEM((POOL_HALO + tm, w), F32),
                        pltpu.VMEM((7, CONV_HALO - 8 + tm, w), F32),
                        pltpu.VMEM((len(POOL_WINDOWS), POOL_HALO + tm, w), F32)],
        compiler_params=_cparams(("arbitrary",)),
        name="proj_mix",
    )(x2, g, w_in, *args)


PREP_GB = 8


def _ssm_prep_kernel(ar_ref, ai_ref, ldt_ref, cre_ref, cim_ref, bre_ref, bim_ref,
                     toep_ref, bpow_ref, cpow_ref, lama_ref, lamb_ref, cl_s, bl_s):
    n = SSM_STATE
    p = SSM_GROUP
    lanes = SSM_LANES
    lane = lax.broadcasted_iota(jnp.int32, (1, lanes), 1)
    for e in range(PREP_GB):
        ar = ar_ref[e]
        ai = ai_ref[e]
        dt = jnp.exp(ldt_ref[e])
        mag = jnp.exp(ar * dt)
        ang = ai * dt
        lr = mag * jnp.cos(ang)
        li = mag * jnp.sin(ang)
        den = ar * ar + ai * ai
        fr = ((lr - 1.0) * ar + li * ai) / den
        fi = (li * ar - (lr - 1.0) * ai) / den
        cre = cre_ref[e]
        cim = cim_ref[e]
        br = bre_ref[e]
        bi = bim_ref[e]
        bbr = fr * br - fi * bi
        bbi = fr * bi + fi * br
        pr = jnp.ones_like(lr)
        pi = jnp.zeros_like(lr)
        for k in range(SSM_T + 1):
            cl_s[k * p:(k + 1) * p, 0:n] = cre * pr - cim * pi
            cl_s[k * p:(k + 1) * p, n:2 * n] = -(cre * pi + cim * pr)
            if k < SSM_T:
                s = SSM_T - 1 - k
                bl_s[s * p:(s + 1) * p, 0:n] = pr * bbr - pi * bbi
                bl_s[s * p:(s + 1) * p, n:2 * n] = pr * bbi + pi * bbr
                pr, pi = pr * lr - pi * li, pr * li + pi * lr
        qr, qi = pr, pi
        for j in range(8):
            if j < SSM_SCAN_STEPS:
                lama_ref[e, j:j + 1, 0:n] = qr
                lama_ref[e, j:j + 1, n:2 * n] = qr
                lamb_ref[e, j:j + 1, 0:n] = -qi
                lamb_ref[e, j:j + 1, n:2 * n] = qi
                qr, qi = qr * qr - qi * qi, 2.0 * (qr * qi)
            else:
                lama_ref[e, j:j + 1, :] = jnp.zeros((1, 2 * n), F32)
                lamb_ref[e, j:j + 1, :] = jnp.zeros((1, 2 * n), F32)
        cpow_ref[e] = cl_s[p:(SSM_T + 1) * p, :].astype(cpow_ref.dtype)
        bpow_ref[e] = bl_s[...].astype(bpow_ref.dtype)
        bbcat = bl_s[(SSM_T - 1) * p:SSM_T * p, :]
        kt = lax.dot_general(bbcat, cl_s[0:SSM_T * p, :], (((1,), (1,)), ((), ())),
                             preferred_element_type=F32, precision=lax.Precision.HIGHEST)
        for s in range(SSM_T):
            blk = pltpu.roll(kt, s * p, axis=1) if s else kt
            toep_ref[e, s * p:(s + 1) * p, :] = jnp.where(lane >= s * p, blk, 0.0).astype(toep_ref.dtype)


def _ssm_prep(a_re, a_im, log_dt, b_re, b_im, c_re, c_im):
    n, p = SSM_STATE, SSM_GROUP
    gg = a_re.shape[0] * a_re.shape[1]
    gb = PREP_GB
    lanes = SSM_LANES
    row = lambda a: a.reshape(gg, 1, n)
    ldt = jnp.broadcast_to(log_dt.reshape(gg, 1), (gg, n))
    bt = lambda b: jnp.transpose(b.reshape(gg, n, p), (0, 2, 1))
    rspec = pl.BlockSpec((gb, 1, n), lambda g: (g, 0, 0))
    pspec = pl.BlockSpec((gb, p, n), lambda g: (g, 0, 0))
    ospec = lambda a, b: pl.BlockSpec((gb, a, b), lambda g: (g, 0, 0))
    return pl.pallas_call(
        _ssm_prep_kernel,
        grid=(gg // gb,),
        in_specs=[rspec, rspec, rspec, pspec, pspec, pspec, pspec],
        out_specs=[ospec(lanes, lanes), ospec(lanes, 2 * n), ospec(lanes, 2 * n), ospec(8, 2 * n), ospec(8, 2 * n)],
        out_shape=[
            jax.ShapeDtypeStruct((gg, lanes, lanes), BF16),
            jax.ShapeDtypeStruct((gg, lanes, 2 * n), BF16),
            jax.ShapeDtypeStruct((gg, lanes, 2 * n), BF16),
            jax.ShapeDtypeStruct((gg, 8, 2 * n), F32),
            jax.ShapeDtypeStruct((gg, 8, 2 * n), F32),
        ],
        scratch_shapes=[pltpu.VMEM(((SSM_T + 1) * p, 2 * n), F32), pltpu.VMEM((SSM_T * p, 2 * n), F32)],
        compiler_params=_cparams(("arbitrary",)),
        name="ssm_prep",
    )(row(a_re), row(a_im), row(ldt), c_re.reshape(gg, p, n), c_im.reshape(gg, p, n), bt(b_re), bt(b_im))


SSM_GB = 128 // SSM_GROUP
SSM_RB = 16


def _piece_transpose(xs, masks):
    n = len(xs)
    w = 128 // n
    rolled = []
    for k in range(n):
        wk = xs[k % n]
        for a in range(1, n):
            wk = jnp.where(masks[a], xs[(a + k) % n], wk)
        rolled.append(pltpu.roll(wk, w * k, axis=1) if k else wk)
    ys = []
    for a in range(n):
        ya = rolled[(0 - a) % n]
        for b in range(1, n):
            ya = jnp.where(masks[b], rolled[(b - a) % n], ya)
        ys.append(ya)
    return ys


def _ssm_kernel(u_ref, toep_ref, bpow_ref, cpow_ref, lama_ref, lamb_ref, y_ref, up_ref, yp_ref, xs_ref,
                *, nchunk, pad):
    rows = up_ref.shape[1]
    n2 = 2 * SSM_STATE
    t = SSM_T
    lane = lax.broadcasted_iota(jnp.int32, (1, 128), 1)
    masks = [lane // SSM_GROUP == a for a in range(SSM_GB)]
    nhalf = t // SSM_GB

    def to_chunks(rb, carry):
        r = pl.multiple_of(rb * SSM_RB, SSM_RB)
        for hf in range(nhalf):
            xs = [u_ref[pl.ds(r * t + hf * SSM_GB + b, SSM_RB, stride=t), :] for b in range(SSM_GB)]
            ys = _piece_transpose(xs, masks)
            for a in range(SSM_GB):
                up_ref[a, pl.ds(r, SSM_RB), hf * 128:(hf + 1) * 128] = ys[a].astype(up_ref.dtype)
        return carry
    lax.fori_loop(0, rows // SSM_RB, to_chunks, 0, unroll=2)

    cidx = lax.broadcasted_iota(jnp.int32, (rows, 1), 0) % nchunk
    xs_ref[0:pad, :] = jnp.zeros((pad, n2), F32)
    for a in range(SSM_GB):
        u = up_ref[a]
        xs_ref[pad:pad + rows, :] = jnp.dot(u, bpow_ref[a], preferred_element_type=F32)
        for j in range(SSM_SCAN_STEPS):
            sh = 1 << j
            cur = xs_ref[pad:pad + rows, :]
            prev = jnp.where(cidx >= sh, xs_ref[pad - sh:pad - sh + rows, :], 0.0)
            la = lama_ref[a, j:j + 1, :]
            lb = lamb_ref[a, j:j + 1, :]
            xs_ref[pad:pad + rows, :] = cur + prev * la + pltpu.roll(prev, SSM_STATE, axis=1) * lb
        xprev = jnp.where(cidx >= 1, xs_ref[pad - 1:pad - 1 + rows, :], 0.0).astype(BF16)
        yp_ref[a] = (jnp.dot(u, toep_ref[a], preferred_element_type=F32)
                     + lax.dot_general(xprev, cpow_ref[a], (((1,), (1,)), ((), ())),
                                       preferred_element_type=F32))

    def to_tokens(rb, carry):
        r = pl.multiple_of(rb * 8, 8)
        for hf in range(nhalf):
            ys = [yp_ref[a, pl.ds(r, 8), hf * 128:(hf + 1) * 128] for a in range(SSM_GB)]
            xs = _piece_transpose(ys, masks)
            for b in range(SSM_GB):
                y_ref[pl.ds(r * t + hf * SSM_GB + b, 8, stride=t), :] = xs[b]
        return carry
    lax.fori_loop(0, rows // 8, to_tokens, 0, unroll=2)


def _ssm_scan(pm, toep, bpow, cpow, lama, lamb, layer, *, nchunk):
    m = pm.shape[0]
    rows = m // SSM_T
    lanes = SSM_LANES
    n2 = 2 * SSM_STATE
    pad = 1 << (SSM_SCAN_STEPS - 1)
    nblk = W_BRANCH // 128
    gspec = lambda a, b: pl.BlockSpec((SSM_GB, a, b), lambda i: (layer * nblk + i, 0, 0))
    return pl.pallas_call(
        functools.partial(_ssm_kernel, nchunk=nchunk, pad=pad),
        grid=(nblk,),
        in_specs=[pl.BlockSpec((m, 128), lambda i: (0, i)),
                  gspec(lanes, lanes), gspec(lanes, n2), gspec(lanes, n2), gspec(8, n2), gspec(8, n2)],
        out_specs=pl.BlockSpec((m, 128), lambda i: (0, i)),
        out_shape=jax.ShapeDtypeStruct((m, W_BRANCH), F32),
        scratch_shapes=[pltpu.VMEM((SSM_GB, rows, lanes), BF16), pltpu.VMEM((SSM_GB, rows, lanes), F32),
                        pltpu.VMEM((pad + rows, n2), F32)],
        compiler_params=_cparams(("parallel",)),
        name="ssm_scan",
    )(pm, toep, bpow, cpow, lama, lamb)


def _ssm_post_kernel(yraw_ref, u_ref, d_ref, wglu_ref, o_ref):
    w = W_BRANCH
    y = yraw_ref[...] + d_ref[...] * u_ref[...]
    zg = jnp.dot(_gelu(y).astype(BF16), wglu_ref[...], preferred_element_type=F32)
    o_ref[...] = (zg[:, :w] * _sigmoid(zg[:, w:])).astype(o_ref.dtype)


def _ssm_post(yraw, u, d, wglu, *, tm=512):
    m, w = yraw.shape
    row = pl.BlockSpec((tm, w), lambda i: (i, 0))
    return pl.pallas_call(
        _ssm_post_kernel,
        grid=(m // tm,),
        in_specs=[row, row, pl.BlockSpec((1, w), lambda i: (0, 0)), pl.BlockSpec(wglu.shape, lambda i: (0, 0))],
        out_specs=row,
        out_shape=jax.ShapeDtypeStruct((m, w), BF16),
        compiler_params=_cparams(("parallel",)),
        name="ssm_post",
    )(yraw, u, d, wglu)


def _merge_kernel(*refs, n_side):
    nb = N_BRANCH
    h_ref = refs[0]
    ys = refs[1:1 + nb]
    wg = refs[1 + nb:1 + 2 * nb]
    wb = refs[1 + 2 * nb:1 + 3 * nb]
    side_in = refs[1 + 3 * nb:1 + 3 * nb + n_side]
    o_ref = refs[1 + 3 * nb + n_side]
    side_out = refs[2 + 3 * nb + n_side:2 + 3 * nb + 2 * n_side]
    h = h_ref[...]
    acc = None
    for k in range(nb):
        gate = jnp.dot(h, wg[k][...], preferred_element_type=F32)
        yb = jnp.dot(ys[k][...], wb[k][0], preferred_element_type=F32)
        term = yb * _sigmoid(gate)
        acc = term if acc is None else acc + term
    o_ref[...] = acc.astype(o_ref.dtype)
    _side_cast(side_in, side_out)


def _merge(h, ys, wgate, wbranch, side_weights, side_layer, *, tm=1024, tn=256):
    m, d = h.shape
    w = W_BRANCH
    nj = d // tn
    j0 = OFF_GATE // tn
    gate_spec = lambda k: pl.BlockSpec((None, d, tn), lambda i, j: (0, 0, j0 + k * nj + j))
    br_spec = lambda k: pl.BlockSpec((1, w, tn), lambda i, j: (k, 0, j))
    s_in, s_out, s_shape = _side_cast_specs(side_weights, side_layer, (m // tm) * nj, lambda i, j: i * nj + j)
    outs = pl.pallas_call(
        functools.partial(_merge_kernel, n_side=len(side_weights)),
        grid=(m // tm, nj),
        in_specs=[pl.BlockSpec((tm, d), lambda i, j: (i, 0))]
        + [pl.BlockSpec((tm, w), lambda i, j: (i, 0))] * N_BRANCH
        + [gate_spec(k) for k in range(N_BRANCH)]
        + [br_spec(k) for k in range(N_BRANCH)]
        + s_in,
        out_specs=[pl.BlockSpec((tm, tn), lambda i, j: (i, j))] + s_out,
        out_shape=[jax.ShapeDtypeStruct((m, d), BF16)] + s_shape,
        compiler_params=_cparams(("arbitrary", "arbitrary")),
        name="gated_merge",
    )(h, *ys, *([wgate] * N_BRANCH), *([wbranch] * N_BRANCH), *side_weights)
    return outs[0], outs[1:]


def _out_kernel(m_ref, w_ref, x_ref, g_ref, gn_ref, o_ref, hn_ref):
    mix = jnp.dot(m_ref[...], w_ref[...], preferred_element_type=F32)
    x1 = x_ref[...] + _rms(mix, g_ref[...])
    o_ref[...] = x1
    hn_ref[...] = _rms(x1, gn_ref[...]).astype(hn_ref.dtype)


def _out_proj(merged, w_o, x2, g, g_next, layer, *, tm=512):
    m, d = x2.shape
    row = pl.BlockSpec((tm, d), lambda i: (i, 0))
    vec = pl.BlockSpec((1, d), lambda i: (0, 0))
    return pl.pallas_call(
        _out_kernel,
        grid=(m // tm,),
        in_specs=[row, pl.BlockSpec((None, d, d), lambda i: (layer, 0, 0), pipeline_mode=pl.Buffered(1)),
                  row, vec, vec],
        out_specs=[row, row],
        out_shape=[jax.ShapeDtypeStruct((m, d), F32), jax.ShapeDtypeStruct((m, d), BF16)],
        compiler_params=_cparams(("parallel",)),
        name="out_proj",
    )(merged, w_o, x2, g, g_next)


def _ffn_kernel(*refs, rows, n_side):
    x_ref, h_ref, w1_ref, w2_ref, gpost_ref = refs[:5]
    side_in = refs[5:5 + n_side]
    o_ref = refs[5 + n_side]
    side_out = refs[6 + n_side:6 + 2 * n_side]
    j = pl.program_id(1)

    @pl.when(j == 0)
    def _():
        o_ref[...] = jnp.zeros(o_ref.shape, o_ref.dtype)

    a = jnp.dot(h_ref[...], w1_ref[...], preferred_element_type=F32)
    a = jnp.maximum(a, 0.0)
    a = (a * a).astype(BF16)
    nw = 512
    for c in range(o_ref.shape[1] // nw):
        o_ref[:, c * nw:(c + 1) * nw] += jnp.dot(a, w2_ref[:, c * nw:(c + 1) * nw],
                                                 preferred_element_type=F32)
    _side_cast(side_in, side_out)

    @pl.when(j == pl.num_programs(1) - 1)
    def _():
        def body(i, carry):
            r = pl.multiple_of(i * rows, rows)
            o_ref[pl.ds(r, rows), :] = (x_ref[pl.ds(r, rows), :]
                                        + _rms(o_ref[pl.ds(r, rows), :], gpost_ref[...]))
            return carry
        lax.fori_loop(0, x_ref.shape[0] // rows, body, 0, unroll=2)


def _ffn(x2, h2, w1, w2, gpost, side_weights, side_layer, *, tm=1024, tf=512):
    m, d = x2.shape
    f = w1.shape[2]
    nj = f // tf
    s_in, s_out, s_shape = _side_cast_specs(side_weights, side_layer, (m // tm) * nj, lambda i, j: i * nj + j)
    outs = pl.pallas_call(
        functools.partial(_ffn_kernel, rows=128, n_side=len(side_weights)),
        grid=(m // tm, nj),
        in_specs=[
            pl.BlockSpec((tm, d), lambda i, j: (i, 0)),
            pl.BlockSpec((tm, d), lambda i, j: (i, 0)),
            pl.BlockSpec((None, d, tf), lambda i, j: (0, 0, j)),
            pl.BlockSpec((None, tf, d), lambda i, j: (0, j, 0)),
            pl.BlockSpec((1, d), lambda i, j: (0, 0)),
        ] + s_in,
        out_specs=[pl.BlockSpec((tm, d), lambda i, j: (i, 0))] + s_out,
        out_shape=[jax.ShapeDtypeStruct((m, d), F32)] + s_shape,
        compiler_params=_cparams(("arbitrary", "arbitrary")),
        name="ffn",
    )(x2, h2, w1, w2, gpost, *side_weights)
    return outs[0], outs[1:]


def kernel(x, g_pre_mix, w_in, ssm_a_re, ssm_a_im, ssm_log_dt, ssm_b_re, ssm_b_im, ssm_c_re, ssm_c_im,
           ssm_d, ssm_w_glu, pool_w, pool_scale, conv_w, conv_b, conv_ln_g, conv_ln_b, gmlp_ln_g,
           gmlp_ln_b, gmlp_ws, gmlp_bs, w_branch, w_o, g_post_mix, g_pre_mlp, w_ff1, w_ff2, g_post_mlp):
    nb, seq, d = x.shape
    m = nb * seq
    depth = w_in.shape[0]
    nchunk = seq // SSM_T
    assert d == D_MODEL and nchunk == 1 << SSM_SCAN_STEPS

    toep, bpow, cpow, lama, lamb = _ssm_prep(ssm_a_re, ssm_a_im, ssm_log_dt, ssm_b_re, ssm_b_im,
                                             ssm_c_re, ssm_c_im)

    w_br2 = w_branch.reshape(depth, N_BRANCH * W_BRANCH, d)
    w_in_b = _cast_bf16(w_in, 0, 256)
    w_br_b = _cast_bf16(w_br2, 0, 1024)
    w_o_b = _cast_bf16(w_o, 0, 1024)

    row = lambda a: a.reshape(1, -1)
    x2 = x.reshape(m, d)
    for l in range(depth):
        h, u, y_pool, y_conv, y_gmlp = _proj_mix(
            x2, row(g_pre_mix[l]), w_in_b, 0, pool_w[l].astype(BF16), row(pool_scale[l]),
            jnp.broadcast_to(conv_w[l][:, None, :], (CONV_WIDTH, 8, W_BRANCH)),
            row(conv_b[l]), row(conv_ln_g[l]), row(conv_ln_b[l]), row(gmlp_ln_g[l]), row(gmlp_ln_b[l]),
            gmlp_ws[l], jnp.transpose(gmlp_bs[l]), seq=seq)
        yraw = _ssm_scan(u, toep, bpow, cpow, lama, lamb, l, nchunk=nchunk)
        y_ssm = _ssm_post(yraw, u, row(ssm_d[l]), ssm_w_glu[l].astype(BF16))
        merged, (w_ff1_b, w_ff2_b) = _merge(h, (y_ssm, y_pool, y_conv, y_gmlp), w_in_b,
                                            w_br_b.reshape(N_BRANCH, W_BRANCH, d), (w_ff1, w_ff2), l)
        x2, h2 = _out_proj(merged, w_o_b, x2, row(g_post_mix[l]), row(g_pre_mlp[l]), 0)
        nxt = (w_in, w_br2, w_o) if l + 1 < depth else ()
        x2, nxt_b = _ffn(x2, h2, w_ff1_b, w_ff2_b, row(g_post_mlp[l]), nxt, l + 1)
        if nxt:
            w_in_b, w_br_b, w_o_b = nxt_b
    return x2.reshape(nb, seq, d)
```

```python
import functools

import jax
import jax.numpy as jnp
from jax import lax
from jax.experimental import pallas as pl
from jax.experimental.pallas import tpu as pltpu

D_MODEL = 2048
DEPTH = 2
N_BRANCH = 4
W_BRANCH = D_MODEL // 4
SSM_GROUP = 16
SSM_GROUPS = W_BRANCH // SSM_GROUP
SSM_STATE = 64
POOL_WINDOWS = (2, 4, 8, 16)
POOL_GW = W_BRANCH // len(POOL_WINDOWS)
CONV_WIDTH = 31
GMLP_CHUNK = 128
GMLP_HEADS = 4
GMLP_HD = W_BRANCH // GMLP_HEADS
D_FF = 4 * D_MODEL
EPS = 1e-6

OFF_SSM = 0
OFF_POOL = OFF_SSM + W_BRANCH
OFF_CONV = OFF_POOL + W_BRANCH
OFF_GMLP = OFF_CONV + 2 * W_BRANCH
OFF_GATE = OFF_GMLP + 2 * W_BRANCH

SSM_T = 16
SSM_LANES = SSM_T * SSM_GROUP
SSM_SCAN_STEPS = 7
CONV_HALO = 32
POOL_HALO = 8 * len(POOL_WINDOWS)
assert all(win == 2 << k for k, win in enumerate(POOL_WINDOWS))
CONV_RB = 64

VMEM_LIMIT = 56 * 1024 * 1024
BF16 = jnp.bfloat16
F32 = jnp.float32


def _cparams(sem):
    return pltpu.CompilerParams(dimension_semantics=sem, vmem_limit_bytes=VMEM_LIMIT)


def _sigmoid(x):
    return 0.5 * (1.0 + jnp.tanh(0.5 * x))


def _gelu(x):
    c = 0.7978845608028654
    return 0.5 * x * (1.0 + jnp.tanh(c * (x + 0.044715 * (x * x * x))))


def _rms(x, g):
    ms = jnp.mean(x * x, axis=-1, keepdims=True)
    return x * lax.rsqrt(ms + EPS) * g


def _layer_norm(x, g, b):
    mu = jnp.mean(x, axis=-1, keepdims=True)
    xc = x - mu
    var = jnp.mean(xc * xc, axis=-1, keepdims=True)
    return xc * lax.rsqrt(var + EPS) * g + b


def _cast_kernel(w_ref, o_ref):
    o_ref[...] = w_ref[...].astype(o_ref.dtype)


def _cast_bf16(w, layer, rows):
    _, r, c = w.shape
    return pl.pallas_call(
        _cast_kernel,
        grid=(r // rows,),
        in_specs=[pl.BlockSpec((1, rows, c), lambda i: (layer, i, 0))],
        out_specs=pl.BlockSpec((1, rows, c), lambda i: (0, i, 0)),
        out_shape=jax.ShapeDtypeStruct((1, r, c), BF16),
        compiler_params=_cparams(("parallel",)),
        name="cast_bf16",
    )(w)


def _side_cast_specs(weights, layer, nsteps, step_of):
    in_specs, out_specs, out_shapes = [], [], []
    for w in weights:
        _, r, c = w.shape
        assert r % nsteps == 0 and (r // nsteps) % 16 == 0
        rows = r // nsteps
        in_specs.append(pl.BlockSpec((None, rows, c), lambda *ids: (layer, step_of(*ids), 0)))
        out_specs.append(pl.BlockSpec((None, rows, c), lambda *ids: (0, step_of(*ids), 0)))
        out_shapes.append(jax.ShapeDtypeStruct((1, r, c), BF16))
    return in_specs, out_specs, out_shapes


def _side_cast(in_refs, out_refs):
    for wi, wo in zip(in_refs, out_refs):
        wo[...] = wi[...].astype(wo.dtype)


def _proj_mix_kernel(x_ref, g_ref, w_ref, poolw_ref, pscale_ref, convw_ref, convb_ref, clg_ref, clb_ref,
                     glg_ref, glb_ref, ws_ref, bst_ref,
                     h_ref, u_ref, ypool_ref, yconv_ref, ygmlp_ref, vbuf, pbuf, sbuf, qbuf, *, rows, tiles_per_seq):
    c = pl.program_id(0) % tiles_per_seq
    tc = x_ref.shape[0]
    w = W_BRANCH

    @pl.when(c == 0)
    def _():
        vbuf[0:CONV_HALO, :] = jnp.zeros((CONV_HALO, w), F32)
        pbuf[0:POOL_HALO, :] = jnp.zeros((POOL_HALO, w), F32)

    def body(i, carry):
        r = pl.multiple_of(i * rows, rows)
        h_ref[pl.ds(r, rows), :] = _rms(x_ref[pl.ds(r, rows), :], g_ref[...]).astype(h_ref.dtype)
        return carry
    lax.fori_loop(0, tc // rows, body, 0, unroll=2)

    h = h_ref[...]
    proj = lambda off, width: jnp.dot(h, w_ref[:, off:off + width], preferred_element_type=F32)
    u_ref[...] = proj(OFF_SSM, w)

    pbuf[POOL_HALO:POOL_HALO + tc, :] = proj(OFF_POOL, w)
    src = pbuf
    for k in range(len(POOL_WINDOWS)):
        lo, sh, c0 = 8 * (k + 1), 1 << k, k * POOL_GW
        n = POOL_HALO + tc - lo
        a_hi = src[lo:lo + n, c0:] if k == 0 else src[k - 1, lo:lo + n, c0:]
        a_lo = src[lo - sh:lo - sh + n, c0:] if k == 0 else src[k - 1, lo - sh:lo - sh + n, c0:]
        qbuf[k, lo:lo + n, c0:] = a_hi + a_lo
        src = qbuf
    t = c * tc + lax.broadcasted_iota(jnp.int32, (tc, 1), 0)
    for gi, win in enumerate(POOL_WINDOWS):
        cols = slice(gi * POOL_GW, (gi + 1) * POOL_GW)
        tok = pbuf[POOL_HALO:POOL_HALO + tc, cols]
        s = qbuf[gi, POOL_HALO:POOL_HALO + tc, cols]
        count = jnp.minimum(t + 1, win).astype(F32)
        pooled = s / count - tok
        mixed = jnp.dot(pooled.astype(BF16), poolw_ref[gi], preferred_element_type=F32)
        ypool_ref[:, cols] = (mixed * pscale_ref[:, cols]).astype(ypool_ref.dtype)
    pbuf[0:POOL_HALO, :] = pbuf[tc:tc + POOL_HALO, :]

    cv = proj(OFF_CONV, 2 * w)
    vbuf[CONV_HALO:CONV_HALO + tc, :] = cv[:, :w] * _sigmoid(cv[:, w:])
    span = CONV_HALO - 8
    for b in range(1, 8):
        sbuf[b - 1] = vbuf[8 - b:8 - b + span + tc, :]
    for rb in range(tc // CONV_RB):
        r0 = rb * CONV_RB
        acc = jnp.broadcast_to(convb_ref[...], (CONV_RB, w))
        for b in range(8):
            for a in range((CONV_WIDTH - 1 - b) // 8 + 1):
                k = CONV_WIDTH - 1 - (8 * a + b)
                lo = span - 8 * a + r0
                tap = sbuf[b - 1, lo:lo + CONV_RB, :] if b else vbuf[8 + lo:8 + lo + CONV_RB, :]
                wk = convw_ref[k]
                acc = acc + (tap.reshape(CONV_RB // 8, 8, w) * wk[None]).reshape(CONV_RB, w)
        yn = _layer_norm(acc, clg_ref[...], clb_ref[...])
        yconv_ref[r0:r0 + CONV_RB, :] = (yn * _sigmoid(yn)).astype(yconv_ref.dtype)
    vbuf[0:CONV_HALO, :] = vbuf[tc:tc + CONV_HALO, :]

    gm = proj(OFF_GMLP, 2 * w)
    ug = _gelu(gm[:, :w])
    vg = _layer_norm(_gelu(gm[:, w:]), glg_ref[...], glb_ref[...]).astype(BF16)
    ri = lax.broadcasted_iota(jnp.int32, (GMLP_CHUNK, GMLP_CHUNK), 0)
    ci = lax.broadcasted_iota(jnp.int32, (GMLP_CHUNK, GMLP_CHUNK), 1)
    causal = ri >= ci
    for hd in range(GMLP_HEADS):
        wsm = jnp.where(causal, ws_ref[hd], 0.0).astype(BF16)
        cols = slice(hd * GMLP_HD, (hd + 1) * GMLP_HD)
        for ch in range(tc // GMLP_CHUNK):
            rws = slice(ch * GMLP_CHUNK, (ch + 1) * GMLP_CHUNK)
            sv = jnp.dot(wsm, vg[rws, cols], preferred_element_type=F32) + bst_ref[:, hd:hd + 1]
            ygmlp_ref[rws, cols] = (ug[rws, cols] * sv).astype(ygmlp_ref.dtype)


def _proj_mix(x2, g, w_in, layer, poolw, pscale, convw, convb, clg, clb, glg, glb, ws, bst, *, seq, tm=512):
    m, d = x2.shape
    n = OFF_GATE
    w = W_BRANCH
    assert seq % tm == 0 and tm % GMLP_CHUNK == 0 and tm % CONV_RB == 0
    row = lambda width: pl.BlockSpec((tm, width), lambda i: (i, 0))
    full = lambda a: pl.BlockSpec(a.shape, lambda i: (0,) * a.ndim)
    args = (poolw, pscale, convw, convb, clg, clb, glg, glb, ws, bst)
    return pl.pallas_call(
        functools.partial(_proj_mix_kernel, rows=128, tiles_per_seq=seq // tm),
        grid=(m // tm,),
        in_specs=[row(d), full(g),
                  pl.BlockSpec((None, d, n), lambda i: (layer, 0, 0), pipeline_mode=pl.Buffered(1))]
        + [full(a) for a in args],
        out_specs=[row(d), row(w), row(w), row(w), row(w)],
        out_shape=[jax.ShapeDtypeStruct((m, d), BF16), jax.ShapeDtypeStruct((m, w), F32)]
        + [jax.ShapeDtypeStruct((m, w), BF16)] * 3,
        scratch_shapes=[pltpu.VMEM((CONV_HALO + tm, w), F32), pltpu.VMEM((POOL_HALO + tm, w), F32),
                        pltpu.VMEM((7, CONV_HALO - 8 + tm, w), F32),
                        pltpu.VMEM((len(POOL_WINDOWS), POOL_HALO + tm, w), F32)],
        compiler_params=_cparams(("arbitrary",)),
        name="proj_mix",
    )(x2, g, w_in, *args)


PREP_GB = 8


def _ssm_prep_kernel(ar_ref, ai_ref, ldt_ref, cre_ref, cim_ref, bre_ref, bim_ref,
                     toep_ref, bpow_ref, cpow_ref, lama_ref, lamb_ref, cl_s, bl_s):
    n = SSM_STATE
    p = SSM_GROUP
    lanes = SSM_LANES
    lane = lax.broadcasted_iota(jnp.int32, (1, lanes), 1)
    for e in range(PREP_GB):
        ar = ar_ref[e]
        ai = ai_ref[e]
        dt = jnp.exp(ldt_ref[e])
        mag = jnp.exp(ar * dt)
        ang = ai * dt
        lr = mag * jnp.cos(ang)
        li = mag * jnp.sin(ang)
        den = ar * ar + ai * ai
        fr = ((lr - 1.0) * ar + li * ai) / den
        fi = (li * ar - (lr - 1.0) * ai) / den
        cre = cre_ref[e]
        cim = cim_ref[e]
        br = bre_ref[e]
        bi = bim_ref[e]
        bbr = fr * br - fi * bi
        bbi = fr * bi + fi * br
        pr = jnp.ones_like(lr)
        pi = jnp.zeros_like(lr)
        for k in range(SSM_T + 1):
            cl_s[k * p:(k + 1) * p, 0:n] = cre * pr - cim * pi
            cl_s[k * p:(k + 1) * p, n:2 * n] = -(cre * pi + cim * pr)
            if k < SSM_T:
                s = SSM_T - 1 - k
                bl_s[s * p:(s + 1) * p, 0:n] = pr * bbr - pi * bbi
                bl_s[s * p:(s + 1) * p, n:2 * n] = pr * bbi + pi * bbr
                pr, pi = pr * lr - pi * li, pr * li + pi * lr
        qr, qi = pr, pi
        for j in range(8):
            if j < SSM_SCAN_STEPS:
                lama_ref[e, j:j + 1, 0:n] = qr
                lama_ref[e, j:j + 1, n:2 * n] = qr
                lamb_ref[e, j:j + 1, 0:n] = -qi
                lamb_ref[e, j:j + 1, n:2 * n] = qi
                qr, qi = qr * qr - qi * qi, 2.0 * (qr * qi)
            else:
                lama_ref[e, j:j + 1, :] = jnp.zeros((1, 2 * n), F32)
                lamb_ref[e, j:j + 1, :] = jnp.zeros((1, 2 * n), F32)
        cpow_ref[e] = cl_s[p:(SSM_T + 1) * p, :].astype(cpow_ref.dtype)
        bpow_ref[e] = bl_s[...].astype(bpow_ref.dtype)
        bbcat = bl_s[(SSM_T - 1) * p:SSM_T * p, :]
        kt = lax.dot_general(bbcat, cl_s[0:SSM_T * p, :], (((1,), (1,)), ((), ())),
                             preferred_element_type=F32, precision=lax.Precision.HIGHEST)
        for s in range(SSM_T):
            blk = pltpu.roll(kt, s * p, axis=1) if s else kt
            toep_ref[e, s * p:(s + 1) * p, :] = jnp.where(lane >= s * p, blk, 0.0).astype(toep_ref.dtype)


def _ssm_prep(a_re, a_im, log_dt, b_re, b_im, c_re, c_im):
    n, p = SSM_STATE, SSM_GROUP
    gg = a_re.shape[0] * a_re.shape[1]
    gb = PREP_GB
    lanes = SSM_LANES
    row = lambda a: a.reshape(gg, 1, n)
    ldt = jnp.broadcast_to(log_dt.reshape(gg, 1), (gg, n))
    bt = lambda b: jnp.transpose(b.reshape(gg, n, p), (0, 2, 1))
    rspec = pl.BlockSpec((gb, 1, n), lambda g: (g, 0, 0))
    pspec = pl.BlockSpec((gb, p, n), lambda g: (g, 0, 0))
    ospec = lambda a, b: pl.BlockSpec((gb, a, b), lambda g: (g, 0, 0))
    return pl.pallas_call(
        _ssm_prep_kernel,
        grid=(gg // gb,),
        in_specs=[rspec, rspec, rspec, pspec, pspec, pspec, pspec],
        out_specs=[ospec(lanes, lanes), ospec(lanes, 2 * n), ospec(lanes, 2 * n), ospec(8, 2 * n), ospec(8, 2 * n)],
        out_shape=[
            jax.ShapeDtypeStruct((gg, lanes, lanes), BF16),
            jax.ShapeDtypeStruct((gg, lanes, 2 * n), BF16),
            jax.ShapeDtypeStruct((gg, lanes, 2 * n), BF16),
            jax.ShapeDtypeStruct((gg, 8, 2 * n), F32),
            jax.ShapeDtypeStruct((gg, 8, 2 * n), F32),
        ],
        scratch_shapes=[pltpu.VMEM(((SSM_T + 1) * p, 2 * n), F32), pltpu.VMEM((SSM_T * p, 2 * n), F32)],
        compiler_params=_cparams(("arbitrary",)),
        name="ssm_prep",
    )(row(a_re), row(a_im), row(ldt), c_re.reshape(gg, p, n), c_im.reshape(gg, p, n), bt(b_re), bt(b_im))


SSM_GB = 128 // SSM_GROUP
SSM_RB = 16


def _piece_transpose(xs, masks):
    n = len(xs)
    w = 128 // n
    rolled = []
    for k in range(n):
        wk = xs[k % n]
        for a in range(1, n):
            wk = jnp.where(masks[a], xs[(a + k) % n], wk)
        rolled.append(pltpu.roll(wk, w * k, axis=1) if k else wk)
    ys = []
    for a in range(n):
        ya = rolled[(0 - a) % n]
        for b in range(1, n):
            ya = jnp.where(masks[b], rolled[(b - a) % n], ya)
        ys.append(ya)
    return ys


def _ssm_kernel(*refs, nchunk, pad, n_side):
    u_ref, toep_ref, bpow_ref, cpow_ref, lama_ref, lamb_ref = refs[:6]
    side_in = refs[6:6 + n_side]
    y_ref = refs[6 + n_side]
    side_out = refs[7 + n_side:7 + 2 * n_side]
    up_ref, yp_ref, xs_ref = refs[7 + 2 * n_side:]
    _side_cast(side_in, side_out)
    rows = up_ref.shape[1]
    n2 = 2 * SSM_STATE
    t = SSM_T
    lane = lax.broadcasted_iota(jnp.int32, (1, 128), 1)
    masks = [lane // SSM_GROUP == a for a in range(SSM_GB)]
    nhalf = t // SSM_GB

    def to_chunks(rb, carry):
        r = pl.multiple_of(rb * SSM_RB, SSM_RB)
        for hf in range(nhalf):
            xs = [u_ref[pl.ds(r * t + hf * SSM_GB + b, SSM_RB, stride=t), :] for b in range(SSM_GB)]
            ys = _piece_transpose(xs, masks)
            for a in range(SSM_GB):
                up_ref[a, pl.ds(r, SSM_RB), hf * 128:(hf + 1) * 128] = ys[a].astype(up_ref.dtype)
        return carry
    lax.fori_loop(0, rows // SSM_RB, to_chunks, 0, unroll=2)

    cidx = lax.broadcasted_iota(jnp.int32, (rows, 1), 0) % nchunk
    xs_ref[0:pad, :] = jnp.zeros((pad, n2), F32)
    for a in range(SSM_GB):
        u = up_ref[a]
        xs_ref[pad:pad + rows, :] = jnp.dot(u, bpow_ref[a], preferred_element_type=F32)
        for j in range(SSM_SCAN_STEPS):
            sh = 1 << j
            cur = xs_ref[pad:pad + rows, :]
            prev = jnp.where(cidx >= sh, xs_ref[pad - sh:pad - sh + rows, :], 0.0)
            la = lama_ref[a, j:j + 1, :]
            lb = lamb_ref[a, j:j + 1, :]
            xs_ref[pad:pad + rows, :] = cur + prev * la + pltpu.roll(prev, SSM_STATE, axis=1) * lb
        xprev = jnp.where(cidx >= 1, xs_ref[pad - 1:pad - 1 + rows, :], 0.0).astype(BF16)
        yp_ref[a] = (jnp.dot(u, toep_ref[a], preferred_element_type=F32)
                     + lax.dot_general(xprev, cpow_ref[a], (((1,), (1,)), ((), ())),
                                       preferred_element_type=F32))

    def to_tokens(rb, carry):
        r = pl.multiple_of(rb * 8, 8)
        for hf in range(nhalf):
            ys = [yp_ref[a, pl.ds(r, 8), hf * 128:(hf + 1) * 128] for a in range(SSM_GB)]
            xs = _piece_transpose(ys, masks)
            for b in range(SSM_GB):
                y_ref[pl.ds(r * t + hf * SSM_GB + b, 8, stride=t), :] = xs[b]
        return carry
    lax.fori_loop(0, rows // 8, to_tokens, 0, unroll=2)


def _ssm_scan(pm, toep, bpow, cpow, lama, lamb, layer, side_weights, *, nchunk):
    m = pm.shape[0]
    rows = m // SSM_T
    lanes = SSM_LANES
    n2 = 2 * SSM_STATE
    pad = 1 << (SSM_SCAN_STEPS - 1)
    nblk = W_BRANCH // 128
    gspec = lambda a, b: pl.BlockSpec((SSM_GB, a, b), lambda i: (layer * nblk + i, 0, 0))
    s_in, s_out, s_shape = _side_cast_specs(side_weights, layer, nblk, lambda i: i)
    outs = pl.pallas_call(
        functools.partial(_ssm_kernel, nchunk=nchunk, pad=pad, n_side=len(side_weights)),
        grid=(nblk,),
        in_specs=[pl.BlockSpec((m, 128), lambda i: (0, i)),
                  gspec(lanes, lanes), gspec(lanes, n2), gspec(lanes, n2), gspec(8, n2), gspec(8, n2)] + s_in,
        out_specs=[pl.BlockSpec((m, 128), lambda i: (0, i))] + s_out,
        out_shape=[jax.ShapeDtypeStruct((m, W_BRANCH), F32)] + s_shape,
        scratch_shapes=[pltpu.VMEM((SSM_GB, rows, lanes), BF16), pltpu.VMEM((SSM_GB, rows, lanes), F32),
                        pltpu.VMEM((pad + rows, n2), F32)],
        compiler_params=_cparams(("arbitrary",)),
        name="ssm_scan",
    )(pm, toep, bpow, cpow, lama, lamb, *side_weights)
    return outs[0], outs[1:]


def _ssm_post_kernel(yraw_ref, u_ref, d_ref, wglu_ref, o_ref):
    w = W_BRANCH
    y = yraw_ref[...] + d_ref[...] * u_ref[...]
    zg = jnp.dot(_gelu(y).astype(BF16), wglu_ref[...], preferred_element_type=F32)
    o_ref[...] = (zg[:, :w] * _sigmoid(zg[:, w:])).astype(o_ref.dtype)


def _ssm_post(yraw, u, d, wglu, *, tm=512):
    m, w = yraw.shape
    row = pl.BlockSpec((tm, w), lambda i: (i, 0))
    return pl.pallas_call(
        _ssm_post_kernel,
        grid=(m // tm,),
        in_specs=[row, row, pl.BlockSpec((1, w), lambda i: (0, 0)), pl.BlockSpec(wglu.shape, lambda i: (0, 0))],
        out_specs=row,
        out_shape=jax.ShapeDtypeStruct((m, w), BF16),
        compiler_params=_cparams(("parallel",)),
        name="ssm_post",
    )(yraw, u, d, wglu)


def _merge_kernel(*refs, n_side):
    nb = N_BRANCH
    h_ref = refs[0]
    ys = refs[1:1 + nb]
    wg = refs[1 + nb:1 + 2 * nb]
    wb = refs[1 + 2 * nb:1 + 3 * nb]
    side_in = refs[1 + 3 * nb:1 + 3 * nb + n_side]
    o_ref = refs[1 + 3 * nb + n_side]
    side_out = refs[2 + 3 * nb + n_side:2 + 3 * nb + 2 * n_side]
    h = h_ref[...]
    acc = None
    for k in range(nb):
        gate = jnp.dot(h, wg[k][...], preferred_element_type=F32)
        yb = jnp.dot(ys[k][...], wb[k][0], preferred_element_type=F32)
        term = yb * _sigmoid(gate)
        acc = term if acc is None else acc + term
    o_ref[...] = acc.astype(o_ref.dtype)
    _side_cast(side_in, side_out)


def _merge(h, ys, wgate, wbranch, side_weights, side_layer, *, tm=1024, tn=256):
    m, d = h.shape
    w = W_BRANCH
    nj = d // tn
    j0 = OFF_GATE // tn
    gate_spec = lambda k: pl.BlockSpec((None, d, tn), lambda i, j: (0, 0, j0 + k * nj + j))
    br_spec = lambda k: pl.BlockSpec((1, w, tn), lambda i, j: (k, 0, j))
    s_in, s_out, s_shape = _side_cast_specs(side_weights, side_layer, (m // tm) * nj, lambda i, j: i * nj + j)
    outs = pl.pallas_call(
        functools.partial(_merge_kernel, n_side=len(side_weights)),
        grid=(m // tm, nj),
        in_specs=[pl.BlockSpec((tm, d), lambda i, j: (i, 0))]
        + [pl.BlockSpec((tm, w), lambda i, j: (i, 0))] * N_BRANCH
        + [gate_spec(k) for k in range(N_BRANCH)]
        + [br_spec(k) for k in range(N_BRANCH)]
        + s_in,
        out_specs=[pl.BlockSpec((tm, tn), lambda i, j: (i, j))] + s_out,
        out_shape=[jax.ShapeDtypeStruct((m, d), BF16)] + s_shape,
        compiler_params=_cparams(("arbitrary", "arbitrary")),
        name="gated_merge",
    )(h, *ys, *([wgate] * N_BRANCH), *([wbranch] * N_BRANCH), *side_weights)
    return outs[0], outs[1:]


def _out_kernel(m_ref, w_ref, x_ref, g_ref, gn_ref, o_ref, hn_ref):
    mix = jnp.dot(m_ref[...], w_ref[...], preferred_element_type=F32)
    x1 = x_ref[...] + _rms(mix, g_ref[...])
    o_ref[...] = x1
    hn_ref[...] = _rms(x1, gn_ref[...]).astype(hn_ref.dtype)


def _out_proj(merged, w_o, x2, g, g_next, layer, *, tm=512):
    m, d = x2.shape
    row = pl.BlockSpec((tm, d), lambda i: (i, 0))
    vec = pl.BlockSpec((1, d), lambda i: (0, 0))
    return pl.pallas_call(
        _out_kernel,
        grid=(m // tm,),
        in_specs=[row, pl.BlockSpec((None, d, d), lambda i: (layer, 0, 0), pipeline_mode=pl.Buffered(1)),
                  row, vec, vec],
        out_specs=[row, row],
        out_shape=[jax.ShapeDtypeStruct((m, d), F32), jax.ShapeDtypeStruct((m, d), BF16)],
        compiler_params=_cparams(("parallel",)),
        name="out_proj",
    )(merged, w_o, x2, g, g_next)


def _ffn_kernel(*refs, rows, n_side):
    x_ref, h_ref, w1_ref, w2_ref, gpost_ref = refs[:5]
    side_in = refs[5:5 + n_side]
    o_ref = refs[5 + n_side]
    side_out = refs[6 + n_side:6 + 2 * n_side]
    j = pl.program_id(1)

    @pl.when(j == 0)
    def _():
        o_ref[...] = jnp.zeros(o_ref.shape, o_ref.dtype)

    a = jnp.dot(h_ref[...], w1_ref[...], preferred_element_type=F32)
    a = jnp.maximum(a, 0.0)
    a = (a * a).astype(BF16)
    nw = 512
    for c in range(o_ref.shape[1] // nw):
        o_ref[:, c * nw:(c + 1) * nw] += jnp.dot(a, w2_ref[:, c * nw:(c + 1) * nw],
                                                 preferred_element_type=F32)
    _side_cast(side_in, side_out)

    @pl.when(j == pl.num_programs(1) - 1)
    def _():
        def body(i, carry):
            r = pl.multiple_of(i * rows, rows)
            o_ref[pl.ds(r, rows), :] = (x_ref[pl.ds(r, rows), :]
                                        + _rms(o_ref[pl.ds(r, rows), :], gpost_ref[...]))
            return carry
        lax.fori_loop(0, x_ref.shape[0] // rows, body, 0, unroll=2)


def _ffn(x2, h2, w1, w2, gpost, side_weights, side_layer, *, tm=1024, tf=512):
    m, d = x2.shape
    f = w1.shape[2]
    nj = f // tf
    s_in, s_out, s_shape = _side_cast_specs(side_weights, side_layer, (m // tm) * nj, lambda i, j: i * nj + j)
    outs = pl.pallas_call(
        functools.partial(_ffn_kernel, rows=128, n_side=len(side_weights)),
        grid=(m // tm, nj),
        in_specs=[
            pl.BlockSpec((tm, d), lambda i, j: (i, 0)),
            pl.BlockSpec((tm, d), lambda i, j: (i, 0)),
            pl.BlockSpec((None, d, tf), lambda i, j: (0, 0, j)),
            pl.BlockSpec((None, tf, d), lambda i, j: (0, j, 0)),
            pl.BlockSpec((1, d), lambda i, j: (0, 0)),
        ] + s_in,
        out_specs=[pl.BlockSpec((tm, d), lambda i, j: (i, 0))] + s_out,
        out_shape=[jax.ShapeDtypeStruct((m, d), F32)] + s_shape,
        compiler_params=_cparams(("arbitrary", "arbitrary")),
        name="ffn",
    )(x2, h2, w1, w2, gpost, *side_weights)
    return outs[0], outs[1:]


def kernel(x, g_pre_mix, w_in, ssm_a_re, ssm_a_im, ssm_log_dt, ssm_b_re, ssm_b_im, ssm_c_re, ssm_c_im,
           ssm_d, ssm_w_glu, pool_w, pool_scale, conv_w, conv_b, conv_ln_g, conv_ln_b, gmlp_ln_g,
           gmlp_ln_b, gmlp_ws, gmlp_bs, w_branch, w_o, g_post_mix, g_pre_mlp, w_ff1, w_ff2, g_post_mlp):
    nb, seq, d = x.shape
    m = nb * seq
    depth = w_in.shape[0]
    nchunk = seq // SSM_T
    assert d == D_MODEL and nchunk == 1 << SSM_SCAN_STEPS

    toep, bpow, cpow, lama, lamb = _ssm_prep(ssm_a_re, ssm_a_im, ssm_log_dt, ssm_b_re, ssm_b_im,
                                             ssm_c_re, ssm_c_im)

    w_br2 = w_branch.reshape(depth, N_BRANCH * W_BRANCH, d)
    w_in_b = _cast_bf16(w_in, 0, 256)

    row = lambda a: a.reshape(1, -1)
    x2 = x.reshape(m, d)
    for l in range(depth):
        h, u, y_pool, y_conv, y_gmlp = _proj_mix(
            x2, row(g_pre_mix[l]), w_in_b, 0, pool_w[l].astype(BF16), row(pool_scale[l]),
            jnp.broadcast_to(conv_w[l][:, None, :], (CONV_WIDTH, 8, W_BRANCH)),
            row(conv_b[l]), row(conv_ln_g[l]), row(conv_ln_b[l]), row(gmlp_ln_g[l]), row(gmlp_ln_b[l]),
            gmlp_ws[l], jnp.transpose(gmlp_bs[l]), seq=seq)
        yraw, (w_br_b, w_o_b) = _ssm_scan(u, toep, bpow, cpow, lama, lamb, l, (w_br2, w_o), nchunk=nchunk)
        y_ssm = _ssm_post(yraw, u, row(ssm_d[l]), ssm_w_glu[l].astype(BF16))
        merged, (w_ff1_b, w_ff2_b) = _merge(h, (y_ssm, y_pool, y_conv, y_gmlp), w_in_b,
                                            w_br_b.reshape(N_BRANCH, W_BRANCH, d), (w_ff1, w_ff2), l)
        x2, h2 = _out_proj(merged, w_o_b, x2, row(g_post_mix[l]), row(g_pre_mlp[l]), 0)
        nxt = (w_in,) if l + 1 < depth else ()
        x2, nxt_b = _ffn(x2, h2, w_ff1_b, w_ff2_b, row(g_post_mlp[l]), nxt, l + 1)
        if nxt:
            (w_in_b,) = nxt_b
    return x2.reshape(nb, seq, d)
```

```python
import functools

import jax
import jax.numpy as jnp
from jax import lax
from jax.experimental import pallas as pl
from jax.experimental.pallas import tpu as pltpu

D_MODEL = 2048
DEPTH = 2
N_BRANCH = 4
W_BRANCH = D_MODEL // 4
SSM_GROUP = 16
SSM_GROUPS = W_BRANCH // SSM_GROUP
SSM_STATE = 64
POOL_WINDOWS = (2, 4, 8, 16)
POOL_GW = W_BRANCH // len(POOL_WINDOWS)
CONV_WIDTH = 31
GMLP_CHUNK = 128
GMLP_HEADS = 4
GMLP_HD = W_BRANCH // GMLP_HEADS
D_FF = 4 * D_MODEL
EPS = 1e-6

OFF_SSM = 0
OFF_POOL = OFF_SSM + W_BRANCH
OFF_CONV = OFF_POOL + W_BRANCH
OFF_GMLP = OFF_CONV + 2 * W_BRANCH
OFF_GATE = OFF_GMLP + 2 * W_BRANCH

SSM_T = 16
SSM_LANES = SSM_T * SSM_GROUP
SSM_SCAN_STEPS = 7
CONV_HALO = 32
POOL_HALO = 8 * len(POOL_WINDOWS)
assert all(win == 2 << k for k, win in enumerate(POOL_WINDOWS))
CONV_RB = 64

VMEM_LIMIT = 56 * 1024 * 1024
BF16 = jnp.bfloat16
F32 = jnp.float32


def _cparams(sem):
    return pltpu.CompilerParams(dimension_semantics=sem, vmem_limit_bytes=VMEM_LIMIT)


def _sigmoid(x):
    return 0.5 * (1.0 + jnp.tanh(0.5 * x))


def _gelu(x):
    c = 0.7978845608028654
    return 0.5 * x * (1.0 + jnp.tanh(c * (x + 0.044715 * (x * x * x))))


def _rms(x, g):
    ms = jnp.mean(x * x, axis=-1, keepdims=True)
    return x * lax.rsqrt(ms + EPS) * g


def _layer_norm(x, g, b):
    mu = jnp.mean(x, axis=-1, keepdims=True)
    xc = x - mu
    var = jnp.mean(xc * xc, axis=-1, keepdims=True)
    return xc * lax.rsqrt(var + EPS) * g + b


def _cast_kernel(w_ref, o_ref):
    o_ref[...] = w_ref[...].astype(o_ref.dtype)


def _cast_bf16(w, layer, rows):
    _, r, c = w.shape
    return pl.pallas_call(
        _cast_kernel,
        grid=(r // rows,),
        in_specs=[pl.BlockSpec((1, rows, c), lambda i: (layer, i, 0))],
        out_specs=pl.BlockSpec((1, rows, c), lambda i: (0, i, 0)),
        out_shape=jax.ShapeDtypeStruct((1, r, c), BF16),
        compiler_params=_cparams(("parallel",)),
        name="cast_bf16",
    )(w)


def _side_cast_specs(weights, layer, nsteps, step_of):
    in_specs, out_specs, out_shapes = [], [], []
    for w in weights:
        _, r, c = w.shape
        assert r % nsteps == 0 and (r // nsteps) % 16 == 0
        rows = r // nsteps
        in_specs.append(pl.BlockSpec((None, rows, c), lambda *ids: (layer, step_of(*ids), 0)))
        out_specs.append(pl.BlockSpec((None, rows, c), lambda *ids: (0, step_of(*ids), 0)))
        out_shapes.append(jax.ShapeDtypeStruct((1, r, c), BF16))
    return in_specs, out_specs, out_shapes


def _side_cast(in_refs, out_refs):
    for wi, wo in zip(in_refs, out_refs):
        wo[...] = wi[...].astype(wo.dtype)


def _proj_mix_kernel(x_ref, g_ref, w_ref, poolw_ref, pscale_ref, convw_ref, convb_ref, clg_ref, clb_ref,
                     glg_ref, glb_ref, ws_ref, bst_ref,
                     h_ref, u_ref, ypool_ref, yconv_ref, ygmlp_ref, vbuf, pbuf, sbuf, qbuf, ugbuf, vgbuf,
                     *, rows, tiles_per_seq):
    c = pl.program_id(0) % tiles_per_seq
    tc = x_ref.shape[0]
    w = W_BRANCH

    @pl.when(c == 0)
    def _():
        vbuf[0:CONV_HALO, :] = jnp.zeros((CONV_HALO, w), F32)
        pbuf[0:POOL_HALO, :] = jnp.zeros((POOL_HALO, w), F32)

    def body(i, carry):
        r = pl.multiple_of(i * rows, rows)
        h_ref[pl.ds(r, rows), :] = _rms(x_ref[pl.ds(r, rows), :], g_ref[...]).astype(h_ref.dtype)
        return carry
    lax.fori_loop(0, tc // rows, body, 0, unroll=2)

    h = h_ref[...]
    proj = lambda off, width: jnp.dot(h, w_ref[:, off:off + width], preferred_element_type=F32)
    gm = proj(OFF_GMLP, 2 * w)
    ugbuf[...] = _gelu(gm[:, :w])
    vgbuf[...] = _layer_norm(_gelu(gm[:, w:]), glg_ref[...], glb_ref[...]).astype(vgbuf.dtype)
    cv = proj(OFF_CONV, 2 * w)
    vbuf[CONV_HALO:CONV_HALO + tc, :] = cv[:, :w] * _sigmoid(cv[:, w:])
    pbuf[POOL_HALO:POOL_HALO + tc, :] = proj(OFF_POOL, w)
    u_ref[...] = proj(OFF_SSM, w)

    src = pbuf
    for k in range(len(POOL_WINDOWS)):
        lo, sh, c0 = 8 * (k + 1), 1 << k, k * POOL_GW
        n = POOL_HALO + tc - lo
        a_hi = src[lo:lo + n, c0:] if k == 0 else src[k - 1, lo:lo + n, c0:]
        a_lo = src[lo - sh:lo - sh + n, c0:] if k == 0 else src[k - 1, lo - sh:lo - sh + n, c0:]
        qbuf[k, lo:lo + n, c0:] = a_hi + a_lo
        src = qbuf
    t = c * tc + lax.broadcasted_iota(jnp.int32, (tc, 1), 0)
    for gi, win in enumerate(POOL_WINDOWS):
        cols = slice(gi * POOL_GW, (gi + 1) * POOL_GW)
        tok = pbuf[POOL_HALO:POOL_HALO + tc, cols]
        s = qbuf[gi, POOL_HALO:POOL_HALO + tc, cols]
        count = jnp.minimum(t + 1, win).astype(F32)
        pooled = s / count - tok
        mixed = jnp.dot(pooled.astype(BF16), poolw_ref[gi], preferred_element_type=F32)
        ypool_ref[:, cols] = (mixed * pscale_ref[:, cols]).astype(ypool_ref.dtype)
    pbuf[0:POOL_HALO, :] = pbuf[tc:tc + POOL_HALO, :]

    span = CONV_HALO - 8
    for b in range(1, 8):
        sbuf[b - 1] = vbuf[8 - b:8 - b + span + tc, :]
    for rb in range(tc // CONV_RB):
        r0 = rb * CONV_RB
        acc = jnp.broadcast_to(convb_ref[...], (CONV_RB, w))
        for b in range(8):
            for a in range((CONV_WIDTH - 1 - b) // 8 + 1):
                k = CONV_WIDTH - 1 - (8 * a + b)
                lo = span - 8 * a + r0
                tap = sbuf[b - 1, lo:lo + CONV_RB, :] if b else vbuf[8 + lo:8 + lo + CONV_RB, :]
                wk = convw_ref[k]
                acc = acc + (tap.reshape(CONV_RB // 8, 8, w) * wk[None]).reshape(CONV_RB, w)
        yn = _layer_norm(acc, clg_ref[...], clb_ref[...])
        yconv_ref[r0:r0 + CONV_RB, :] = (yn * _sigmoid(yn)).astype(yconv_ref.dtype)
    vbuf[0:CONV_HALO, :] = vbuf[tc:tc + CONV_HALO, :]

    ri = lax.broadcasted_iota(jnp.int32, (GMLP_CHUNK, GMLP_CHUNK), 0)
    ci = lax.broadcasted_iota(jnp.int32, (GMLP_CHUNK, GMLP_CHUNK), 1)
    causal = ri >= ci
    for hd in range(GMLP_HEADS):
        wsm = jnp.where(causal, ws_ref[hd], 0.0).astype(BF16)
        cols = slice(hd * GMLP_HD, (hd + 1) * GMLP_HD)
        for ch in range(tc // GMLP_CHUNK):
            rws = slice(ch * GMLP_CHUNK, (ch + 1) * GMLP_CHUNK)
            sv = jnp.dot(wsm, vgbuf[rws, cols], preferred_element_type=F32) + bst_ref[:, hd:hd + 1]
            ygmlp_ref[rws, cols] = (ugbuf[rws, cols] * sv).astype(ygmlp_ref.dtype)


def _proj_mix(x2, g, w_in, layer, poolw, pscale, convw, convb, clg, clb, glg, glb, ws, bst, *, seq, tm=512):
    m, d = x2.shape
    n = OFF_GATE
    w = W_BRANCH
    assert seq % tm == 0 and tm % GMLP_CHUNK == 0 and tm % CONV_RB == 0
    row = lambda width: pl.BlockSpec((tm, width), lambda i: (i, 0))
    full = lambda a: pl.BlockSpec(a.shape, lambda i: (0,) * a.ndim)
    args = (poolw, pscale, convw, convb, clg, clb, glg, glb, ws, bst)
    return pl.pallas_call(
        functools.partial(_proj_mix_kernel, rows=128, tiles_per_seq=seq // tm),
        grid=(m // tm,),
        in_specs=[row(d), full(g),
                  pl.BlockSpec((None, d, n), lambda i: (layer, 0, 0), pipeline_mode=pl.Buffered(1))]
        + [full(a) for a in args],
        out_specs=[row(d), row(w), row(w), row(w), row(w)],
        out_shape=[jax.ShapeDtypeStruct((m, d), BF16), jax.ShapeDtypeStruct((m, w), F32)]
        + [jax.ShapeDtypeStruct((m, w), BF16)] * 3,
        scratch_shapes=[pltpu.VMEM((CONV_HALO + tm, w), F32), pltpu.VMEM((POOL_HALO + tm, w), F32),
                        pltpu.VMEM((7, CONV_HALO - 8 + tm, w), F32),
                        pltpu.VMEM((len(POOL_WINDOWS), POOL_HALO + tm, w), F32),
                        pltpu.VMEM((tm, w), F32), pltpu.VMEM((tm, w), BF16)],
        compiler_params=_cparams(("arbitrary",)),
        name="proj_mix",
    )(x2, g, w_in, *args)


PREP_GB = 8


def _ssm_prep_kernel(ar_ref, ai_ref, ldt_ref, cre_ref, cim_ref, bre_ref, bim_ref,
                     toep_ref, bpow_ref, cpow_ref, lama_ref, lamb_ref, cl_s, bl_s):
    n = SSM_STATE
    p = SSM_GROUP
    lanes = SSM_LANES
    lane = lax.broadcasted_iota(jnp.int32, (1, lanes), 1)
    for e in range(PREP_GB):
        ar = ar_ref[e]
        ai = ai_ref[e]
        dt = jnp.exp(ldt_ref[e])
        mag = jnp.exp(ar * dt)
        ang = ai * dt
        lr = mag * jnp.cos(ang)
        li = mag * jnp.sin(ang)
        den = ar * ar + ai * ai
        fr = ((lr - 1.0) * ar + li * ai) / den
        fi = (li * ar - (lr - 1.0) * ai) / den
        cre = cre_ref[e]
        cim = cim_ref[e]
        br = bre_ref[e]
        bi = bim_ref[e]
        bbr = fr * br - fi * bi
        bbi = fr * bi + fi * br
        pr = jnp.ones_like(lr)
        pi = jnp.zeros_like(lr)
        for k in range(SSM_T + 1):
            cl_s[k * p:(k + 1) * p, 0:n] = cre * pr - cim * pi
            cl_s[k * p:(k + 1) * p, n:2 * n] = -(cre * pi + cim * pr)
            if k < SSM_T:
                s = SSM_T - 1 - k
                bl_s[s * p:(s + 1) * p, 0:n] = pr * bbr - pi * bbi
                bl_s[s * p:(s + 1) * p, n:2 * n] = pr * bbi + pi * bbr
                pr, pi = pr * lr - pi * li, pr * li + pi * lr
        qr, qi = pr, pi
        for j in range(8):
            if j < SSM_SCAN_STEPS:
                lama_ref[e, j:j + 1, 0:n] = qr
                lama_ref[e, j:j + 1, n:2 * n] = qr
                lamb_ref[e, j:j + 1, 0:n] = -qi
                lamb_ref[e, j:j + 1, n:2 * n] = qi
                qr, qi = qr * qr - qi * qi, 2.0 * (qr * qi)
            else:
                lama_ref[e, j:j + 1, :] = jnp.zeros((1, 2 * n), F32)
                lamb_ref[e, j:j + 1, :] = jnp.zeros((1, 2 * n), F32)
        cpow_ref[e] = cl_s[p:(SSM_T + 1) * p, :].astype(cpow_ref.dtype)
        bpow_ref[e] = bl_s[...].astype(bpow_ref.dtype)
        bbcat = bl_s[(SSM_T - 1) * p:SSM_T * p, :]
        kt = lax.dot_general(bbcat, cl_s[0:SSM_T * p, :], (((1,), (1,)), ((), ())),
                             preferred_element_type=F32, precision=lax.Precision.HIGHEST)
        for s in range(SSM_T):
            blk = pltpu.roll(kt, s * p, axis=1) if s else kt
            toep_ref[e, s * p:(s + 1) * p, :] = jnp.where(lane >= s * p, blk, 0.0).astype(toep_ref.dtype)


def _ssm_prep(a_re, a_im, log_dt, b_re, b_im, c_re, c_im):
    n, p = SSM_STATE, SSM_GROUP
    gg = a_re.shape[0] * a_re.shape[1]
    gb = PREP_GB
    lanes = SSM_LANES
    row = lambda a: a.reshape(gg, 1, n)
    ldt = jnp.broadcast_to(log_dt.reshape(gg, 1), (gg, n))
    bt = lambda b: jnp.transpose(b.reshape(gg, n, p), (0, 2, 1))
    rspec = pl.BlockSpec((gb, 1, n), lambda g: (g, 0, 0))
    pspec = pl.BlockSpec((gb, p, n), lambda g: (g, 0, 0))
    ospec = lambda a, b: pl.BlockSpec((gb, a, b), lambda g: (g, 0, 0))
    return pl.pallas_call(
        _ssm_prep_kernel,
        grid=(gg // gb,),
        in_specs=[rspec, rspec, rspec, pspec, pspec, pspec, pspec],
        out_specs=[ospec(lanes, lanes), ospec(lanes, 2 * n), ospec(lanes, 2 * n), ospec(8, 2 * n), ospec(8, 2 * n)],
        out_shape=[
            jax.ShapeDtypeStruct((gg, lanes, lanes), BF16),
            jax.ShapeDtypeStruct((gg, lanes, 2 * n), BF16),
            jax.ShapeDtypeStruct((gg, lanes, 2 * n), BF16),
            jax.ShapeDtypeStruct((gg, 8, 2 * n), F32),
            jax.ShapeDtypeStruct((gg, 8, 2 * n), F32),
        ],
        scratch_shapes=[pltpu.VMEM(((SSM_T + 1) * p, 2 * n), F32), pltpu.VMEM((SSM_T * p, 2 * n), F32)],
        compiler_params=_cparams(("arbitrary",)),
        name="ssm_prep",
    )(row(a_re), row(a_im), row(ldt), c_re.reshape(gg, p, n), c_im.reshape(gg, p, n), bt(b_re), bt(b_im))


SSM_GB = 128 // SSM_GROUP
SSM_RB = 16


def _piece_transpose(xs, masks):
    n = len(xs)
    w = 128 // n
    rolled = []
    for k in range(n):
        wk = xs[k % n]
        for a in range(1, n):
            wk = jnp.where(masks[a], xs[(a + k) % n], wk)
        rolled.append(pltpu.roll(wk, w * k, axis=1) if k else wk)
    ys = []
    for a in range(n):
        ya = rolled[(0 - a) % n]
        for b in range(1, n):
            ya = jnp.where(masks[b], rolled[(b - a) % n], ya)
        ys.append(ya)
    return ys


def _ssm_kernel(*refs, nchunk, pad, n_side):
    u_ref, toep_ref, bpow_ref, cpow_ref, lama_ref, lamb_ref = refs[:6]
    side_in = refs[6:6 + n_side]
    y_ref = refs[6 + n_side]
    side_out = refs[7 + n_side:7 + 2 * n_side]
    up_ref, yp_ref, xs_ref = refs[7 + 2 * n_side:]
    _side_cast(side_in, side_out)
    rows = up_ref.shape[1]
    n2 = 2 * SSM_STATE
    t = SSM_T
    lane = lax.broadcasted_iota(jnp.int32, (1, 128), 1)
    masks = [lane // SSM_GROUP == a for a in range(SSM_GB)]
    nhalf = t // SSM_GB

    def to_chunks(rb, carry):
        r = pl.multiple_of(rb * SSM_RB, SSM_RB)
        for hf in range(nhalf):
            xs = [u_ref[pl.ds(r * t + hf * SSM_GB + b, SSM_RB, stride=t), :] for b in range(SSM_GB)]
            ys = _piece_transpose(xs, masks)
            for a in range(SSM_GB):
                up_ref[a, pl.ds(r, SSM_RB), hf * 128:(hf + 1) * 128] = ys[a].astype(up_ref.dtype)
        return carry
    lax.fori_loop(0, rows // SSM_RB, to_chunks, 0, unroll=2)

    cidx = lax.broadcasted_iota(jnp.int32, (rows, 1), 0) % nchunk
    xs_ref[0:pad, :] = jnp.zeros((pad, n2), F32)
    for a in range(SSM_GB):
        u = up_ref[a]
        xs_ref[pad:pad + rows, :] = jnp.dot(u, bpow_ref[a], preferred_element_type=F32)
        for j in range(SSM_SCAN_STEPS):
            sh = 1 << j
            cur = xs_ref[pad:pad + rows, :]
            prev = jnp.where(cidx >= sh, xs_ref[pad - sh:pad - sh + rows, :], 0.0)
            la = lama_ref[a, j:j + 1, :]
            lb = lamb_ref[a, j:j + 1, :]
            xs_ref[pad:pad + rows, :] = cur + prev * la + pltpu.roll(prev, SSM_STATE, axis=1) * lb
        xprev = jnp.where(cidx >= 1, xs_ref[pad - 1:pad - 1 + rows, :], 0.0).astype(BF16)
        yp_ref[a] = (jnp.dot(u, toep_ref[a], preferred_element_type=F32)
                     + lax.dot_general(xprev, cpow_ref[a], (((1,), (1,)), ((), ())),
                                       preferred_element_type=F32))

    def to_tokens(rb, carry):
        r = pl.multiple_of(rb * 8, 8)
        for hf in range(nhalf):
            ys = [yp_ref[a, pl.ds(r, 8), hf * 128:(hf + 1) * 128] for a in range(SSM_GB)]
            xs = _piece_transpose(ys, masks)
            for b in range(SSM_GB):
                y_ref[pl.ds(r * t + hf * SSM_GB + b, 8, stride=t), :] = xs[b]
        return carry
    lax.fori_loop(0, rows // 8, to_tokens, 0, unroll=2)


def _ssm_scan(pm, toep, bpow, cpow, lama, lamb, layer, side_weights, *, nchunk):
    m = pm.shape[0]
    rows = m // SSM_T
    lanes = SSM_LANES
    n2 = 2 * SSM_STATE
    pad = 1 << (SSM_SCAN_STEPS - 1)
    nblk = W_BRANCH // 128
    gspec = lambda a, b: pl.BlockSpec((SSM_GB, a, b), lambda i: (layer * nblk + i, 0, 0))
    s_in, s_out, s_shape = _side_cast_specs(side_weights, layer, nblk, lambda i: i)
    outs = pl.pallas_call(
        functools.partial(_ssm_kernel, nchunk=nchunk, pad=pad, n_side=len(side_weights)),
        grid=(nblk,),
        in_specs=[pl.BlockSpec((m, 128), lambda i: (0, i)),
                  gspec(lanes, lanes), gspec(lanes, n2), gspec(lanes, n2), gspec(8, n2), gspec(8, n2)] + s_in,
        out_specs=[pl.BlockSpec((m, 128), lambda i: (0, i))] + s_out,
        out_shape=[jax.ShapeDtypeStruct((m, W_BRANCH), F32)] + s_shape,
        scratch_shapes=[pltpu.VMEM((SSM_GB, rows, lanes), BF16), pltpu.VMEM((SSM_GB, rows, lanes), F32),
                        pltpu.VMEM((pad + rows, n2), F32)],
        compiler_params=_cparams(("arbitrary",)),
        name="ssm_scan",
    )(pm, toep, bpow, cpow, lama, lamb, *side_weights)
    return outs[0], outs[1:]


def _ssm_post_kernel(yraw_ref, u_ref, d_ref, wglu_ref, o_ref):
    w = W_BRANCH
    y = yraw_ref[...] + d_ref[...] * u_ref[...]
    zg = jnp.dot(_gelu(y).astype(BF16), wglu_ref[...], preferred_element_type=F32)
    o_ref[...] = (zg[:, :w] * _sigmoid(zg[:, w:])).astype(o_ref.dtype)


def _ssm_post(yraw, u, d, wglu, *, tm=512):
    m, w = yraw.shape
    row = pl.BlockSpec((tm, w), lambda i: (i, 0))
    return pl.pallas_call(
        _ssm_post_kernel,
        grid=(m // tm,),
        in_specs=[row, row, pl.BlockSpec((1, w), lambda i: (0, 0)), pl.BlockSpec(wglu.shape, lambda i: (0, 0))],
        out_specs=row,
        out_shape=jax.ShapeDtypeStruct((m, w), BF16),
        compiler_params=_cparams(("parallel",)),
        name="ssm_post",
    )(yraw, u, d, wglu)


def _merge_kernel(*refs, n_side):
    nb = N_BRANCH
    h_ref = refs[0]
    ys = refs[1:1 + nb]
    wg = refs[1 + nb:1 + 2 * nb]
    wb = refs[1 + 2 * nb:1 + 3 * nb]
    side_in = refs[1 + 3 * nb:1 + 3 * nb + n_side]
    o_ref = refs[1 + 3 * nb + n_side]
    side_out = refs[2 + 3 * nb + n_side:2 + 3 * nb + 2 * n_side]
    h = h_ref[...]
    acc = None
    for k in range(nb):
        gate = jnp.dot(h, wg[k][...], preferred_element_type=F32)
        yb = jnp.dot(ys[k][...], wb[k][0], preferred_element_type=F32)
        term = yb * _sigmoid(gate)
        acc = term if acc is None else acc + term
    o_ref[...] = acc.astype(o_ref.dtype)
    _side_cast(side_in, side_out)


def _merge(h, ys, wgate, wbranch, side_weights, side_layer, *, tm=1024, tn=256):
    m, d = h.shape
    w = W_BRANCH
    nj = d // tn
    j0 = OFF_GATE // tn
    gate_spec = lambda k: pl.BlockSpec((None, d, tn), lambda i, j: (0, 0, j0 + k * nj + j))
    br_spec = lambda k: pl.BlockSpec((1, w, tn), lambda i, j: (k, 0, j))
    s_in, s_out, s_shape = _side_cast_specs(side_weights, side_layer, (m // tm) * nj, lambda i, j: i * nj + j)
    outs = pl.pallas_call(
        functools.partial(_merge_kernel, n_side=len(side_weights)),
        grid=(m // tm, nj),
        in_specs=[pl.BlockSpec((tm, d), lambda i, j: (i, 0))]
        + [pl.BlockSpec((tm, w), lambda i, j: (i, 0))] * N_BRANCH
        + [gate_spec(k) for k in range(N_BRANCH)]
        + [br_spec(k) for k in range(N_BRANCH)]
        + s_in,
        out_specs=[pl.BlockSpec((tm, tn), lambda i, j: (i, j))] + s_out,
        out_shape=[jax.ShapeDtypeStruct((m, d), BF16)] + s_shape,
        compiler_params=_cparams(("arbitrary", "arbitrary")),
        name="gated_merge",
    )(h, *ys, *([wgate] * N_BRANCH), *([wbranch] * N_BRANCH), *side_weights)
    return outs[0], outs[1:]


def _out_kernel(m_ref, w_ref, x_ref, g_ref, gn_ref, o_ref, hn_ref):
    mix = jnp.dot(m_ref[...], w_ref[...], preferred_element_type=F32)
    x1 = x_ref[...] + _rms(mix, g_ref[...])
    o_ref[...] = x1
    hn_ref[...] = _rms(x1, gn_ref[...]).astype(hn_ref.dtype)


def _out_proj(merged, w_o, x2, g, g_next, layer, *, tm=512):
    m, d = x2.shape
    row = pl.BlockSpec((tm, d), lambda i: (i, 0))
    vec = pl.BlockSpec((1, d), lambda i: (0, 0))
    return pl.pallas_call(
        _out_kernel,
        grid=(m // tm,),
        in_specs=[row, pl.BlockSpec((None, d, d), lambda i: (layer, 0, 0), pipeline_mode=pl.Buffered(1)),
                  row, vec, vec],
        out_specs=[row, row],
        out_shape=[jax.ShapeDtypeStruct((m, d), F32), jax.ShapeDtypeStruct((m, d), BF16)],
        compiler_params=_cparams(("parallel",)),
        name="out_proj",
    )(merged, w_o, x2, g, g_next)


def _ffn_kernel(*refs, rows, n_side):
    x_ref, h_ref, w1_ref, w2_ref, gpost_ref = refs[:5]
    side_in = refs[5:5 + n_side]
    o_ref = refs[5 + n_side]
    side_out = refs[6 + n_side:6 + 2 * n_side]
    j = pl.program_id(1)

    @pl.when(j == 0)
    def _():
        o_ref[...] = jnp.zeros(o_ref.shape, o_ref.dtype)

    a = jnp.dot(h_ref[...], w1_ref[...], preferred_element_type=F32)
    a = jnp.maximum(a, 0.0)
    a = (a * a).astype(BF16)
    nw = 512
    for c in range(o_ref.shape[1] // nw):
        o_ref[:, c * nw:(c + 1) * nw] += jnp.dot(a, w2_ref[:, c * nw:(c + 1) * nw],
                                                 preferred_element_type=F32)
    _side_cast(side_in, side_out)

    @pl.when(j == pl.num_programs(1) - 1)
    def _():
        def body(i, carry):
            r = pl.multiple_of(i * rows, rows)
            o_ref[pl.ds(r, rows), :] = (x_ref[pl.ds(r, rows), :]
                                        + _rms(o_ref[pl.ds(r, rows), :], gpost_ref[...]))
            return carry
        lax.fori_loop(0, x_ref.shape[0] // rows, body, 0, unroll=2)


def _ffn(x2, h2, w1, w2, gpost, side_weights, side_layer, *, tm=1024, tf=512):
    m, d = x2.shape
    f = w1.shape[2]
    nj = f // tf
    s_in, s_out, s_shape = _side_cast_specs(side_weights, side_layer, (m // tm) * nj, lambda i, j: i * nj + j)
    outs = pl.pallas_call(
        functools.partial(_ffn_kernel, rows=128, n_side=len(side_weights)),
        grid=(m // tm, nj),
        in_specs=[
            pl.BlockSpec((tm, d), lambda i, j: (i, 0)),
            pl.BlockSpec((tm, d), lambda i, j: (i, 0)),
            pl.BlockSpec((None, d, tf), lambda i, j: (0, 0, j)),
            pl.BlockSpec((None, tf, d), lambda i, j: (0, j, 0)),
            pl.BlockSpec((1, d), lambda i, j: (0, 0)),
        ] + s_in,
        out_specs=[pl.BlockSpec((tm, d), lambda i, j: (i, 0))] + s_out,
        out_shape=[jax.ShapeDtypeStruct((m, d), F32)] + s_shape,
        compiler_params=_cparams(("arbitrary", "arbitrary")),
        name="ffn",
    )(x2, h2, w1, w2, gpost, *side_weights)
    return outs[0], outs[1:]


def kernel(x, g_pre_mix, w_in, ssm_a_re, ssm_a_im, ssm_log_dt, ssm_b_re, ssm_b_im, ssm_c_re, ssm_c_im,
           ssm_d, ssm_w_glu, pool_w, pool_scale, conv_w, conv_b, conv_ln_g, conv_ln_b, gmlp_ln_g,
           gmlp_ln_b, gmlp_ws, gmlp_bs, w_branch, w_o, g_post_mix, g_pre_mlp, w_ff1, w_ff2, g_post_mlp):
    nb, seq, d = x.shape
    m = nb * seq
    depth = w_in.shape[0]
    nchunk = seq // SSM_T
    assert d == D_MODEL and nchunk == 1 << SSM_SCAN_STEPS

    toep, bpow, cpow, lama, lamb = _ssm_prep(ssm_a_re, ssm_a_im, ssm_log_dt, ssm_b_re, ssm_b_im,
                                             ssm_c_re, ssm_c_im)

    w_br2 = w_branch.reshape(depth, N_BRANCH * W_BRANCH, d)
    w_in_b = _cast_bf16(w_in, 0, 256)

    row = lambda a: a.reshape(1, -1)
    x2 = x.reshape(m, d)
    for l in range(depth):
        h, u, y_pool, y_conv, y_gmlp = _proj_mix(
            x2, row(g_pre_mix[l]), w_in_b, 0, pool_w[l].astype(BF16), row(pool_scale[l]),
            jnp.broadcast_to(conv_w[l][:, None, :], (CONV_WIDTH, 8, W_BRANCH)),
            row(conv_b[l]), row(conv_ln_g[l]), row(conv_ln_b[l]), row(gmlp_ln_g[l]), row(gmlp_ln_b[l]),
            gmlp_ws[l], jnp.transpose(gmlp_bs[l]), seq=seq)
        yraw, (w_br_b, w_o_b) = _ssm_scan(u, toep, bpow, cpow, lama, lamb, l, (w_br2, w_o), nchunk=nchunk)
        y_ssm = _ssm_post(yraw, u, row(ssm_d[l]), ssm_w_glu[l].astype(BF16))
        merged, (w_ff1_b, w_ff2_b) = _merge(h, (y_ssm, y_pool, y_conv, y_gmlp), w_in_b,
                                            w_br_b.reshape(N_BRANCH, W_BRANCH, d), (w_ff1, w_ff2), l)
        x2, h2 = _out_proj(merged, w_o_b, x2, row(g_post_mix[l]), row(g_pre_mlp[l]), 0)
        nxt = (w_in,) if l + 1 < depth else ()
        x2, nxt_b = _ffn(x2, h2, w_ff1_b, w_ff2_b, row(g_post_mlp[l]), nxt, l + 1)
        if nxt:
            (w_in_b,) = nxt_b
    return x2.reshape(nb, seq, d)
```

```python
import functools

import jax
import jax.numpy as jnp
from jax import lax
from jax.experimental import pallas as pl
from jax.experimental.pallas import tpu as pltpu

D_MODEL = 2048
N_BRANCH = 4
W_BRANCH = D_MODEL // 4
SSM_GROUP = 16
SSM_STATE = 64
POOL_WINDOWS = (2, 4, 8, 16)
POOL_GW = W_BRANCH // len(POOL_WINDOWS)
CONV_WIDTH = 31
GMLP_CHUNK = 128
GMLP_HEADS = 4
GMLP_HD = W_BRANCH // GMLP_HEADS
EPS = 1e-6

LANES = 128
SUBLANES = 8
BF16_ROWS = 2 * SUBLANES
V7X_VMEM_BYTES = 64 * 1024 * 1024

OFF_SSM = 0
OFF_POOL = OFF_SSM + W_BRANCH
OFF_CONV = OFF_POOL + W_BRANCH
OFF_GMLP = OFF_CONV + 2 * W_BRANCH
OFF_GATE = OFF_GMLP + 2 * W_BRANCH

SSM_T = 16
SSM_LANES = SSM_T * SSM_GROUP
SSM_SCAN_STEPS = 7
CONV_HALO = 32
POOL_HALO = SUBLANES * len(POOL_WINDOWS)
assert all(win == 2 << k for k, win in enumerate(POOL_WINDOWS))
CONV_RB = 64

VMEM_LIMIT = V7X_VMEM_BYTES - 8 * 1024 * 1024
TM_PROJ = 512
TM_MERGE, TN_MERGE = 1024, 256
TM_OUT = 512
TM_FFN, TF_FFN = 1024, 512
TM_POST = 512
CAST_ROWS = 256
NORM_ROWS = 128
BF16 = jnp.bfloat16
F32 = jnp.float32


def _cparams(sem):
    return pltpu.CompilerParams(dimension_semantics=sem, vmem_limit_bytes=VMEM_LIMIT)


def _sigmoid(x):
    return 0.5 * (1.0 + jnp.tanh(0.5 * x))


def _gelu(x):
    c = 0.7978845608028654
    return 0.5 * x * (1.0 + jnp.tanh(c * (x + 0.044715 * (x * x * x))))


def _rms(x, g):
    ms = jnp.mean(x * x, axis=-1, keepdims=True)
    return x * lax.rsqrt(ms + EPS) * g


def _layer_norm(x, g, b):
    mu = jnp.mean(x, axis=-1, keepdims=True)
    xc = x - mu
    var = jnp.mean(xc * xc, axis=-1, keepdims=True)
    return xc * lax.rsqrt(var + EPS) * g + b


def _cast_kernel(w_ref, o_ref):
    o_ref[...] = w_ref[...].astype(o_ref.dtype)


def _cast_bf16(w, layer, rows):
    _, r, c = w.shape
    return pl.pallas_call(
        _cast_kernel,
        grid=(r // rows,),
        in_specs=[pl.BlockSpec((1, rows, c), lambda i: (layer, i, 0))],
        out_specs=pl.BlockSpec((1, rows, c), lambda i: (0, i, 0)),
        out_shape=jax.ShapeDtypeStruct((1, r, c), BF16),
        compiler_params=_cparams(("parallel",)),
        name="cast_bf16",
    )(w)


def _side_cast_specs(weights, layer, nsteps, step_of):
    in_specs, out_specs, out_shapes = [], [], []
    for w in weights:
        _, r, c = w.shape
        assert r % nsteps == 0 and (r // nsteps) % BF16_ROWS == 0
        rows = r // nsteps
        in_specs.append(pl.BlockSpec((None, rows, c), lambda *ids: (layer, step_of(*ids), 0)))
        out_specs.append(pl.BlockSpec((None, rows, c), lambda *ids: (0, step_of(*ids), 0)))
        out_shapes.append(jax.ShapeDtypeStruct((1, r, c), BF16))
    return in_specs, out_specs, out_shapes


def _side_cast(in_refs, out_refs):
    for wi, wo in zip(in_refs, out_refs):
        wo[...] = wi[...].astype(wo.dtype)


def _proj_mix_kernel(x_ref, g_ref, w_ref, poolw_ref, pscale_ref, convw_ref, convb_ref, clg_ref, clb_ref,
                     glg_ref, glb_ref, ws_ref, bst_ref,
                     h_ref, u_ref, ypool_ref, yconv_ref, ygmlp_ref, vbuf, pbuf, sbuf, qbuf, ugbuf, vgbuf,
                     *, rows, tiles_per_seq):
    c = pl.program_id(0) % tiles_per_seq
    tc = x_ref.shape[0]
    w = W_BRANCH

    @pl.when(c == 0)
    def _():
        vbuf[0:CONV_HALO, :] = jnp.zeros((CONV_HALO, w), F32)
        pbuf[0:POOL_HALO, :] = jnp.zeros((POOL_HALO, w), F32)

    def body(i, carry):
        r = pl.multiple_of(i * rows, rows)
        h_ref[pl.ds(r, rows), :] = _rms(x_ref[pl.ds(r, rows), :], g_ref[...]).astype(h_ref.dtype)
        return carry
    lax.fori_loop(0, tc // rows, body, 0, unroll=2)

    h = h_ref[...]
    proj = lambda off, width: jnp.dot(h, w_ref[:, off:off + width], preferred_element_type=F32)
    gm = proj(OFF_GMLP, 2 * w)
    ugbuf[...] = _gelu(gm[:, :w])
    vgbuf[...] = _layer_norm(_gelu(gm[:, w:]), glg_ref[...], glb_ref[...]).astype(vgbuf.dtype)
    cv = proj(OFF_CONV, 2 * w)
    vbuf[CONV_HALO:CONV_HALO + tc, :] = cv[:, :w] * _sigmoid(cv[:, w:])
    pbuf[POOL_HALO:POOL_HALO + tc, :] = proj(OFF_POOL, w)
    u_ref[...] = proj(OFF_SSM, w)

    src = pbuf
    for k in range(len(POOL_WINDOWS)):
        lo, sh, c0 = SUBLANES * (k + 1), 1 << k, k * POOL_GW
        n = POOL_HALO + tc - lo
        a_hi = src[lo:lo + n, c0:] if k == 0 else src[k - 1, lo:lo + n, c0:]
        a_lo = src[lo - sh:lo - sh + n, c0:] if k == 0 else src[k - 1, lo - sh:lo - sh + n, c0:]
        qbuf[k, lo:lo + n, c0:] = a_hi + a_lo
        src = qbuf
    t = c * tc + lax.broadcasted_iota(jnp.int32, (tc, 1), 0)
    for gi, win in enumerate(POOL_WINDOWS):
        cols = slice(gi * POOL_GW, (gi + 1) * POOL_GW)
        tok = pbuf[POOL_HALO:POOL_HALO + tc, cols]
        s = qbuf[gi, POOL_HALO:POOL_HALO + tc, cols]
        count = jnp.minimum(t + 1, win).astype(F32)
        pooled = s / count - tok
        mixed = jnp.dot(pooled.astype(BF16), poolw_ref[gi], preferred_element_type=F32)
        ypool_ref[:, cols] = (mixed * pscale_ref[:, cols]).astype(ypool_ref.dtype)
    pbuf[0:POOL_HALO, :] = pbuf[tc:tc + POOL_HALO, :]

    sl = SUBLANES
    span = CONV_HALO - sl
    for b in range(1, sl):
        sbuf[b - 1] = vbuf[sl - b:sl - b + span + tc, :]
    for rb in range(tc // CONV_RB):
        r0 = rb * CONV_RB
        acc = jnp.broadcast_to(convb_ref[...], (CONV_RB, w))
        for b in range(sl):
            for a in range((CONV_WIDTH - 1 - b) // sl + 1):
                k = CONV_WIDTH - 1 - (sl * a + b)
                lo = span - sl * a + r0
                tap = sbuf[b - 1, lo:lo + CONV_RB, :] if b else vbuf[sl + lo:sl + lo + CONV_RB, :]
                wk = convw_ref[k]
                acc = acc + (tap.reshape(CONV_RB // sl, sl, w) * wk[None]).reshape(CONV_RB, w)
        yn = _layer_norm(acc, clg_ref[...], clb_ref[...])
        yconv_ref[r0:r0 + CONV_RB, :] = (yn * _sigmoid(yn)).astype(yconv_ref.dtype)
    vbuf[0:CONV_HALO, :] = vbuf[tc:tc + CONV_HALO, :]

    ri = lax.broadcasted_iota(jnp.int32, (GMLP_CHUNK, GMLP_CHUNK), 0)
    ci = lax.broadcasted_iota(jnp.int32, (GMLP_CHUNK, GMLP_CHUNK), 1)
    causal = ri >= ci
    for hd in range(GMLP_HEADS):
        wsm = jnp.where(causal, ws_ref[hd], 0.0).astype(BF16)
        cols = slice(hd * GMLP_HD, (hd + 1) * GMLP_HD)
        for ch in range(tc // GMLP_CHUNK):
            rws = slice(ch * GMLP_CHUNK, (ch + 1) * GMLP_CHUNK)
            sv = jnp.dot(wsm, vgbuf[rws, cols], preferred_element_type=F32) + bst_ref[:, hd:hd + 1]
            ygmlp_ref[rws, cols] = (ugbuf[rws, cols] * sv).astype(ygmlp_ref.dtype)


def _proj_mix(x2, g, w_in, layer, poolw, pscale, convw, convb, clg, clb, glg, glb, ws, bst, *, seq, tm=TM_PROJ):
    m, d = x2.shape
    n = OFF_GATE
    w = W_BRANCH
    assert seq % tm == 0 and tm % GMLP_CHUNK == 0 and tm % CONV_RB == 0
    row = lambda width: pl.BlockSpec((tm, width), lambda i: (i, 0))
    full = lambda a: pl.BlockSpec(a.shape, lambda i: (0,) * a.ndim)
    args = (poolw, pscale, convw, convb, clg, clb, glg, glb, ws, bst)
    return pl.pallas_call(
        functools.partial(_proj_mix_kernel, rows=NORM_ROWS, tiles_per_seq=seq // tm),
        grid=(m // tm,),
        in_specs=[row(d), full(g),
                  pl.BlockSpec((None, d, n), lambda i: (layer, 0, 0), pipeline_mode=pl.Buffered(1))]
        + [full(a) for a in args],
        out_specs=[row(d), row(w), row(w), row(w), row(w)],
        out_shape=[jax.ShapeDtypeStruct((m, d), BF16), jax.ShapeDtypeStruct((m, w), F32)]
        + [jax.ShapeDtypeStruct((m, w), BF16)] * 3,
        scratch_shapes=[pltpu.VMEM((CONV_HALO + tm, w), F32), pltpu.VMEM((POOL_HALO + tm, w), F32),
                        pltpu.VMEM((SUBLANES - 1, CONV_HALO - SUBLANES + tm, w), F32),
                        pltpu.VMEM((len(POOL_WINDOWS), POOL_HALO + tm, w), F32),
                        pltpu.VMEM((tm, w), F32), pltpu.VMEM((tm, w), BF16)],
        compiler_params=_cparams(("arbitrary",)),
        name="proj_mix",
    )(x2, g, w_in, *args)


PREP_GB = 8


def _ssm_prep_kernel(ar_ref, ai_ref, ldt_ref, cre_ref, cim_ref, bre_ref, bim_ref,
                     toep_ref, bpow_ref, cpow_ref, lama_ref, lamb_ref, cl_s, bl_s):
    n = SSM_STATE
    p = SSM_GROUP
    lanes = SSM_LANES
    lane = lax.broadcasted_iota(jnp.int32, (1, lanes), 1)
    for e in range(PREP_GB):
        ar = ar_ref[e]
        ai = ai_ref[e]
        dt = jnp.exp(ldt_ref[e])
        mag = jnp.exp(ar * dt)
        ang = ai * dt
        lr = mag * jnp.cos(ang)
        li = mag * jnp.sin(ang)
        den = ar * ar + ai * ai
        fr = ((lr - 1.0) * ar + li * ai) / den
        fi = (li * ar - (lr - 1.0) * ai) / den
        cre = cre_ref[e]
        cim = cim_ref[e]
        br = bre_ref[e]
        bi = bim_ref[e]
        bbr = fr * br - fi * bi
        bbi = fr * bi + fi * br
        pr = jnp.ones_like(lr)
        pi = jnp.zeros_like(lr)
        for k in range(SSM_T + 1):
            cl_s[k * p:(k + 1) * p, 0:n] = cre * pr - cim * pi
            cl_s[k * p:(k + 1) * p, n:2 * n] = -(cre * pi + cim * pr)
            if k < SSM_T:
                s = SSM_T - 1 - k
                bl_s[s * p:(s + 1) * p, 0:n] = pr * bbr - pi * bbi
                bl_s[s * p:(s + 1) * p, n:2 * n] = pr * bbi + pi * bbr
                pr, pi = pr * lr - pi * li, pr * li + pi * lr
        qr, qi = pr, pi
        for j in range(SUBLANES):
            if j < SSM_SCAN_STEPS:
                lama_ref[e, j:j + 1, 0:n] = qr
                lama_ref[e, j:j + 1, n:2 * n] = qr
                lamb_ref[e, j:j + 1, 0:n] = -qi
                lamb_ref[e, j:j + 1, n:2 * n] = qi
                qr, qi = qr * qr - qi * qi, 2.0 * (qr * qi)
            else:
                lama_ref[e, j:j + 1, :] = jnp.zeros((1, 2 * n), F32)
                lamb_ref[e, j:j + 1, :] = jnp.zeros((1, 2 * n), F32)
        cpow_ref[e] = cl_s[p:(SSM_T + 1) * p, :].astype(cpow_ref.dtype)
        bpow_ref[e] = bl_s[...].astype(bpow_ref.dtype)
        bbcat = bl_s[(SSM_T - 1) * p:SSM_T * p, :]
        kt = lax.dot_general(bbcat, cl_s[0:SSM_T * p, :], (((1,), (1,)), ((), ())),
                             preferred_element_type=F32, precision=lax.Precision.HIGHEST)
        for s in range(SSM_T):
            blk = pltpu.roll(kt, s * p, axis=1) if s else kt
            toep_ref[e, s * p:(s + 1) * p, :] = jnp.where(lane >= s * p, blk, 0.0).astype(toep_ref.dtype)


def _ssm_prep(a_re, a_im, log_dt, b_re, b_im, c_re, c_im):
    n, p = SSM_STATE, SSM_GROUP
    gg = a_re.shape[0] * a_re.shape[1]
    gb = PREP_GB
    lanes = SSM_LANES
    row = lambda a: a.reshape(gg, 1, n)
    ldt = jnp.broadcast_to(log_dt.reshape(gg, 1), (gg, n))
    bt = lambda b: jnp.transpose(b.reshape(gg, n, p), (0, 2, 1))
    rspec = pl.BlockSpec((gb, 1, n), lambda g: (g, 0, 0))
    pspec = pl.BlockSpec((gb, p, n), lambda g: (g, 0, 0))
    ospec = lambda a, b: pl.BlockSpec((gb, a, b), lambda g: (g, 0, 0))
    return pl.pallas_call(
        _ssm_prep_kernel,
        grid=(gg // gb,),
        in_specs=[rspec, rspec, rspec, pspec, pspec, pspec, pspec],
        out_specs=[ospec(lanes, lanes), ospec(lanes, 2 * n), ospec(lanes, 2 * n),
                   ospec(SUBLANES, 2 * n), ospec(SUBLANES, 2 * n)],
        out_shape=[
            jax.ShapeDtypeStruct((gg, lanes, lanes), BF16),
            jax.ShapeDtypeStruct((gg, lanes, 2 * n), BF16),
            jax.ShapeDtypeStruct((gg, lanes, 2 * n), BF16),
            jax.ShapeDtypeStruct((gg, SUBLANES, 2 * n), F32),
            jax.ShapeDtypeStruct((gg, SUBLANES, 2 * n), F32),
        ],
        scratch_shapes=[pltpu.VMEM(((SSM_T + 1) * p, 2 * n), F32), pltpu.VMEM((SSM_T * p, 2 * n), F32)],
        compiler_params=_cparams(("arbitrary",)),
        name="ssm_prep",
    )(row(a_re), row(a_im), row(ldt), c_re.reshape(gg, p, n), c_im.reshape(gg, p, n), bt(b_re), bt(b_im))


SSM_GB = LANES // SSM_GROUP
SSM_RB = BF16_ROWS


def _piece_transpose(xs, masks):
    n = len(xs)
    w = LANES // n
    rolled = []
    for k in range(n):
        wk = xs[k % n]
        for a in range(1, n):
            wk = jnp.where(masks[a], xs[(a + k) % n], wk)
        rolled.append(pltpu.roll(wk, w * k, axis=1) if k else wk)
    ys = []
    for a in range(n):
        ya = rolled[(0 - a) % n]
        for b in range(1, n):
            ya = jnp.where(masks[b], rolled[(b - a) % n], ya)
        ys.append(ya)
    return ys


def _ssm_kernel(*refs, nchunk, pad, n_side):
    u_ref, toep_ref, bpow_ref, cpow_ref, lama_ref, lamb_ref = refs[:6]
    side_in = refs[6:6 + n_side]
    y_ref = refs[6 + n_side]
    side_out = refs[7 + n_side:7 + 2 * n_side]
    up_ref, yp_ref, xs_ref = refs[7 + 2 * n_side:]
    rows = up_ref.shape[1]
    n2 = 2 * SSM_STATE
    t = SSM_T
    lane = lax.broadcasted_iota(jnp.int32, (1, LANES), 1)
    masks = [lane // SSM_GROUP == a for a in range(SSM_GB)]
    nhalf = t // SSM_GB

    def to_chunks(rb, carry):
        r = pl.multiple_of(rb * SSM_RB, SSM_RB)
        for hf in range(nhalf):
            xs = [u_ref[pl.ds(r * t + hf * SSM_GB + b, SSM_RB, stride=t), :] for b in range(SSM_GB)]
            ys = _piece_transpose(xs, masks)
            for a in range(SSM_GB):
                up_ref[a, pl.ds(r, SSM_RB), hf * LANES:(hf + 1) * LANES] = ys[a].astype(up_ref.dtype)
        return carry
    lax.fori_loop(0, rows // SSM_RB, to_chunks, 0, unroll=2)

    _side_cast(side_in, side_out)
    cidx = lax.broadcasted_iota(jnp.int32, (rows, 1), 0) % nchunk
    xs_ref[0:pad, :] = jnp.zeros((pad, n2), F32)
    for a in range(SSM_GB):
        u = up_ref[a]
        xs_ref[pad:pad + rows, :] = jnp.dot(u, bpow_ref[a], preferred_element_type=F32)
        for j in range(SSM_SCAN_STEPS):
            sh = 1 << j
            cur = xs_ref[pad:pad + rows, :]
            prev = jnp.where(cidx >= sh, xs_ref[pad - sh:pad - sh + rows, :], 0.0)
            la = lama_ref[a, j:j + 1, :]
            lb = lamb_ref[a, j:j + 1, :]
            xs_ref[pad:pad + rows, :] = cur + prev * la + pltpu.roll(prev, SSM_STATE, axis=1) * lb
        xprev = jnp.where(cidx >= 1, xs_ref[pad - 1:pad - 1 + rows, :], 0.0).astype(BF16)
        yp_ref[a] = (jnp.dot(u, toep_ref[a], preferred_element_type=F32)
                     + lax.dot_general(xprev, cpow_ref[a], (((1,), (1,)), ((), ())),
                                       preferred_element_type=F32))

    def to_tokens(rb, carry):
        r = pl.multiple_of(rb * SUBLANES, SUBLANES)
        for hf in range(nhalf):
            ys = [yp_ref[a, pl.ds(r, SUBLANES), hf * LANES:(hf + 1) * LANES] for a in range(SSM_GB)]
            xs = _piece_transpose(ys, masks)
            for b in range(SSM_GB):
                y_ref[pl.ds(r * t + hf * SSM_GB + b, SUBLANES, stride=t), :] = xs[b]
        return carry
    lax.fori_loop(0, rows // SUBLANES, to_tokens, 0, unroll=2)


def _ssm_scan(pm, toep, bpow, cpow, lama, lamb, layer, side_weights, *, nchunk):
    m = pm.shape[0]
    rows = m // SSM_T
    lanes = SSM_LANES
    n2 = 2 * SSM_STATE
    pad = 1 << (SSM_SCAN_STEPS - 1)
    nblk = W_BRANCH // LANES
    gspec = lambda a, b: pl.BlockSpec((SSM_GB, a, b), lambda i: (layer * nblk + i, 0, 0))
    s_in, s_out, s_shape = _side_cast_specs(side_weights, layer, nblk, lambda i: i)
    outs = pl.pallas_call(
        functools.partial(_ssm_kernel, nchunk=nchunk, pad=pad, n_side=len(side_weights)),
        grid=(nblk,),
        in_specs=[pl.BlockSpec((m, LANES), lambda i: (0, i)),
                  gspec(lanes, lanes), gspec(lanes, n2), gspec(lanes, n2),
                  gspec(SUBLANES, n2), gspec(SUBLANES, n2)] + s_in,
        out_specs=[pl.BlockSpec((m, LANES), lambda i: (0, i))] + s_out,
        out_shape=[jax.ShapeDtypeStruct((m, W_BRANCH), F32)] + s_shape,
        scratch_shapes=[pltpu.VMEM((SSM_GB, rows, lanes), BF16), pltpu.VMEM((SSM_GB, rows, lanes), F32),
                        pltpu.VMEM((pad + rows, n2), F32)],
        compiler_params=_cparams(("arbitrary",)),
        name="ssm_scan",
    )(pm, toep, bpow, cpow, lama, lamb, *side_weights)
    return outs[0], outs[1:]


def _merge_kernel(*refs, n_side, post_rows):
    nb = N_BRANCH
    w = W_BRANCH
    h_ref, yraw_ref, u_ref, d_ref, wglu_ref = refs[:5]
    ys = refs[5:4 + nb]
    wg = refs[4 + nb:4 + 2 * nb]
    wb = refs[4 + 2 * nb:4 + 3 * nb]
    side_in = refs[4 + 3 * nb:4 + 3 * nb + n_side]
    o_ref = refs[4 + 3 * nb + n_side]
    side_out = refs[5 + 3 * nb + n_side:5 + 3 * nb + 2 * n_side]
    yssm_ref = refs[5 + 3 * nb + 2 * n_side]

    @pl.when(pl.program_id(1) == 0)
    def _():
        for r0 in range(0, h_ref.shape[0], post_rows):
            rs = slice(r0, r0 + post_rows)
            y = yraw_ref[rs, :] + d_ref[...] * u_ref[rs, :]
            zg = jnp.dot(_gelu(y).astype(BF16), wglu_ref[...], preferred_element_type=F32)
            yssm_ref[rs, :] = (zg[:, :w] * _sigmoid(zg[:, w:])).astype(yssm_ref.dtype)

    ys = (yssm_ref,) + tuple(ys)
    h = h_ref[...]
    acc = None
    for k in range(nb):
        gate = jnp.dot(h, wg[k][...], preferred_element_type=F32)
        yb = jnp.dot(ys[k][...], wb[k][0], preferred_element_type=F32)
        term = yb * _sigmoid(gate)
        acc = term if acc is None else acc + term
    o_ref[...] = acc.astype(o_ref.dtype)
    _side_cast(side_in, side_out)


def _merge(h, yraw, u, ssm_d, wglu, ys, wgate, wbranch, side_weights, side_layer, *, tm=TM_MERGE, tn=TN_MERGE):
    m, d = h.shape
    w = W_BRANCH
    nj = d // tn
    j0 = OFF_GATE // tn
    row = lambda width: pl.BlockSpec((tm, width), lambda i, j: (i, 0))
    full = lambda a: pl.BlockSpec(a.shape, lambda i, j: (0,) * a.ndim)
    gate_spec = lambda k: pl.BlockSpec((None, d, tn), lambda i, j: (0, 0, j0 + k * nj + j))
    br_spec = lambda k: pl.BlockSpec((1, w, tn), lambda i, j: (k, 0, j))
    s_in, s_out, s_shape = _side_cast_specs(side_weights, side_layer, (m // tm) * nj, lambda i, j: i * nj + j)
    outs = pl.pallas_call(
        functools.partial(_merge_kernel, n_side=len(side_weights), post_rows=TM_POST),
        grid=(m // tm, nj),
        in_specs=[row(d), row(w), row(w), full(ssm_d), full(wglu)]
        + [row(w)] * len(ys)
        + [gate_spec(k) for k in range(N_BRANCH)]
        + [br_spec(k) for k in range(N_BRANCH)]
        + s_in,
        out_specs=[pl.BlockSpec((tm, tn), lambda i, j: (i, j))] + s_out,
        out_shape=[jax.ShapeDtypeStruct((m, d), BF16)] + s_shape,
        scratch_shapes=[pltpu.VMEM((tm, w), BF16)],
        compiler_params=_cparams(("arbitrary", "arbitrary")),
        name="gated_merge",
    )(h, yraw, u, ssm_d, wglu, *ys, *([wgate] * N_BRANCH), *([wbranch] * N_BRANCH), *side_weights)
    return outs[0], outs[1:]


def _out_kernel(m_ref, w_ref, x_ref, g_ref, gn_ref, o_ref, hn_ref):
    mix = jnp.dot(m_ref[...], w_ref[...], preferred_element_type=F32)
    x1 = x_ref[...] + _rms(mix, g_ref[...])
    o_ref[...] = x1
    hn_ref[...] = _rms(x1, gn_ref[...]).astype(hn_ref.dtype)


def _out_proj(merged, w_o, x2, g, g_next, layer, *, tm=TM_OUT):
    m, d = x2.shape
    row = pl.BlockSpec((tm, d), lambda i: (i, 0))
    vec = pl.BlockSpec((1, d), lambda i: (0, 0))
    return pl.pallas_call(
        _out_kernel,
        grid=(m // tm,),
        in_specs=[row, pl.BlockSpec((None, d, d), lambda i: (layer, 0, 0), pipeline_mode=pl.Buffered(1)),
                  row, vec, vec],
        out_specs=[row, row],
        out_shape=[jax.ShapeDtypeStruct((m, d), F32), jax.ShapeDtypeStruct((m, d), BF16)],
        compiler_params=_cparams(("parallel",)),
        name="out_proj",
    )(merged, w_o, x2, g, g_next)


def _ffn_kernel(*refs, rows, n_side):
    x_ref, h_ref, w1_ref, w2_ref, gpost_ref = refs[:5]
    side_in = refs[5:5 + n_side]
    o_ref = refs[5 + n_side]
    side_out = refs[6 + n_side:6 + 2 * n_side]
    j = pl.program_id(1)

    @pl.when(j == 0)
    def _():
        o_ref[...] = jnp.zeros(o_ref.shape, o_ref.dtype)

    a = jnp.dot(h_ref[...], w1_ref[...], preferred_element_type=F32)
    a = jnp.maximum(a, 0.0)
    a = (a * a).astype(BF16)
    nw = 512
    for c in range(o_ref.shape[1] // nw):
        o_ref[:, c * nw:(c + 1) * nw] += jnp.dot(a, w2_ref[:, c * nw:(c + 1) * nw],
                                                 preferred_element_type=F32)
    _side_cast(side_in, side_out)

    @pl.when(j == pl.num_programs(1) - 1)
    def _():
        def body(i, carry):
            r = pl.multiple_of(i * rows, rows)
            o_ref[pl.ds(r, rows), :] = (x_ref[pl.ds(r, rows), :]
                                        + _rms(o_ref[pl.ds(r, rows), :], gpost_ref[...]))
            return carry
        lax.fori_loop(0, x_ref.shape[0] // rows, body, 0, unroll=2)


def _ffn(x2, h2, w1, w2, gpost, side_weights, side_layer, *, tm=TM_FFN, tf=TF_FFN):
    m, d = x2.shape
    f = w1.shape[2]
    nj = f // tf
    s_in, s_out, s_shape = _side_cast_specs(side_weights, side_layer, (m // tm) * nj, lambda i, j: i * nj + j)
    outs = pl.pallas_call(
        functools.partial(_ffn_kernel, rows=NORM_ROWS, n_side=len(side_weights)),
        grid=(m // tm, nj),
        in_specs=[
            pl.BlockSpec((tm, d), lambda i, j: (i, 0)),
            pl.BlockSpec((tm, d), lambda i, j: (i, 0)),
            pl.BlockSpec((None, d, tf), lambda i, j: (0, 0, j)),
            pl.BlockSpec((None, tf, d), lambda i, j: (0, j, 0)),
            pl.BlockSpec((1, d), lambda i, j: (0, 0)),
        ] + s_in,
        out_specs=[pl.BlockSpec((tm, d), lambda i, j: (i, 0))] + s_out,
        out_shape=[jax.ShapeDtypeStruct((m, d), F32)] + s_shape,
        compiler_params=_cparams(("arbitrary", "arbitrary")),
        name="ffn",
    )(x2, h2, w1, w2, gpost, *side_weights)
    return outs[0], outs[1:]


def kernel(x, g_pre_mix, w_in, ssm_a_re, ssm_a_im, ssm_log_dt, ssm_b_re, ssm_b_im, ssm_c_re, ssm_c_im,
           ssm_d, ssm_w_glu, pool_w, pool_scale, conv_w, conv_b, conv_ln_g, conv_ln_b, gmlp_ln_g,
           gmlp_ln_b, gmlp_ws, gmlp_bs, w_branch, w_o, g_post_mix, g_pre_mlp, w_ff1, w_ff2, g_post_mlp):
    nb, seq, d = x.shape
    m = nb * seq
    depth = w_in.shape[0]
    nchunk = seq // SSM_T
    assert d == D_MODEL and nchunk == 1 << SSM_SCAN_STEPS

    toep, bpow, cpow, lama, lamb = _ssm_prep(ssm_a_re, ssm_a_im, ssm_log_dt, ssm_b_re, ssm_b_im,
                                             ssm_c_re, ssm_c_im)

    w_br2 = w_branch.reshape(depth, N_BRANCH * W_BRANCH, d)
    w_in_b = _cast_bf16(w_in, 0, CAST_ROWS)

    row = lambda a: a.reshape(1, -1)
    x2 = x.reshape(m, d)
    for l in range(depth):
        h, u, y_pool, y_conv, y_gmlp = _proj_mix(
            x2, row(g_pre_mix[l]), w_in_b, 0, pool_w[l].astype(BF16), row(pool_scale[l]),
            jnp.broadcast_to(conv_w[l][:, None, :], (CONV_WIDTH, SUBLANES, W_BRANCH)),
            row(conv_b[l]), row(conv_ln_g[l]), row(conv_ln_b[l]), row(gmlp_ln_g[l]), row(gmlp_ln_b[l]),
            gmlp_ws[l], jnp.transpose(gmlp_bs[l]), seq=seq)
        yraw, (w_br_b, w_o_b) = _ssm_scan(u, toep, bpow, cpow, lama, lamb, l, (w_br2, w_o), nchunk=nchunk)
        merged, (w_ff1_b, w_ff2_b) = _merge(h, yraw, u, row(ssm_d[l]), ssm_w_glu[l].astype(BF16),
                                            (y_pool, y_conv, y_gmlp), w_in_b,
                                            w_br_b.reshape(N_BRANCH, W_BRANCH, d), (w_ff1, w_ff2), l)
        x2, h2 = _out_proj(merged, w_o_b, x2, row(g_post_mix[l]), row(g_pre_mlp[l]), 0)
        nxt = (w_in,) if l + 1 < depth else ()
        x2, nxt_b = _ffn(x2, h2, w_ff1_b, w_ff2_b, row(g_post_mlp[l]), nxt, l + 1)
        if nxt:
            (w_in_b,) = nxt_b
    return x2.reshape(nb, seq, d)
```

```python
import functools

import jax
import jax.numpy as jnp
from jax import lax
from jax.experimental import pallas as pl
from jax.experimental.pallas import tpu as pltpu

D_MODEL = 2048
N_BRANCH = 4
W_BRANCH = D_MODEL // 4
SSM_GROUP = 16
SSM_STATE = 64
POOL_WINDOWS = (2, 4, 8, 16)
POOL_GW = W_BRANCH // len(POOL_WINDOWS)
CONV_WIDTH = 31
GMLP_CHUNK = 128
GMLP_HEADS = 4
GMLP_HD = W_BRANCH // GMLP_HEADS
EPS = 1e-6

LANES = 128
SUBLANES = 8
BF16_ROWS = 2 * SUBLANES
V7X_VMEM_BYTES = 64 * 1024 * 1024

OFF_SSM = 0
OFF_POOL = OFF_SSM + W_BRANCH
OFF_CONV = OFF_POOL + W_BRANCH
OFF_GMLP = OFF_CONV + 2 * W_BRANCH
OFF_GATE = OFF_GMLP + 2 * W_BRANCH

SSM_T = 16
SSM_LANES = SSM_T * SSM_GROUP
SSM_SCAN_STEPS = 7
CONV_HALO = 32
POOL_HALO = SUBLANES * len(POOL_WINDOWS)
assert all(win == 2 << k for k, win in enumerate(POOL_WINDOWS))
CONV_RB = 64

VMEM_LIMIT = V7X_VMEM_BYTES - 8 * 1024 * 1024
TM_PROJ = 512
TM_MERGE, TN_MERGE = 1024, 256
TM_OUT = 512
TM_FFN, TF_FFN = 1024, 512
TM_POST = 512
CAST_ROWS = 256
NORM_ROWS = 128
BF16 = jnp.bfloat16
F32 = jnp.float32


def _cparams(sem):
    return pltpu.CompilerParams(dimension_semantics=sem, vmem_limit_bytes=VMEM_LIMIT)


def _sigmoid(x):
    return 0.5 * (1.0 + jnp.tanh(0.5 * x))


def _gelu(x):
    c = 0.7978845608028654
    return 0.5 * x * (1.0 + jnp.tanh(c * (x + 0.044715 * (x * x * x))))


def _rms(x, g):
    ms = jnp.mean(x * x, axis=-1, keepdims=True)
    return x * lax.rsqrt(ms + EPS) * g


def _layer_norm(x, g, b):
    mu = jnp.mean(x, axis=-1, keepdims=True)
    xc = x - mu
    var = jnp.mean(xc * xc, axis=-1, keepdims=True)
    return xc * lax.rsqrt(var + EPS) * g + b


def _cast_kernel(w_ref, o_ref):
    o_ref[...] = w_ref[...].astype(o_ref.dtype)


def _cast_bf16(w, layer, rows):
    _, r, c = w.shape
    return pl.pallas_call(
        _cast_kernel,
        grid=(r // rows,),
        in_specs=[pl.BlockSpec((1, rows, c), lambda i: (layer, i, 0))],
        out_specs=pl.BlockSpec((1, rows, c), lambda i: (0, i, 0)),
        out_shape=jax.ShapeDtypeStruct((1, r, c), BF16),
        compiler_params=_cparams(("parallel",)),
        name="cast_bf16",
    )(w)


def _side_cast_specs(weights, layer, nsteps, step_of):
    in_specs, out_specs, out_shapes = [], [], []
    for w in weights:
        _, r, c = w.shape
        assert r % nsteps == 0 and (r // nsteps) % BF16_ROWS == 0
        rows = r // nsteps
        in_specs.append(pl.BlockSpec((None, rows, c), lambda *ids: (layer, step_of(*ids), 0)))
        out_specs.append(pl.BlockSpec((None, rows, c), lambda *ids: (0, step_of(*ids), 0)))
        out_shapes.append(jax.ShapeDtypeStruct((1, r, c), BF16))
    return in_specs, out_specs, out_shapes


def _side_cast(in_refs, out_refs):
    for wi, wo in zip(in_refs, out_refs):
        wo[...] = wi[...].astype(wo.dtype)


def _proj_mix_kernel(x_ref, g_ref, w_ref, poolw_ref, pscale_ref, convw_ref, convb_ref, clg_ref, clb_ref,
                     glg_ref, glb_ref, ws_ref, bst_ref,
                     h_ref, u_ref, ypool_ref, yconv_ref, ygmlp_ref, vbuf, pbuf, sbuf, qbuf, ugbuf, vgbuf,
                     *, rows, tiles_per_seq):
    c = pl.program_id(0) % tiles_per_seq
    tc = x_ref.shape[0]
    w = W_BRANCH

    @pl.when(c == 0)
    def _():
        vbuf[0:CONV_HALO, :] = jnp.zeros((CONV_HALO, w), F32)
        pbuf[0:POOL_HALO, :] = jnp.zeros((POOL_HALO, w), F32)

    def body(i, carry):
        r = pl.multiple_of(i * rows, rows)
        h_ref[pl.ds(r, rows), :] = _rms(x_ref[pl.ds(r, rows), :], g_ref[...]).astype(h_ref.dtype)
        return carry
    lax.fori_loop(0, tc // rows, body, 0, unroll=2)

    h = h_ref[...]
    proj = lambda off, width: jnp.dot(h, w_ref[:, off:off + width], preferred_element_type=F32)
    gm = proj(OFF_GMLP, 2 * w)
    ugbuf[...] = _gelu(gm[:, :w])
    vgbuf[...] = _layer_norm(_gelu(gm[:, w:]), glg_ref[...], glb_ref[...]).astype(vgbuf.dtype)
    cv = proj(OFF_CONV, 2 * w)
    vbuf[CONV_HALO:CONV_HALO + tc, :] = cv[:, :w] * _sigmoid(cv[:, w:])
    pbuf[POOL_HALO:POOL_HALO + tc, :] = proj(OFF_POOL, w)
    u_ref[...] = proj(OFF_SSM, w)

    src = pbuf
    for k in range(len(POOL_WINDOWS)):
        lo, sh, c0 = SUBLANES * (k + 1), 1 << k, k * POOL_GW
        n = POOL_HALO + tc - lo
        a_hi = src[lo:lo + n, c0:] if k == 0 else src[k - 1, lo:lo + n, c0:]
        a_lo = src[lo - sh:lo - sh + n, c0:] if k == 0 else src[k - 1, lo - sh:lo - sh + n, c0:]
        qbuf[k, lo:lo + n, c0:] = a_hi + a_lo
        src = qbuf
    t = c * tc + lax.broadcasted_iota(jnp.int32, (tc, 1), 0)
    for gi, win in enumerate(POOL_WINDOWS):
        cols = slice(gi * POOL_GW, (gi + 1) * POOL_GW)
        tok = pbuf[POOL_HALO:POOL_HALO + tc, cols]
        s = qbuf[gi, POOL_HALO:POOL_HALO + tc, cols]
        count = jnp.minimum(t + 1, win).astype(F32)
        pooled = s / count - tok
        mixed = jnp.dot(pooled.astype(BF16), poolw_ref[gi], preferred_element_type=F32)
        ypool_ref[:, cols] = (mixed * pscale_ref[:, cols]).astype(ypool_ref.dtype)
    pbuf[0:POOL_HALO, :] = pbuf[tc:tc + POOL_HALO, :]

    sl = SUBLANES
    span = CONV_HALO - sl
    for b in range(1, sl):
        sbuf[b - 1] = vbuf[sl - b:sl - b + span + tc, :]
    for rb in range(tc // CONV_RB):
        r0 = rb * CONV_RB
        acc = jnp.broadcast_to(convb_ref[...], (CONV_RB, w))
        for b in range(sl):
            for a in range((CONV_WIDTH - 1 - b) // sl + 1):
                k = CONV_WIDTH - 1 - (sl * a + b)
                lo = span - sl * a + r0
                tap = sbuf[b - 1, lo:lo + CONV_RB, :] if b else vbuf[sl + lo:sl + lo + CONV_RB, :]
                wk = convw_ref[k]
                acc = acc + (tap.reshape(CONV_RB // sl, sl, w) * wk[None]).reshape(CONV_RB, w)
        yn = _layer_norm(acc, clg_ref[...], clb_ref[...])
        yconv_ref[r0:r0 + CONV_RB, :] = (yn * _sigmoid(yn)).astype(yconv_ref.dtype)
    vbuf[0:CONV_HALO, :] = vbuf[tc:tc + CONV_HALO, :]

    ri = lax.broadcasted_iota(jnp.int32, (GMLP_CHUNK, GMLP_CHUNK), 0)
    ci = lax.broadcasted_iota(jnp.int32, (GMLP_CHUNK, GMLP_CHUNK), 1)
    causal = ri >= ci
    for hd in range(GMLP_HEADS):
        wsm = jnp.where(causal, ws_ref[hd], 0.0).astype(BF16)
        cols = slice(hd * GMLP_HD, (hd + 1) * GMLP_HD)
        for ch in range(tc // GMLP_CHUNK):
            rws = slice(ch * GMLP_CHUNK, (ch + 1) * GMLP_CHUNK)
            sv = jnp.dot(wsm, vgbuf[rws, cols], preferred_element_type=F32) + bst_ref[:, hd:hd + 1]
            ygmlp_ref[rws, cols] = (ugbuf[rws, cols] * sv).astype(ygmlp_ref.dtype)


def _proj_mix(x2, g, w_in, layer, poolw, pscale, convw, convb, clg, clb, glg, glb, ws, bst, *, seq, tm=TM_PROJ):
    m, d = x2.shape
    n = OFF_GATE
    w = W_BRANCH
    assert seq % tm == 0 and tm % GMLP_CHUNK == 0 and tm % CONV_RB == 0
    row = lambda width: pl.BlockSpec((tm, width), lambda i: (i, 0))
    full = lambda a: pl.BlockSpec(a.shape, lambda i: (0,) * a.ndim)
    args = (poolw, pscale, convw, convb, clg, clb, glg, glb, ws, bst)
    return pl.pallas_call(
        functools.partial(_proj_mix_kernel, rows=NORM_ROWS, tiles_per_seq=seq // tm),
        grid=(m // tm,),
        in_specs=[row(d), full(g),
                  pl.BlockSpec((None, d, n), lambda i: (layer, 0, 0), pipeline_mode=pl.Buffered(1))]
        + [full(a) for a in args],
        out_specs=[row(d), row(w), row(w), row(w), row(w)],
        out_shape=[jax.ShapeDtypeStruct((m, d), BF16), jax.ShapeDtypeStruct((m, w), F32)]
        + [jax.ShapeDtypeStruct((m, w), BF16)] * 3,
        scratch_shapes=[pltpu.VMEM((CONV_HALO + tm, w), F32), pltpu.VMEM((POOL_HALO + tm, w), F32),
                        pltpu.VMEM((SUBLANES - 1, CONV_HALO - SUBLANES + tm, w), F32),
                        pltpu.VMEM((len(POOL_WINDOWS), POOL_HALO + tm, w), F32),
                        pltpu.VMEM((tm, w), F32), pltpu.VMEM((tm, w), BF16)],
        compiler_params=_cparams(("arbitrary",)),
        name="proj_mix",
    )(x2, g, w_in, *args)


PREP_GB = 8


def _ssm_prep_kernel(ar_ref, ai_ref, ldt_ref, cre_ref, cim_ref, bre_ref, bim_ref,
                     toep_ref, bpow_ref, cpow_ref, lama_ref, lamb_ref, cl_s, bl_s):
    n = SSM_STATE
    p = SSM_GROUP
    lanes = SSM_LANES
    lane = lax.broadcasted_iota(jnp.int32, (1, lanes), 1)
    for e in range(PREP_GB):
        ar = ar_ref[e]
        ai = ai_ref[e]
        dt = jnp.exp(ldt_ref[e])
        mag = jnp.exp(ar * dt)
        ang = ai * dt
        lr = mag * jnp.cos(ang)
        li = mag * jnp.sin(ang)
        den = ar * ar + ai * ai
        fr = ((lr - 1.0) * ar + li * ai) / den
        fi = (li * ar - (lr - 1.0) * ai) / den
        cre = cre_ref[e]
        cim = cim_ref[e]
        br = bre_ref[e]
        bi = bim_ref[e]
        bbr = fr * br - fi * bi
        bbi = fr * bi + fi * br
        pr = jnp.ones_like(lr)
        pi = jnp.zeros_like(lr)
        for k in range(SSM_T + 1):
            cl_s[k * p:(k + 1) * p, 0:n] = cre * pr - cim * pi
            cl_s[k * p:(k + 1) * p, n:2 * n] = -(cre * pi + cim * pr)
            if k < SSM_T:
                s = SSM_T - 1 - k
                bl_s[s * p:(s + 1) * p, 0:n] = pr * bbr - pi * bbi
                bl_s[s * p:(s + 1) * p, n:2 * n] = pr * bbi + pi * bbr
                pr, pi = pr * lr - pi * li, pr * li + pi * lr
        qr, qi = pr, pi
        for j in range(SUBLANES):
            if j < SSM_SCAN_STEPS:
                lama_ref[e, j:j + 1, 0:n] = qr
                lama_ref[e, j:j + 1, n:2 * n] = qr
                lamb_ref[e, j:j + 1, 0:n] = -qi
                lamb_ref[e, j:j + 1, n:2 * n] = qi
                qr, qi = qr * qr - qi * qi, 2.0 * (qr * qi)
            else:
                lama_ref[e, j:j + 1, :] = jnp.zeros((1, 2 * n), F32)
                lamb_ref[e, j:j + 1, :] = jnp.zeros((1, 2 * n), F32)
        cpow_ref[e] = cl_s[p:(SSM_T + 1) * p, :].astype(cpow_ref.dtype)
        bpow_ref[e] = bl_s[...].astype(bpow_ref.dtype)
        bbcat = bl_s[(SSM_T - 1) * p:SSM_T * p, :]
        kt = lax.dot_general(bbcat, cl_s[0:SSM_T * p, :], (((1,), (1,)), ((), ())),
                             preferred_element_type=F32, precision=lax.Precision.HIGHEST)
        for s in range(SSM_T):
            blk = pltpu.roll(kt, s * p, axis=1) if s else kt
            toep_ref[e, s * p:(s + 1) * p, :] = jnp.where(lane >= s * p, blk, 0.0).astype(toep_ref.dtype)


def _ssm_prep(a_re, a_im, log_dt, b_re, b_im, c_re, c_im):
    n, p = SSM_STATE, SSM_GROUP
    gg = a_re.shape[0] * a_re.shape[1]
    gb = PREP_GB
    lanes = SSM_LANES
    row = lambda a: a.reshape(gg, 1, n)
    ldt = jnp.broadcast_to(log_dt.reshape(gg, 1), (gg, n))
    bt = lambda b: jnp.transpose(b.reshape(gg, n, p), (0, 2, 1))
    rspec = pl.BlockSpec((gb, 1, n), lambda g: (g, 0, 0))
    pspec = pl.BlockSpec((gb, p, n), lambda g: (g, 0, 0))
    ospec = lambda a, b: pl.BlockSpec((gb, a, b), lambda g: (g, 0, 0))
    return pl.pallas_call(
        _ssm_prep_kernel,
        grid=(gg // gb,),
        in_specs=[rspec, rspec, rspec, pspec, pspec, pspec, pspec],
        out_specs=[ospec(lanes, lanes), ospec(lanes, 2 * n), ospec(lanes, 2 * n),
                   ospec(SUBLANES, 2 * n), ospec(SUBLANES, 2 * n)],
        out_shape=[
            jax.ShapeDtypeStruct((gg, lanes, lanes), BF16),
            jax.ShapeDtypeStruct((gg, lanes, 2 * n), BF16),
            jax.ShapeDtypeStruct((gg, lanes, 2 * n), BF16),
            jax.ShapeDtypeStruct((gg, SUBLANES, 2 * n), F32),
            jax.ShapeDtypeStruct((gg, SUBLANES, 2 * n), F32),
        ],
        scratch_shapes=[pltpu.VMEM(((SSM_T + 1) * p, 2 * n), F32), pltpu.VMEM((SSM_T * p, 2 * n), F32)],
        compiler_params=_cparams(("arbitrary",)),
        name="ssm_prep",
    )(row(a_re), row(a_im), row(ldt), c_re.reshape(gg, p, n), c_im.reshape(gg, p, n), bt(b_re), bt(b_im))


SSM_GB = LANES // SSM_GROUP
SSM_RB = BF16_ROWS


def _piece_transpose(xs, masks):
    n = len(xs)
    w = LANES // n
    rolled = []
    for k in range(n):
        wk = xs[k % n]
        for a in range(1, n):
            wk = jnp.where(masks[a], xs[(a + k) % n], wk)
        rolled.append(pltpu.roll(wk, w * k, axis=1) if k else wk)
    ys = []
    for a in range(n):
        ya = rolled[(0 - a) % n]
        for b in range(1, n):
            ya = jnp.where(masks[b], rolled[(b - a) % n], ya)
        ys.append(ya)
    return ys


def _ssm_kernel(*refs, nchunk, pad, n_side):
    u_ref, toep_ref, bpow_ref, cpow_ref, lama_ref, lamb_ref = refs[:6]
    side_in = refs[6:6 + n_side]
    y_ref = refs[6 + n_side]
    side_out = refs[7 + n_side:7 + 2 * n_side]
    up_ref, yp_ref, xs_ref = refs[7 + 2 * n_side:]
    _side_cast(side_in, side_out)
    rows = up_ref.shape[1]
    n2 = 2 * SSM_STATE
    t = SSM_T
    lane = lax.broadcasted_iota(jnp.int32, (1, LANES), 1)
    masks = [lane // SSM_GROUP == a for a in range(SSM_GB)]
    nhalf = t // SSM_GB

    def to_chunks(rb, carry):
        r = pl.multiple_of(rb * SSM_RB, SSM_RB)
        for hf in range(nhalf):
            xs = [u_ref[pl.ds(r * t + hf * SSM_GB + b, SSM_RB, stride=t), :] for b in range(SSM_GB)]
            ys = _piece_transpose(xs, masks)
            for a in range(SSM_GB):
                up_ref[a, pl.ds(r, SSM_RB), hf * LANES:(hf + 1) * LANES] = ys[a].astype(up_ref.dtype)
        return carry
    lax.fori_loop(0, rows // SSM_RB, to_chunks, 0, unroll=2)

    cidx = lax.broadcasted_iota(jnp.int32, (rows, 1), 0) % nchunk
    xs_ref[0:pad, :] = jnp.zeros((pad, n2), F32)
    for a in range(SSM_GB):
        u = up_ref[a]
        xs_ref[pad:pad + rows, :] = jnp.dot(u, bpow_ref[a], preferred_element_type=F32)
        for j in range(SSM_SCAN_STEPS):
            sh = 1 << j
            cur = xs_ref[pad:pad + rows, :]
            prev = jnp.where(cidx >= sh, xs_ref[pad - sh:pad - sh + rows, :], 0.0)
            la = lama_ref[a, j:j + 1, :]
            lb = lamb_ref[a, j:j + 1, :]
            xs_ref[pad:pad + rows, :] = cur + prev * la + pltpu.roll(prev, SSM_STATE, axis=1) * lb
        xprev = jnp.where(cidx >= 1, xs_ref[pad - 1:pad - 1 + rows, :], 0.0).astype(BF16)
        yp_ref[a] = (jnp.dot(u, toep_ref[a], preferred_element_type=F32)
                     + lax.dot_general(xprev, cpow_ref[a], (((1,), (1,)), ((), ())),
                                       preferred_element_type=F32))

    def to_tokens(rb, carry):
        r = pl.multiple_of(rb * SUBLANES, SUBLANES)
        for hf in range(nhalf):
            ys = [yp_ref[a, pl.ds(r, SUBLANES), hf * LANES:(hf + 1) * LANES] for a in range(SSM_GB)]
            xs = _piece_transpose(ys, masks)
            for b in range(SSM_GB):
                y_ref[pl.ds(r * t + hf * SSM_GB + b, SUBLANES, stride=t), :] = xs[b]
        return carry
    lax.fori_loop(0, rows // SUBLANES, to_tokens, 0, unroll=2)


def _ssm_scan(pm, toep, bpow, cpow, lama, lamb, layer, side_weights, *, nchunk):
    m = pm.shape[0]
    rows = m // SSM_T
    lanes = SSM_LANES
    n2 = 2 * SSM_STATE
    pad = 1 << (SSM_SCAN_STEPS - 1)
    nblk = W_BRANCH // LANES
    gspec = lambda a, b: pl.BlockSpec((SSM_GB, a, b), lambda i: (layer * nblk + i, 0, 0))
    s_in, s_out, s_shape = _side_cast_specs(side_weights, layer, nblk, lambda i: i)
    outs = pl.pallas_call(
        functools.partial(_ssm_kernel, nchunk=nchunk, pad=pad, n_side=len(side_weights)),
        grid=(nblk,),
        in_specs=[pl.BlockSpec((m, LANES), lambda i: (0, i)),
                  gspec(lanes, lanes), gspec(lanes, n2), gspec(lanes, n2),
                  gspec(SUBLANES, n2), gspec(SUBLANES, n2)] + s_in,
        out_specs=[pl.BlockSpec((m, LANES), lambda i: (0, i))] + s_out,
        out_shape=[jax.ShapeDtypeStruct((m, W_BRANCH), F32)] + s_shape,
        scratch_shapes=[pltpu.VMEM((SSM_GB, rows, lanes), BF16), pltpu.VMEM((SSM_GB, rows, lanes), F32),
                        pltpu.VMEM((pad + rows, n2), F32)],
        compiler_params=_cparams(("arbitrary",)),
        name="ssm_scan",
    )(pm, toep, bpow, cpow, lama, lamb, *side_weights)
    return outs[0], outs[1:]


def _ssm_post_kernel(yraw_ref, u_ref, d_ref, wglu_ref, o_ref):
    w = W_BRANCH
    y = yraw_ref[...] + d_ref[...] * u_ref[...]
    zg = jnp.dot(_gelu(y).astype(BF16), wglu_ref[...], preferred_element_type=F32)
    o_ref[...] = (zg[:, :w] * _sigmoid(zg[:, w:])).astype(o_ref.dtype)


def _ssm_post(yraw, u, d, wglu, *, tm=TM_POST):
    m, w = yraw.shape
    row = pl.BlockSpec((tm, w), lambda i: (i, 0))
    return pl.pallas_call(
        _ssm_post_kernel,
        grid=(m // tm,),
        in_specs=[row, row, pl.BlockSpec((1, w), lambda i: (0, 0)), pl.BlockSpec(wglu.shape, lambda i: (0, 0))],
        out_specs=row,
        out_shape=jax.ShapeDtypeStruct((m, w), BF16),
        compiler_params=_cparams(("parallel",)),
        name="ssm_post",
    )(yraw, u, d, wglu)


def _merge_kernel(*refs, n_side):
    nb = N_BRANCH
    h_ref = refs[0]
    ys = refs[1:1 + nb]
    wg = refs[1 + nb:1 + 2 * nb]
    wb = refs[1 + 2 * nb:1 + 3 * nb]
    side_in = refs[1 + 3 * nb:1 + 3 * nb + n_side]
    o_ref = refs[1 + 3 * nb + n_side]
    side_out = refs[2 + 3 * nb + n_side:2 + 3 * nb + 2 * n_side]
    h = h_ref[...]
    acc = None
    for k in range(nb):
        gate = jnp.dot(h, wg[k][...], preferred_element_type=F32)
        yb = jnp.dot(ys[k][...], wb[k][0], preferred_element_type=F32)
        term = yb * _sigmoid(gate)
        acc = term if acc is None else acc + term
    o_ref[...] = acc.astype(o_ref.dtype)
    _side_cast(side_in, side_out)


def _merge(h, ys, wgate, wbranch, side_weights, side_layer, *, tm=TM_MERGE, tn=TN_MERGE):
    m, d = h.shape
    w = W_BRANCH
    nj = d // tn
    j0 = OFF_GATE // tn
    gate_spec = lambda k: pl.BlockSpec((None, d, tn), lambda i, j: (0, 0, j0 + k * nj + j))
    br_spec = lambda k: pl.BlockSpec((1, w, tn), lambda i, j: (k, 0, j))
    s_in, s_out, s_shape = _side_cast_specs(side_weights, side_layer, (m // tm) * nj, lambda i, j: i * nj + j)
    outs = pl.pallas_call(
        functools.partial(_merge_kernel, n_side=len(side_weights)),
        grid=(m // tm, nj),
        in_specs=[pl.BlockSpec((tm, d), lambda i, j: (i, 0))]
        + [pl.BlockSpec((tm, w), lambda i, j: (i, 0))] * N_BRANCH
        + [gate_spec(k) for k in range(N_BRANCH)]
        + [br_spec(k) for k in range(N_BRANCH)]
        + s_in,
        out_specs=[pl.BlockSpec((tm, tn), lambda i, j: (i, j))] + s_out,
        out_shape=[jax.ShapeDtypeStruct((m, d), BF16)] + s_shape,
        compiler_params=_cparams(("arbitrary", "arbitrary")),
        name="gated_merge",
    )(h, *ys, *([wgate] * N_BRANCH), *([wbranch] * N_BRANCH), *side_weights)
    return outs[0], outs[1:]


def _out_kernel(m_ref, w_ref, x_ref, g_ref, gn_ref, o_ref, hn_ref):
    mix = jnp.dot(m_ref[...], w_ref[...], preferred_element_type=F32)
    x1 = x_ref[...] + _rms(mix, g_ref[...])
    o_ref[...] = x1
    hn_ref[...] = _rms(x1, gn_ref[...]).astype(hn_ref.dtype)


def _out_proj(merged, w_o, x2, g, g_next, layer, *, tm=TM_OUT):
    m, d = x2.shape
    row = pl.BlockSpec((tm, d), lambda i: (i, 0))
    vec = pl.BlockSpec((1, d), lambda i: (0, 0))
    return pl.pallas_call(
        _out_kernel,
        grid=(m // tm,),
        in_specs=[row, pl.BlockSpec((None, d, d), lambda i: (layer, 0, 0), pipeline_mode=pl.Buffered(1)),
                  row, vec, vec],
        out_specs=[row, row],
        out_shape=[jax.ShapeDtypeStruct((m, d), F32), jax.ShapeDtypeStruct((m, d), BF16)],
        compiler_params=_cparams(("parallel",)),
        name="out_proj",
    )(merged, w_o, x2, g, g_next)


def _ffn_kernel(*refs, rows, n_side):
    x_ref, h_ref, w1_ref, w2_ref, gpost_ref = refs[:5]
    side_in = refs[5:5 + n_side]
    o_ref = refs[5 + n_side]
    side_out = refs[6 + n_side:6 + 2 * n_side]
    j = pl.program_id(1)

    @pl.when(j == 0)
    def _():
        o_ref[...] = jnp.zeros(o_ref.shape, o_ref.dtype)

    a = jnp.dot(h_ref[...], w1_ref[...], preferred_element_type=F32)
    a = jnp.maximum(a, 0.0)
    a = (a * a).astype(BF16)
    nw = 512
    for c in range(o_ref.shape[1] // nw):
        o_ref[:, c * nw:(c + 1) * nw] += jnp.dot(a, w2_ref[:, c * nw:(c + 1) * nw],
                                                 preferred_element_type=F32)
    _side_cast(side_in, side_out)

    @pl.when(j == pl.num_programs(1) - 1)
    def _():
        def body(i, carry):
            r = pl.multiple_of(i * rows, rows)
            o_ref[pl.ds(r, rows), :] = (x_ref[pl.ds(r, rows), :]
                                        + _rms(o_ref[pl.ds(r, rows), :], gpost_ref[...]))
            return carry
        lax.fori_loop(0, x_ref.shape[0] // rows, body, 0, unroll=2)


def _ffn(x2, h2, w1, w2, gpost, side_weights, side_layer, *, tm=TM_FFN, tf=TF_FFN):
    m, d = x2.shape
    f = w1.shape[2]
    nj = f // tf
    s_in, s_out, s_shape = _side_cast_specs(side_weights, side_layer, (m // tm) * nj, lambda i, j: i * nj + j)
    outs = pl.pallas_call(
        functools.partial(_ffn_kernel, rows=NORM_ROWS, n_side=len(side_weights)),
        grid=(m // tm, nj),
        in_specs=[
            pl.BlockSpec((tm, d), lambda i, j: (i, 0)),
            pl.BlockSpec((tm, d), lambda i, j: (i, 0)),
            pl.BlockSpec((None, d, tf), lambda i, j: (0, 0, j)),
            pl.BlockSpec((None, tf, d), lambda i, j: (0, j, 0)),
            pl.BlockSpec((1, d), lambda i, j: (0, 0)),
        ] + s_in,
        out_specs=[pl.BlockSpec((tm, d), lambda i, j: (i, 0))] + s_out,
        out_shape=[jax.ShapeDtypeStruct((m, d), F32)] + s_shape,
        compiler_params=_cparams(("arbitrary", "arbitrary")),
        name="ffn",
    )(x2, h2, w1, w2, gpost, *side_weights)
    return outs[0], outs[1:]


def kernel(x, g_pre_mix, w_in, ssm_a_re, ssm_a_im, ssm_log_dt, ssm_b_re, ssm_b_im, ssm_c_re, ssm_c_im,
           ssm_d, ssm_w_glu, pool_w, pool_scale, conv_w, conv_b, conv_ln_g, conv_ln_b, gmlp_ln_g,
           gmlp_ln_b, gmlp_ws, gmlp_bs, w_branch, w_o, g_post_mix, g_pre_mlp, w_ff1, w_ff2, g_post_mlp):
    nb, seq, d = x.shape
    m = nb * seq
    depth = w_in.shape[0]
    nchunk = seq // SSM_T
    assert d == D_MODEL and nchunk == 1 << SSM_SCAN_STEPS

    toep, bpow, cpow, lama, lamb = _ssm_prep(ssm_a_re, ssm_a_im, ssm_log_dt, ssm_b_re, ssm_b_im,
                                             ssm_c_re, ssm_c_im)

    w_br2 = w_branch.reshape(depth, N_BRANCH * W_BRANCH, d)
    w_in_b = _cast_bf16(w_in, 0, CAST_ROWS)

    row = lambda a: a.reshape(1, -1)
    x2 = x.reshape(m, d)
    for l in range(depth):
        h, u, y_pool, y_conv, y_gmlp = _proj_mix(
            x2, row(g_pre_mix[l]), w_in_b, 0, pool_w[l].astype(BF16), row(pool_scale[l]),
            jnp.broadcast_to(conv_w[l][:, None, :], (CONV_WIDTH, SUBLANES, W_BRANCH)),
            row(conv_b[l]), row(conv_ln_g[l]), row(conv_ln_b[l]), row(gmlp_ln_g[l]), row(gmlp_ln_b[l]),
            gmlp_ws[l], jnp.transpose(gmlp_bs[l]), seq=seq)
        yraw, (w_br_b, w_o_b) = _ssm_scan(u, toep, bpow, cpow, lama, lamb, l, (w_br2, w_o), nchunk=nchunk)
        y_ssm = _ssm_post(yraw, u, row(ssm_d[l]), ssm_w_glu[l].astype(BF16))
        merged, (w_ff1_b, w_ff2_b) = _merge(h, (y_ssm, y_pool, y_conv, y_gmlp), w_in_b,
                                            w_br_b.reshape(N_BRANCH, W_BRANCH, d), (w_ff1, w_ff2), l)
        x2, h2 = _out_proj(merged, w_o_b, x2, row(g_post_mix[l]), row(g_pre_mlp[l]), 0)
        nxt = (w_in,) if l + 1 < depth else ()
        x2, nxt_b = _ffn(x2, h2, w_ff1_b, w_ff2_b, row(g_post_mlp[l]), nxt, l + 1)
        if nxt:
            (w_in_b,) = nxt_b
    return x2.reshape(nb, seq, d)
```

```python
import functools

import jax
import jax.numpy as jnp
from jax import lax
from jax.experimental import pallas as pl
from jax.experimental.pallas import tpu as pltpu

D_MODEL = 2048
N_BRANCH = 4
W_BRANCH = D_MODEL // 4
SSM_GROUP = 16
SSM_STATE = 64
POOL_WINDOWS = (2, 4, 8, 16)
POOL_GW = W_BRANCH // len(POOL_WINDOWS)
CONV_WIDTH = 31
GMLP_CHUNK = 128
GMLP_HEADS = 4
GMLP_HD = W_BRANCH // GMLP_HEADS
EPS = 1e-6

LANES = 128
SUBLANES = 8
BF16_ROWS = 2 * SUBLANES
V7X_VMEM_BYTES = 64 * 1024 * 1024

OFF_SSM = 0
OFF_POOL = OFF_SSM + W_BRANCH
OFF_CONV = OFF_POOL + W_BRANCH
OFF_GMLP = OFF_CONV + 2 * W_BRANCH
OFF_GATE = OFF_GMLP + 2 * W_BRANCH

SSM_T = 16
SSM_LANES = SSM_T * SSM_GROUP
SSM_SCAN_STEPS = 7
CONV_HALO = 32
POOL_HALO = SUBLANES * len(POOL_WINDOWS)
assert all(win == 2 << k for k, win in enumerate(POOL_WINDOWS))
CONV_RB = 64

VMEM_LIMIT = V7X_VMEM_BYTES - 8 * 1024 * 1024
TM_PROJ = 512
TM_MERGE, TN_MERGE = 1024, 256
TM_OUT = 512
TM_FFN, TF_FFN = 1024, 512
TM_POST = 512
CAST_ROWS = 256
NORM_ROWS = 128
BF16 = jnp.bfloat16
F32 = jnp.float32


def _cparams(sem):
    return pltpu.CompilerParams(dimension_semantics=sem, vmem_limit_bytes=VMEM_LIMIT)


def _sigmoid(x):
    return 0.5 * (1.0 + jnp.tanh(0.5 * x))


def _gelu(x):
    c = 0.7978845608028654
    return 0.5 * x * (1.0 + jnp.tanh(c * (x + 0.044715 * (x * x * x))))


def _rms(x, g):
    ms = jnp.mean(x * x, axis=-1, keepdims=True)
    return x * lax.rsqrt(ms + EPS) * g


def _layer_norm(x, g, b):
    mu = jnp.mean(x, axis=-1, keepdims=True)
    xc = x - mu
    var = jnp.mean(xc * xc, axis=-1, keepdims=True)
    return xc * lax.rsqrt(var + EPS) * g + b


def _cast_kernel(w_ref, o_ref):
    o_ref[...] = w_ref[...].astype(o_ref.dtype)


def _cast_bf16(w, layer, rows):
    _, r, c = w.shape
    return pl.pallas_call(
        _cast_kernel,
        grid=(r // rows,),
        in_specs=[pl.BlockSpec((1, rows, c), lambda i: (layer, i, 0))],
        out_specs=pl.BlockSpec((1, rows, c), lambda i: (0, i, 0)),
        out_shape=jax.ShapeDtypeStruct((1, r, c), BF16),
        compiler_params=_cparams(("parallel",)),
        name="cast_bf16",
    )(w)


def _side_cast_specs(weights, layer, nsteps, step_of):
    in_specs, out_specs, out_shapes = [], [], []
    for w in weights:
        _, r, c = w.shape
        assert r % nsteps == 0 and (r // nsteps) % BF16_ROWS == 0
        rows = r // nsteps
        in_specs.append(pl.BlockSpec((None, rows, c), lambda *ids: (layer, step_of(*ids), 0)))
        out_specs.append(pl.BlockSpec((None, rows, c), lambda *ids: (0, step_of(*ids), 0)))
        out_shapes.append(jax.ShapeDtypeStruct((1, r, c), BF16))
    return in_specs, out_specs, out_shapes


def _side_cast(in_refs, out_refs):
    for wi, wo in zip(in_refs, out_refs):
        wo[...] = wi[...].astype(wo.dtype)


def _proj_mix_kernel(x_ref, g_ref, w_ref, poolw_ref, pscale_ref, convw_ref, convb_ref, clg_ref, clb_ref,
                     glg_ref, glb_ref, ws_ref, bst_ref,
                     h_ref, u_ref, ypool_ref, yconv_ref, ygmlp_ref, vbuf, pbuf, sbuf, qbuf, ugbuf, vgbuf,
                     *, rows, tiles_per_seq):
    c = pl.program_id(0) % tiles_per_seq
    tc = x_ref.shape[0]
    w = W_BRANCH

    @pl.when(c == 0)
    def _():
        vbuf[0:CONV_HALO, :] = jnp.zeros((CONV_HALO, w), F32)
        pbuf[0:POOL_HALO, :] = jnp.zeros((POOL_HALO, w), F32)

    def body(i, carry):
        r = pl.multiple_of(i * rows, rows)
        h_ref[pl.ds(r, rows), :] = _rms(x_ref[pl.ds(r, rows), :], g_ref[...]).astype(h_ref.dtype)
        return carry
    lax.fori_loop(0, tc // rows, body, 0, unroll=2)

    h = h_ref[...]
    proj = lambda off, width: jnp.dot(h, w_ref[:, off:off + width], preferred_element_type=F32)
    gm = proj(OFF_GMLP, 2 * w)
    ugbuf[...] = _gelu(gm[:, :w])
    vgbuf[...] = _layer_norm(_gelu(gm[:, w:]), glg_ref[...], glb_ref[...]).astype(vgbuf.dtype)
    cv = proj(OFF_CONV, 2 * w)
    vbuf[CONV_HALO:CONV_HALO + tc, :] = cv[:, :w] * _sigmoid(cv[:, w:])
    pbuf[POOL_HALO:POOL_HALO + tc, :] = proj(OFF_POOL, w)
    u_ref[...] = proj(OFF_SSM, w)

    src = pbuf
    for k in range(len(POOL_WINDOWS)):
        lo, sh, c0 = SUBLANES * (k + 1), 1 << k, k * POOL_GW
        n = POOL_HALO + tc - lo
        a_hi = src[lo:lo + n, c0:] if k == 0 else src[k - 1, lo:lo + n, c0:]
        a_lo = src[lo - sh:lo - sh + n, c0:] if k == 0 else src[k - 1, lo - sh:lo - sh + n, c0:]
        qbuf[k, lo:lo + n, c0:] = a_hi + a_lo
        src = qbuf
    t = c * tc + lax.broadcasted_iota(jnp.int32, (tc, 1), 0)
    for gi, win in enumerate(POOL_WINDOWS):
        cols = slice(gi * POOL_GW, (gi + 1) * POOL_GW)
        tok = pbuf[POOL_HALO:POOL_HALO + tc, cols]
        s = qbuf[gi, POOL_HALO:POOL_HALO + tc, cols]
        count = jnp.minimum(t + 1, win).astype(F32)
        pooled = s / count - tok
        mixed = jnp.dot(pooled.astype(BF16), poolw_ref[gi], preferred_element_type=F32)
        ypool_ref[:, cols] = (mixed * pscale_ref[:, cols]).astype(ypool_ref.dtype)
    pbuf[0:POOL_HALO, :] = pbuf[tc:tc + POOL_HALO, :]

    sl = SUBLANES
    span = CONV_HALO - sl
    for b in range(1, sl):
        sbuf[b - 1] = vbuf[sl - b:sl - b + span + tc, :]
    for rb in range(tc // CONV_RB):
        r0 = rb * CONV_RB
        acc = jnp.broadcast_to(convb_ref[...], (CONV_RB, w))
        for b in range(sl):
            for a in range((CONV_WIDTH - 1 - b) // sl + 1):
                k = CONV_WIDTH - 1 - (sl * a + b)
                lo = span - sl * a + r0
                tap = sbuf[b - 1, lo:lo + CONV_RB, :] if b else vbuf[sl + lo:sl + lo + CONV_RB, :]
                wk = convw_ref[k]
                acc = acc + (tap.reshape(CONV_RB // sl, sl, w) * wk[None]).reshape(CONV_RB, w)
        yn = _layer_norm(acc, clg_ref[...], clb_ref[...])
        yconv_ref[r0:r0 + CONV_RB, :] = (yn * _sigmoid(yn)).astype(yconv_ref.dtype)
    vbuf[0:CONV_HALO, :] = vbuf[tc:tc + CONV_HALO, :]

    ri = lax.broadcasted_iota(jnp.int32, (GMLP_CHUNK, GMLP_CHUNK), 0)
    ci = lax.broadcasted_iota(jnp.int32, (GMLP_CHUNK, GMLP_CHUNK), 1)
    causal = ri >= ci
    for hd in range(GMLP_HEADS):
        wsm = jnp.where(causal, ws_ref[hd], 0.0).astype(BF16)
        cols = slice(hd * GMLP_HD, (hd + 1) * GMLP_HD)
        for ch in range(tc // GMLP_CHUNK):
            rws = slice(ch * GMLP_CHUNK, (ch + 1) * GMLP_CHUNK)
            sv = jnp.dot(wsm, vgbuf[rws, cols], preferred_element_type=F32) + bst_ref[:, hd:hd + 1]
            ygmlp_ref[rws, cols] = (ugbuf[rws, cols] * sv).astype(ygmlp_ref.dtype)


def _proj_mix(x2, g, w_in, layer, poolw, pscale, convw, convb, clg, clb, glg, glb, ws, bst, *, seq, tm=TM_PROJ):
    m, d = x2.shape
    n = OFF_GATE
    w = W_BRANCH
    assert seq % tm == 0 and tm % GMLP_CHUNK == 0 and tm % CONV_RB == 0
    row = lambda width: pl.BlockSpec((tm, width), lambda i: (i, 0))
    full = lambda a: pl.BlockSpec(a.shape, lambda i: (0,) * a.ndim)
    args = (poolw, pscale, convw, convb, clg, clb, glg, glb, ws, bst)
    return pl.pallas_call(
        functools.partial(_proj_mix_kernel, rows=NORM_ROWS, tiles_per_seq=seq // tm),
        grid=(m // tm,),
        in_specs=[row(d), full(g),
                  pl.BlockSpec((None, d, n), lambda i: (layer, 0, 0), pipeline_mode=pl.Buffered(1))]
        + [full(a) for a in args],
        out_specs=[row(d), row(w), row(w), row(w), row(w)],
        out_shape=[jax.ShapeDtypeStruct((m, d), BF16), jax.ShapeDtypeStruct((m, w), F32)]
        + [jax.ShapeDtypeStruct((m, w), BF16)] * 3,
        scratch_shapes=[pltpu.VMEM((CONV_HALO + tm, w), F32), pltpu.VMEM((POOL_HALO + tm, w), F32),
                        pltpu.VMEM((SUBLANES - 1, CONV_HALO - SUBLANES + tm, w), F32),
                        pltpu.VMEM((len(POOL_WINDOWS), POOL_HALO + tm, w), F32),
                        pltpu.VMEM((tm, w), F32), pltpu.VMEM((tm, w), BF16)],
        compiler_params=_cparams(("arbitrary",)),
        name="proj_mix",
    )(x2, g, w_in, *args)


PREP_GB = 8


def _ssm_prep_kernel(ar_ref, ai_ref, ldt_ref, cre_ref, cim_ref, bre_ref, bim_ref,
                     toep_ref, bpow_ref, cpow_ref, lama_ref, lamb_ref, cl_s, bl_s):
    n = SSM_STATE
    p = SSM_GROUP
    lanes = SSM_LANES
    lane = lax.broadcasted_iota(jnp.int32, (1, lanes), 1)
    for e in range(PREP_GB):
        ar = ar_ref[e]
        ai = ai_ref[e]
        dt = jnp.exp(ldt_ref[e])
        mag = jnp.exp(ar * dt)
        ang = ai * dt
        lr = mag * jnp.cos(ang)
        li = mag * jnp.sin(ang)
        den = ar * ar + ai * ai
        fr = ((lr - 1.0) * ar + li * ai) / den
        fi = (li * ar - (lr - 1.0) * ai) / den
        cre = cre_ref[e]
        cim = cim_ref[e]
        br = bre_ref[e]
        bi = bim_ref[e]
        bbr = fr * br - fi * bi
        bbi = fr * bi + fi * br
        pr = jnp.ones_like(lr)
        pi = jnp.zeros_like(lr)
        for k in range(SSM_T + 1):
            cl_s[k * p:(k + 1) * p, 0:n] = cre * pr - cim * pi
            cl_s[k * p:(k + 1) * p, n:2 * n] = -(cre * pi + cim * pr)
            if k < SSM_T:
                s = SSM_T - 1 - k
                bl_s[s * p:(s + 1) * p, 0:n] = pr * bbr - pi * bbi
                bl_s[s * p:(s + 1) * p, n:2 * n] = pr * bbi + pi * bbr
                pr, pi = pr * lr - pi * li, pr * li + pi * lr
        qr, qi = pr, pi
        for j in range(SUBLANES):
            if j < SSM_SCAN_STEPS:
                lama_ref[e, j:j + 1, 0:n] = qr
                lama_ref[e, j:j + 1, n:2 * n] = qr
                lamb_ref[e, j:j + 1, 0:n] = -qi
                lamb_ref[e, j:j + 1, n:2 * n] = qi
                qr, qi = qr * qr - qi * qi, 2.0 * (qr * qi)
            else:
                lama_ref[e, j:j + 1, :] = jnp.zeros((1, 2 * n), F32)
                lamb_ref[e, j:j + 1, :] = jnp.zeros((1, 2 * n), F32)
        cpow_ref[e] = cl_s[p:(SSM_T + 1) * p, :].astype(cpow_ref.dtype)
        bpow_ref[e] = bl_s[...].astype(bpow_ref.dtype)
        bbcat = bl_s[(SSM_T - 1) * p:SSM_T * p, :]
        kt = lax.dot_general(bbcat, cl_s[0:SSM_T * p, :], (((1,), (1,)), ((), ())),
                             preferred_element_type=F32, precision=lax.Precision.HIGHEST)
        for s in range(SSM_T):
            blk = pltpu.roll(kt, s * p, axis=1) if s else kt
            toep_ref[e, s * p:(s + 1) * p, :] = jnp.where(lane >= s * p, blk, 0.0).astype(toep_ref.dtype)


def _ssm_prep(a_re, a_im, log_dt, b_re, b_im, c_re, c_im):
    n, p = SSM_STATE, SSM_GROUP
    gg = a_re.shape[0] * a_re.shape[1]
    gb = PREP_GB
    lanes = SSM_LANES
    row = lambda a: a.reshape(gg, 1, n)
    ldt = jnp.broadcast_to(log_dt.reshape(gg, 1), (gg, n))
    bt = lambda b: jnp.transpose(b.reshape(gg, n, p), (0, 2, 1))
    rspec = pl.BlockSpec((gb, 1, n), lambda g: (g, 0, 0))
    pspec = pl.BlockSpec((gb, p, n), lambda g: (g, 0, 0))
    ospec = lambda a, b: pl.BlockSpec((gb, a, b), lambda g: (g, 0, 0))
    return pl.pallas_call(
        _ssm_prep_kernel,
        grid=(gg // gb,),
        in_specs=[rspec, rspec, rspec, pspec, pspec, pspec, pspec],
        out_specs=[ospec(lanes, lanes), ospec(lanes, 2 * n), ospec(lanes, 2 * n),
                   ospec(SUBLANES, 2 * n), ospec(SUBLANES, 2 * n)],
        out_shape=[
            jax.ShapeDtypeStruct((gg, lanes, lanes), BF16),
            jax.ShapeDtypeStruct((gg, lanes, 2 * n), BF16),
            jax.ShapeDtypeStruct((gg, lanes, 2 * n), BF16),
            jax.ShapeDtypeStruct((gg, SUBLANES, 2 * n), F32),
            jax.ShapeDtypeStruct((gg, SUBLANES, 2 * n), F32),
        ],
        scratch_shapes=[pltpu.VMEM(((SSM_T + 1) * p, 2 * n), F32), pltpu.VMEM((SSM_T * p, 2 * n), F32)],
        compiler_params=_cparams(("arbitrary",)),
        name="ssm_prep",
    )(row(a_re), row(a_im), row(ldt), c_re.reshape(gg, p, n), c_im.reshape(gg, p, n), bt(b_re), bt(b_im))


SSM_GB = LANES // SSM_GROUP
SSM_RB = BF16_ROWS


def _piece_transpose(xs, masks):
    n = len(xs)
    w = LANES // n
    rolled = []
    for k in range(n):
        wk = xs[k % n]
        for a in range(1, n):
            wk = jnp.where(masks[a], xs[(a + k) % n], wk)
        rolled.append(pltpu.roll(wk, w * k, axis=1) if k else wk)
    ys = []
    for a in range(n):
        ya = rolled[(0 - a) % n]
        for b in range(1, n):
            ya = jnp.where(masks[b], rolled[(b - a) % n], ya)
        ys.append(ya)
    return ys


def _ssm_kernel(*refs, nchunk, pad, n_side):
    u_ref, toep_ref, bpow_ref, cpow_ref, lama_ref, lamb_ref = refs[:6]
    side_in = refs[6:6 + n_side]
    y_ref = refs[6 + n_side]
    side_out = refs[7 + n_side:7 + 2 * n_side]
    up_ref, yp_ref, xr_ref, xi_ref = refs[7 + 2 * n_side:]
    _side_cast(side_in, side_out)
    rows = up_ref.shape[1]
    n2 = 2 * SSM_STATE
    t = SSM_T
    lane = lax.broadcasted_iota(jnp.int32, (1, LANES), 1)
    masks = [lane // SSM_GROUP == a for a in range(SSM_GB)]
    nhalf = t // SSM_GB

    def to_chunks(rb, carry):
        r = pl.multiple_of(rb * SSM_RB, SSM_RB)
        for hf in range(nhalf):
            xs = [u_ref[pl.ds(r * t + hf * SSM_GB + b, SSM_RB, stride=t), :] for b in range(SSM_GB)]
            ys = _piece_transpose(xs, masks)
            for a in range(SSM_GB):
                up_ref[a, pl.ds(r, SSM_RB), hf * LANES:(hf + 1) * LANES] = ys[a].astype(up_ref.dtype)
        return carry
    lax.fori_loop(0, rows // SSM_RB, to_chunks, 0, unroll=2)

    cidx = lax.broadcasted_iota(jnp.int32, (rows, 1), 0) % nchunk
    lo = lane < SSM_STATE
    swap = lambda v: pltpu.roll(v, SSM_STATE, axis=1)
    nt_dot = lambda x, wt: lax.dot_general(x, wt, (((1,), (1,)), ((), ())), preferred_element_type=F32)
    xr_ref[0:pad, :] = jnp.zeros((pad, n2), F32)
    xi_ref[0:pad, :] = jnp.zeros((pad, n2), F32)
    for a in range(0, SSM_GB, 2):
        b = a + 1
        ua, ub = up_ref[a], up_ref[b]
        sa = jnp.dot(ua, bpow_ref[a], preferred_element_type=F32)
        sb = jnp.dot(ub, bpow_ref[b], preferred_element_type=F32)
        xr_ref[pad:pad + rows, :] = jnp.where(lo, sa, swap(sb))
        xi_ref[pad:pad + rows, :] = jnp.where(lo, swap(sa), sb)
        for j in range(SSM_SCAN_STEPS):
            sh = 1 << j
            lr = jnp.where(lo, lama_ref[a, j:j + 1, :], lama_ref[b, j:j + 1, :])
            li = jnp.where(lo, -lamb_ref[a, j:j + 1, :], lamb_ref[b, j:j + 1, :])
            pr = jnp.where(cidx >= sh, xr_ref[pad - sh:pad - sh + rows, :], 0.0)
            pi = jnp.where(cidx >= sh, xi_ref[pad - sh:pad - sh + rows, :], 0.0)
            cr = xr_ref[pad:pad + rows, :]
            ci = xi_ref[pad:pad + rows, :]
            xr_ref[pad:pad + rows, :] = cr + pr * lr - pi * li
            xi_ref[pad:pad + rows, :] = ci + pi * lr + pr * li
        pr = jnp.where(cidx >= 1, xr_ref[pad - 1:pad - 1 + rows, :], 0.0)
        pi = jnp.where(cidx >= 1, xi_ref[pad - 1:pad - 1 + rows, :], 0.0)
        xa = jnp.where(lo, pr, swap(pi)).astype(BF16)
        xb = jnp.where(lo, swap(pr), pi).astype(BF16)
        yp_ref[a] = jnp.dot(ua, toep_ref[a], preferred_element_type=F32) + nt_dot(xa, cpow_ref[a])
        yp_ref[b] = jnp.dot(ub, toep_ref[b], preferred_element_type=F32) + nt_dot(xb, cpow_ref[b])

    def to_tokens(rb, carry):
        r = pl.multiple_of(rb * SUBLANES, SUBLANES)
        for hf in range(nhalf):
            ys = [yp_ref[a, pl.ds(r, SUBLANES), hf * LANES:(hf + 1) * LANES] for a in range(SSM_GB)]
            xs = _piece_transpose(ys, masks)
            for b in range(SSM_GB):
                y_ref[pl.ds(r * t + hf * SSM_GB + b, SUBLANES, stride=t), :] = xs[b]
        return carry
    lax.fori_loop(0, rows // SUBLANES, to_tokens, 0, unroll=2)


def _ssm_scan(pm, toep, bpow, cpow, lama, lamb, layer, side_weights, *, nchunk):
    m = pm.shape[0]
    rows = m // SSM_T
    lanes = SSM_LANES
    n2 = 2 * SSM_STATE
    pad = 1 << (SSM_SCAN_STEPS - 1)
    nblk = W_BRANCH // LANES
    gspec = lambda a, b: pl.BlockSpec((SSM_GB, a, b), lambda i: (layer * nblk + i, 0, 0))
    s_in, s_out, s_shape = _side_cast_specs(side_weights, layer, nblk, lambda i: i)
    outs = pl.pallas_call(
        functools.partial(_ssm_kernel, nchunk=nchunk, pad=pad, n_side=len(side_weights)),
        grid=(nblk,),
        in_specs=[pl.BlockSpec((m, LANES), lambda i: (0, i)),
                  gspec(lanes, lanes), gspec(lanes, n2), gspec(lanes, n2),
                  gspec(SUBLANES, n2), gspec(SUBLANES, n2)] + s_in,
        out_specs=[pl.BlockSpec((m, LANES), lambda i: (0, i))] + s_out,
        out_shape=[jax.ShapeDtypeStruct((m, W_BRANCH), F32)] + s_shape,
        scratch_shapes=[pltpu.VMEM((SSM_GB, rows, lanes), BF16), pltpu.VMEM((SSM_GB, rows, lanes), F32),
                        pltpu.VMEM((pad + rows, n2), F32), pltpu.VMEM((pad + rows, n2), F32)],
        compiler_params=_cparams(("arbitrary",)),
        name="ssm_scan",
    )(pm, toep, bpow, cpow, lama, lamb, *side_weights)
    return outs[0], outs[1:]


def _ssm_post_kernel(yraw_ref, u_ref, d_ref, wglu_ref, o_ref):
    w = W_BRANCH
    y = yraw_ref[...] + d_ref[...] * u_ref[...]
    zg = jnp.dot(_gelu(y).astype(BF16), wglu_ref[...], preferred_element_type=F32)
    o_ref[...] = (zg[:, :w] * _sigmoid(zg[:, w:])).astype(o_ref.dtype)


def _ssm_post(yraw, u, d, wglu, *, tm=TM_POST):
    m, w = yraw.shape
    row = pl.BlockSpec((tm, w), lambda i: (i, 0))
    return pl.pallas_call(
        _ssm_post_kernel,
        grid=(m // tm,),
        in_specs=[row, row, pl.BlockSpec((1, w), lambda i: (0, 0)), pl.BlockSpec(wglu.shape, lambda i: (0, 0))],
        out_specs=row,
        out_shape=jax.ShapeDtypeStruct((m, w), BF16),
        compiler_params=_cparams(("parallel",)),
        name="ssm_post",
    )(yraw, u, d, wglu)


def _merge_kernel(*refs, n_side):
    nb = N_BRANCH
    h_ref = refs[0]
    ys = refs[1:1 + nb]
    wg = refs[1 + nb:1 + 2 * nb]
    wb = refs[1 + 2 * nb:1 + 3 * nb]
    side_in = refs[1 + 3 * nb:1 + 3 * nb + n_side]
    o_ref = refs[1 + 3 * nb + n_side]
    side_out = refs[2 + 3 * nb + n_side:2 + 3 * nb + 2 * n_side]
    h = h_ref[...]
    acc = None
    for k in range(nb):
        gate = jnp.dot(h, wg[k][...], preferred_element_type=F32)
        yb = jnp.dot(ys[k][...], wb[k][0], preferred_element_type=F32)
        term = yb * _sigmoid(gate)
        acc = term if acc is None else acc + term
    o_ref[...] = acc.astype(o_ref.dtype)
    _side_cast(side_in, side_out)


def _merge(h, ys, wgate, wbranch, side_weights, side_layer, *, tm=TM_MERGE, tn=TN_MERGE):
    m, d = h.shape
    w = W_BRANCH
    nj = d // tn
    j0 = OFF_GATE // tn
    gate_spec = lambda k: pl.BlockSpec((None, d, tn), lambda i, j: (0, 0, j0 + k * nj + j))
    br_spec = lambda k: pl.BlockSpec((1, w, tn), lambda i, j: (k, 0, j))
    s_in, s_out, s_shape = _side_cast_specs(side_weights, side_layer, (m // tm) * nj, lambda i, j: i * nj + j)
    outs = pl.pallas_call(
        functools.partial(_merge_kernel, n_side=len(side_weights)),
        grid=(m // tm, nj),
        in_specs=[pl.BlockSpec((tm, d), lambda i, j: (i, 0))]
        + [pl.BlockSpec((tm, w), lambda i, j: (i, 0))] * N_BRANCH
        + [gate_spec(k) for k in range(N_BRANCH)]
        + [br_spec(k) for k in range(N_BRANCH)]
        + s_in,
        out_specs=[pl.BlockSpec((tm, tn), lambda i, j: (i, j))] + s_out,
        out_shape=[jax.ShapeDtypeStruct((m, d), BF16)] + s_shape,
        compiler_params=_cparams(("arbitrary", "arbitrary")),
        name="gated_merge",
    )(h, *ys, *([wgate] * N_BRANCH), *([wbranch] * N_BRANCH), *side_weights)
    return outs[0], outs[1:]


def _out_kernel(m_ref, w_ref, x_ref, g_ref, gn_ref, o_ref, hn_ref):
    mix = jnp.dot(m_ref[...], w_ref[...], preferred_element_type=F32)
    x1 = x_ref[...] + _rms(mix, g_ref[...])
    o_ref[...] = x1
    hn_ref[...] = _rms(x1, gn_ref[...]).astype(hn_ref.dtype)


def _out_proj(merged, w_o, x2, g, g_next, layer, *, tm=TM_OUT):
    m, d = x2.shape
    row = pl.BlockSpec((tm, d), lambda i: (i, 0))
    vec = pl.BlockSpec((1, d), lambda i: (0, 0))
    return pl.pallas_call(
        _out_kernel,
        grid=(m // tm,),
        in_specs=[row, pl.BlockSpec((None, d, d), lambda i: (layer, 0, 0), pipeline_mode=pl.Buffered(1)),
                  row, vec, vec],
        out_specs=[row, row],
        out_shape=[jax.ShapeDtypeStruct((m, d), F32), jax.ShapeDtypeStruct((m, d), BF16)],
        compiler_params=_cparams(("parallel",)),
        name="out_proj",
    )(merged, w_o, x2, g, g_next)


def _ffn_kernel(*refs, rows, n_side):
    x_ref, h_ref, w1_ref, w2_ref, gpost_ref = refs[:5]
    side_in = refs[5:5 + n_side]
    o_ref = refs[5 + n_side]
    side_out = refs[6 + n_side:6 + 2 * n_side]
    j = pl.program_id(1)

    @pl.when(j == 0)
    def _():
        o_ref[...] = jnp.zeros(o_ref.shape, o_ref.dtype)

    a = jnp.dot(h_ref[...], w1_ref[...], preferred_element_type=F32)
    a = jnp.maximum(a, 0.0)
    a = (a * a).astype(BF16)
    nw = 512
    for c in range(o_ref.shape[1] // nw):
        o_ref[:, c * nw:(c + 1) * nw] += jnp.dot(a, w2_ref[:, c * nw:(c + 1) * nw],
                                                 preferred_element_type=F32)
    _side_cast(side_in, side_out)

    @pl.when(j == pl.num_programs(1) - 1)
    def _():
        def body(i, carry):
            r = pl.multiple_of(i * rows, rows)
            o_ref[pl.ds(r, rows), :] = (x_ref[pl.ds(r, rows), :]
                                        + _rms(o_ref[pl.ds(r, rows), :], gpost_ref[...]))
            return carry
        lax.fori_loop(0, x_ref.shape[0] // rows, body, 0, unroll=2)


def _ffn(x2, h2, w1, w2, gpost, side_weights, side_layer, *, tm=TM_FFN, tf=TF_FFN):
    m, d = x2.shape
    f = w1.shape[2]
    nj = f // tf
    s_in, s_out, s_shape = _side_cast_specs(side_weights, side_layer, (m // tm) * nj, lambda i, j: i * nj + j)
    outs = pl.pallas_call(
        functools.partial(_ffn_kernel, rows=NORM_ROWS, n_side=len(side_weights)),
        grid=(m // tm, nj),
        in_specs=[
            pl.BlockSpec((tm, d), lambda i, j: (i, 0)),
            pl.BlockSpec((tm, d), lambda i, j: (i, 0)),
            pl.BlockSpec((None, d, tf), lambda i, j: (0, 0, j)),
            pl.BlockSpec((None, tf, d), lambda i, j: (0, j, 0)),
            pl.BlockSpec((1, d), lambda i, j: (0, 0)),
        ] + s_in,
        out_specs=[pl.BlockSpec((tm, d), lambda i, j: (i, 0))] + s_out,
        out_shape=[jax.ShapeDtypeStruct((m, d), F32)] + s_shape,
        compiler_params=_cparams(("arbitrary", "arbitrary")),
        name="ffn",
    )(x2, h2, w1, w2, gpost, *side_weights)
    return outs[0], outs[1:]


def kernel(x, g_pre_mix, w_in, ssm_a_re, ssm_a_im, ssm_log_dt, ssm_b_re, ssm_b_im, ssm_c_re, ssm_c_im,
           ssm_d, ssm_w_glu, pool_w, pool_scale, conv_w, conv_b, conv_ln_g, conv_ln_b, gmlp_ln_g,
           gmlp_ln_b, gmlp_ws, gmlp_bs, w_branch, w_o, g_post_mix, g_pre_mlp, w_ff1, w_ff2, g_post_mlp):
    nb, seq, d = x.shape
    m = nb * seq
    depth = w_in.shape[0]
    nchunk = seq // SSM_T
    assert d == D_MODEL and nchunk == 1 << SSM_SCAN_STEPS

    toep, bpow, cpow, lama, lamb = _ssm_prep(ssm_a_re, ssm_a_im, ssm_log_dt, ssm_b_re, ssm_b_im,
                                             ssm_c_re, ssm_c_im)

    w_br2 = w_branch.reshape(depth, N_BRANCH * W_BRANCH, d)
    w_in_b = _cast_bf16(w_in, 0, CAST_ROWS)

    row = lambda a: a.reshape(1, -1)
    x2 = x.reshape(m, d)
    for l in range(depth):
        h, u, y_pool, y_conv, y_gmlp = _proj_mix(
            x2, row(g_pre_mix[l]), w_in_b, 0, pool_w[l].astype(BF16), row(pool_scale[l]),
            jnp.broadcast_to(conv_w[l][:, None, :], (CONV_WIDTH, SUBLANES, W_BRANCH)),
            row(conv_b[l]), row(conv_ln_g[l]), row(conv_ln_b[l]), row(gmlp_ln_g[l]), row(gmlp_ln_b[l]),
            gmlp_ws[l], jnp.transpose(gmlp_bs[l]), seq=seq)
        yraw, (w_br_b, w_o_b) = _ssm_scan(u, toep, bpow, cpow, lama, lamb, l, (w_br2, w_o), nchunk=nchunk)
        y_ssm = _ssm_post(yraw, u, row(ssm_d[l]), ssm_w_glu[l].astype(BF16))
        merged, (w_ff1_b, w_ff2_b) = _merge(h, (y_ssm, y_pool, y_conv, y_gmlp), w_in_b,
                                            w_br_b.reshape(N_BRANCH, W_BRANCH, d), (w_ff1, w_ff2), l)
        x2, h2 = _out_proj(merged, w_o_b, x2, row(g_post_mix[l]), row(g_pre_mlp[l]), 0)
        nxt = (w_in,) if l + 1 < depth else ()
        x2, nxt_b = _ffn(x2, h2, w_ff1_b, w_ff2_b, row(g_post_mlp[l]), nxt, l + 1)
        if nxt:
            (w_in_b,) = nxt_b
    return x2.reshape(nb, seq, d)
```

```python
import functools

import jax
import jax.numpy as jnp
from jax import lax
from jax.experimental import pallas as pl
from jax.experimental.pallas import tpu as pltpu

D_MODEL = 2048
N_BRANCH = 4
W_BRANCH = D_MODEL // 4
SSM_GROUP = 16
SSM_STATE = 64
POOL_WINDOWS = (2, 4, 8, 16)
POOL_GW = W_BRANCH // len(POOL_WINDOWS)
CONV_WIDTH = 31
GMLP_CHUNK = 128
GMLP_HEADS = 4
GMLP_HD = W_BRANCH // GMLP_HEADS
EPS = 1e-6

LANES = 128
SUBLANES = 8
BF16_ROWS = 2 * SUBLANES
V7X_VMEM_BYTES = 64 * 1024 * 1024

OFF_SSM = 0
OFF_POOL = OFF_SSM + W_BRANCH
OFF_CONV = OFF_POOL + W_BRANCH
OFF_GMLP = OFF_CONV + 2 * W_BRANCH
OFF_GATE = OFF_GMLP + 2 * W_BRANCH

SSM_T = 16
SSM_LANES = SSM_T * SSM_GROUP
SSM_SCAN_STEPS = 7
CONV_HALO = 32
POOL_HALO = SUBLANES * len(POOL_WINDOWS)
assert all(win == 2 << k for k, win in enumerate(POOL_WINDOWS))
CONV_RB = 64

VMEM_LIMIT = V7X_VMEM_BYTES - 8 * 1024 * 1024
TM_PROJ = 512
TM_MERGE, TN_MERGE = 1024, 256
TM_OUT = 512
TM_FFN, TF_FFN = 1024, 512
TM_POST = 512
CAST_ROWS = 256
NORM_ROWS = 128
BF16 = jnp.bfloat16
F32 = jnp.float32


def _cparams(sem):
    return pltpu.CompilerParams(dimension_semantics=sem, vmem_limit_bytes=VMEM_LIMIT)


def _sigmoid(x):
    return 0.5 * (1.0 + jnp.tanh(0.5 * x))


def _gelu(x):
    c = 0.7978845608028654
    return 0.5 * x * (1.0 + jnp.tanh(c * (x + 0.044715 * (x * x * x))))


def _rms(x, g):
    ms = jnp.mean(x * x, axis=-1, keepdims=True)
    return x * lax.rsqrt(ms + EPS) * g


def _layer_norm(x, g, b):
    mu = jnp.mean(x, axis=-1, keepdims=True)
    xc = x - mu
    var = jnp.mean(xc * xc, axis=-1, keepdims=True)
    return xc * lax.rsqrt(var + EPS) * g + b


def _cast_kernel(w_ref, o_ref):
    o_ref[...] = w_ref[...].astype(o_ref.dtype)


def _cast_bf16(w, layer, rows):
    _, r, c = w.shape
    return pl.pallas_call(
        _cast_kernel,
        grid=(r // rows,),
        in_specs=[pl.BlockSpec((1, rows, c), lambda i: (layer, i, 0))],
        out_specs=pl.BlockSpec((1, rows, c), lambda i: (0, i, 0)),
        out_shape=jax.ShapeDtypeStruct((1, r, c), BF16),
        compiler_params=_cparams(("parallel",)),
        name="cast_bf16",
    )(w)


def _side_cast_specs(weights, layer, nsteps, step_of):
    in_specs, out_specs, out_shapes = [], [], []
    for w in weights:
        _, r, c = w.shape
        assert r % nsteps == 0 and (r // nsteps) % BF16_ROWS == 0
        rows = r // nsteps
        in_specs.append(pl.BlockSpec((None, rows, c), lambda *ids: (layer, step_of(*ids), 0)))
        out_specs.append(pl.BlockSpec((None, rows, c), lambda *ids: (0, step_of(*ids), 0)))
        out_shapes.append(jax.ShapeDtypeStruct((1, r, c), BF16))
    return in_specs, out_specs, out_shapes


def _side_cast(in_refs, out_refs):
    for wi, wo in zip(in_refs, out_refs):
        wo[...] = wi[...].astype(wo.dtype)


def _proj_mix_kernel(x_ref, g_ref, w_ref, poolw_ref, pscale_ref, convw_ref, convb_ref, clg_ref, clb_ref,
                     glg_ref, glb_ref, ws_ref, bst_ref,
                     h_ref, u_ref, ypool_ref, yconv_ref, ygmlp_ref, vbuf, pbuf, sbuf, qbuf, ugbuf, vgbuf,
                     *, rows, tiles_per_seq):
    c = pl.program_id(0) % tiles_per_seq
    tc = x_ref.shape[0]
    w = W_BRANCH

    @pl.when(c == 0)
    def _():
        vbuf[0:CONV_HALO, :] = jnp.zeros((CONV_HALO, w), F32)
        pbuf[0:POOL_HALO, :] = jnp.zeros((POOL_HALO, w), F32)

    def body(i, carry):
        r = pl.multiple_of(i * rows, rows)
        h_ref[pl.ds(r, rows), :] = _rms(x_ref[pl.ds(r, rows), :], g_ref[...]).astype(h_ref.dtype)
        return carry
    lax.fori_loop(0, tc // rows, body, 0, unroll=2)

    def proj(off, width, r0=0, nrows=tc):
        return jnp.dot(h_ref[r0:r0 + nrows, :], w_ref[:, off:off + width], preferred_element_type=F32)

    half = tc // 2
    for r0 in range(0, tc, half):
        gm = proj(OFF_GMLP, 2 * w, r0, half)
        ugbuf[r0:r0 + half, :] = _gelu(gm[:, :w])
        vgbuf[r0:r0 + half, :] = _layer_norm(_gelu(gm[:, w:]), glg_ref[...], glb_ref[...]).astype(vgbuf.dtype)
    for r0 in range(0, tc, half):
        cv = proj(OFF_CONV, 2 * w, r0, half)
        vbuf[CONV_HALO + r0:CONV_HALO + r0 + half, :] = cv[:, :w] * _sigmoid(cv[:, w:])
    pbuf[POOL_HALO:POOL_HALO + tc, :] = proj(OFF_POOL, w)
    u_ref[...] = proj(OFF_SSM, w)

    src = pbuf
    for k in range(len(POOL_WINDOWS)):
        lo, sh, c0 = SUBLANES * (k + 1), 1 << k, k * POOL_GW
        n = POOL_HALO + tc - lo
        a_hi = src[lo:lo + n, c0:] if k == 0 else src[k - 1, lo:lo + n, c0:]
        a_lo = src[lo - sh:lo - sh + n, c0:] if k == 0 else src[k - 1, lo - sh:lo - sh + n, c0:]
        qbuf[k, lo:lo + n, c0:] = a_hi + a_lo
        src = qbuf
    t = c * tc + lax.broadcasted_iota(jnp.int32, (tc, 1), 0)
    for gi, win in enumerate(POOL_WINDOWS):
        cols = slice(gi * POOL_GW, (gi + 1) * POOL_GW)
        tok = pbuf[POOL_HALO:POOL_HALO + tc, cols]
        s = qbuf[gi, POOL_HALO:POOL_HALO + tc, cols]
        count = jnp.minimum(t + 1, win).astype(F32)
        pooled = s / count - tok
        mixed = jnp.dot(pooled.astype(BF16), poolw_ref[gi], preferred_element_type=F32)
        ypool_ref[:, cols] = (mixed * pscale_ref[:, cols]).astype(ypool_ref.dtype)
    pbuf[0:POOL_HALO, :] = pbuf[tc:tc + POOL_HALO, :]

    sl = SUBLANES
    span = CONV_HALO - sl
    for b in range(1, sl):
        sbuf[b - 1] = vbuf[sl - b:sl - b + span + tc, :]
    for rb in range(tc // CONV_RB):
        r0 = rb * CONV_RB
        acc = jnp.broadcast_to(convb_ref[...], (CONV_RB, w))
        for b in range(sl):
            for a in range((CONV_WIDTH - 1 - b) // sl + 1):
                k = CONV_WIDTH - 1 - (sl * a + b)
                lo = span - sl * a + r0
                tap = sbuf[b - 1, lo:lo + CONV_RB, :] if b else vbuf[sl + lo:sl + lo + CONV_RB, :]
                wk = convw_ref[k]
                acc = acc + (tap.reshape(CONV_RB // sl, sl, w) * wk[None]).reshape(CONV_RB, w)
        yn = _layer_norm(acc, clg_ref[...], clb_ref[...])
        yconv_ref[r0:r0 + CONV_RB, :] = (yn * _sigmoid(yn)).astype(yconv_ref.dtype)
    vbuf[0:CONV_HALO, :] = vbuf[tc:tc + CONV_HALO, :]

    ri = lax.broadcasted_iota(jnp.int32, (GMLP_CHUNK, GMLP_CHUNK), 0)
    ci = lax.broadcasted_iota(jnp.int32, (GMLP_CHUNK, GMLP_CHUNK), 1)
    causal = ri >= ci
    for hd in range(GMLP_HEADS):
        wsm = jnp.where(causal, ws_ref[hd], 0.0).astype(BF16)
        cols = slice(hd * GMLP_HD, (hd + 1) * GMLP_HD)
        for ch in range(tc // GMLP_CHUNK):
            rws = slice(ch * GMLP_CHUNK, (ch + 1) * GMLP_CHUNK)
            sv = jnp.dot(wsm, vgbuf[rws, cols], preferred_element_type=F32) + bst_ref[:, hd:hd + 1]
            ygmlp_ref[rws, cols] = (ugbuf[rws, cols] * sv).astype(ygmlp_ref.dtype)


def _proj_mix(x2, g, w_in, layer, poolw, pscale, convw, convb, clg, clb, glg, glb, ws, bst, *, seq, tm=TM_PROJ):
    m, d = x2.shape
    n = OFF_GATE
    w = W_BRANCH
    assert seq % tm == 0 and tm % GMLP_CHUNK == 0 and tm % CONV_RB == 0
    row = lambda width: pl.BlockSpec((tm, width), lambda i: (i, 0))
    full = lambda a: pl.BlockSpec(a.shape, lambda i: (0,) * a.ndim)
    args = (poolw, pscale, convw, convb, clg, clb, glg, glb, ws, bst)
    return pl.pallas_call(
        functools.partial(_proj_mix_kernel, rows=NORM_ROWS, tiles_per_seq=seq // tm),
        grid=(m // tm,),
        in_specs=[row(d), full(g),
                  pl.BlockSpec((None, d, n), lambda i: (layer, 0, 0), pipeline_mode=pl.Buffered(1))]
        + [full(a) for a in args],
        out_specs=[row(d), row(w), row(w), row(w), row(w)],
        out_shape=[jax.ShapeDtypeStruct((m, d), BF16), jax.ShapeDtypeStruct((m, w), F32)]
        + [jax.ShapeDtypeStruct((m, w), BF16)] * 3,
        scratch_shapes=[pltpu.VMEM((CONV_HALO + tm, w), F32), pltpu.VMEM((POOL_HALO + tm, w), F32),
                        pltpu.VMEM((SUBLANES - 1, CONV_HALO - SUBLANES + tm, w), F32),
                        pltpu.VMEM((len(POOL_WINDOWS), POOL_HALO + tm, w), F32),
                        pltpu.VMEM((tm, w), F32), pltpu.VMEM((tm, w), BF16)],
        compiler_params=_cparams(("arbitrary",)),
        name="proj_mix",
    )(x2, g, w_in, *args)


PREP_GB = 8


def _ssm_prep_kernel(ar_ref, ai_ref, ldt_ref, cre_ref, cim_ref, bre_ref, bim_ref,
                     toep_ref, bpow_ref, cpow_ref, lama_ref, lamb_ref, cl_s, bl_s):
    n = SSM_STATE
    p = SSM_GROUP
    lanes = SSM_LANES
    lane = lax.broadcasted_iota(jnp.int32, (1, lanes), 1)
    for e in range(PREP_GB):
        ar = ar_ref[e]
        ai = ai_ref[e]
        dt = jnp.exp(ldt_ref[e])
        mag = jnp.exp(ar * dt)
        ang = ai * dt
        lr = mag * jnp.cos(ang)
        li = mag * jnp.sin(ang)
        den = ar * ar + ai * ai
        fr = ((lr - 1.0) * ar + li * ai) / den
        fi = (li * ar - (lr - 1.0) * ai) / den
        cre = cre_ref[e]
        cim = cim_ref[e]
        br = bre_ref[e]
        bi = bim_ref[e]
        bbr = fr * br - fi * bi
        bbi = fr * bi + fi * br
        pr = jnp.ones_like(lr)
        pi = jnp.zeros_like(lr)
        for k in range(SSM_T + 1):
            cl_s[k * p:(k + 1) * p, 0:n] = cre * pr - cim * pi
            cl_s[k * p:(k + 1) * p, n:2 * n] = -(cre * pi + cim * pr)
            if k < SSM_T:
                s = SSM_T - 1 - k
                bl_s[s * p:(s + 1) * p, 0:n] = pr * bbr - pi * bbi
                bl_s[s * p:(s + 1) * p, n:2 * n] = pr * bbi + pi * bbr
                pr, pi = pr * lr - pi * li, pr * li + pi * lr
        qr, qi = pr, pi
        for j in range(SUBLANES):
            if j < SSM_SCAN_STEPS:
                lama_ref[e, j:j + 1, 0:n] = qr
                lama_ref[e, j:j + 1, n:2 * n] = qr
                lamb_ref[e, j:j + 1, 0:n] = -qi
                lamb_ref[e, j:j + 1, n:2 * n] = qi
                qr, qi = qr * qr - qi * qi, 2.0 * (qr * qi)
            else:
                lama_ref[e, j:j + 1, :] = jnp.zeros((1, 2 * n), F32)
                lamb_ref[e, j:j + 1, :] = jnp.zeros((1, 2 * n), F32)
        cpow_ref[e] = cl_s[p:(SSM_T + 1) * p, :].astype(cpow_ref.dtype)
        bpow_ref[e] = bl_s[...].astype(bpow_ref.dtype)
        bbcat = bl_s[(SSM_T - 1) * p:SSM_T * p, :]
        kt = lax.dot_general(bbcat, cl_s[0:SSM_T * p, :], (((1,), (1,)), ((), ())),
                             preferred_element_type=F32, precision=lax.Precision.HIGHEST)
        for s in range(SSM_T):
            blk = pltpu.roll(kt, s * p, axis=1) if s else kt
            toep_ref[e, s * p:(s + 1) * p, :] = jnp.where(lane >= s * p, blk, 0.0).astype(toep_ref.dtype)


def _ssm_prep(a_re, a_im, log_dt, b_re, b_im, c_re, c_im):
    n, p = SSM_STATE, SSM_GROUP
    gg = a_re.shape[0] * a_re.shape[1]
    gb = PREP_GB
    lanes = SSM_LANES
    row = lambda a: a.reshape(gg, 1, n)
    ldt = jnp.broadcast_to(log_dt.reshape(gg, 1), (gg, n))
    bt = lambda b: jnp.transpose(b.reshape(gg, n, p), (0, 2, 1))
    rspec = pl.BlockSpec((gb, 1, n), lambda g: (g, 0, 0))
    pspec = pl.BlockSpec((gb, p, n), lambda g: (g, 0, 0))
    ospec = lambda a, b: pl.BlockSpec((gb, a, b), lambda g: (g, 0, 0))
    return pl.pallas_call(
        _ssm_prep_kernel,
        grid=(gg // gb,),
        in_specs=[rspec, rspec, rspec, pspec, pspec, pspec, pspec],
        out_specs=[ospec(lanes, lanes), ospec(lanes, 2 * n), ospec(lanes, 2 * n),
                   ospec(SUBLANES, 2 * n), ospec(SUBLANES, 2 * n)],
        out_shape=[
            jax.ShapeDtypeStruct((gg, lanes, lanes), BF16),
            jax.ShapeDtypeStruct((gg, lanes, 2 * n), BF16),
            jax.ShapeDtypeStruct((gg, lanes, 2 * n), BF16),
            jax.ShapeDtypeStruct((gg, SUBLANES, 2 * n), F32),
            jax.ShapeDtypeStruct((gg, SUBLANES, 2 * n), F32),
        ],
        scratch_shapes=[pltpu.VMEM(((SSM_T + 1) * p, 2 * n), F32), pltpu.VMEM((SSM_T * p, 2 * n), F32)],
        compiler_params=_cparams(("arbitrary",)),
        name="ssm_prep",
    )(row(a_re), row(a_im), row(ldt), c_re.reshape(gg, p, n), c_im.reshape(gg, p, n), bt(b_re), bt(b_im))


SSM_GB = LANES // SSM_GROUP
SSM_RB = BF16_ROWS


def _piece_transpose(xs, masks):
    n = len(xs)
    w = LANES // n
    rolled = []
    for k in range(n):
        wk = xs[k % n]
        for a in range(1, n):
            wk = jnp.where(masks[a], xs[(a + k) % n], wk)
        rolled.append(pltpu.roll(wk, w * k, axis=1) if k else wk)
    ys = []
    for a in range(n):
        ya = rolled[(0 - a) % n]
        for b in range(1, n):
            ya = jnp.where(masks[b], rolled[(b - a) % n], ya)
        ys.append(ya)
    return ys


def _ssm_kernel(*refs, nchunk, pad, n_side):
    u_ref, toep_ref, bpow_ref, cpow_ref, lama_ref, lamb_ref = refs[:6]
    side_in = refs[6:6 + n_side]
    y_ref = refs[6 + n_side]
    side_out = refs[7 + n_side:7 + 2 * n_side]
    up_ref, yp_ref, xr_ref, xi_ref = refs[7 + 2 * n_side:]
    _side_cast(side_in, side_out)
    rows = up_ref.shape[1]
    n2 = 2 * SSM_STATE
    t = SSM_T
    lane = lax.broadcasted_iota(jnp.int32, (1, LANES), 1)
    masks = [lane // SSM_GROUP == a for a in range(SSM_GB)]
    nhalf = t // SSM_GB

    def to_chunks(rb, carry):
        r = pl.multiple_of(rb * SSM_RB, SSM_RB)
        for hf in range(nhalf):
            xs = [u_ref[pl.ds(r * t + hf * SSM_GB + b, SSM_RB, stride=t), :] for b in range(SSM_GB)]
            ys = _piece_transpose(xs, masks)
            for a in range(SSM_GB):
                up_ref[a, pl.ds(r, SSM_RB), hf * LANES:(hf + 1) * LANES] = ys[a].astype(up_ref.dtype)
        return carry
    lax.fori_loop(0, rows // SSM_RB, to_chunks, 0, unroll=2)

    cidx = lax.broadcasted_iota(jnp.int32, (rows, 1), 0) % nchunk
    lo = lane < SSM_STATE
    swap = lambda v: pltpu.roll(v, SSM_STATE, axis=1)
    nt_dot = lambda x, wt: lax.dot_general(x, wt, (((1,), (1,)), ((), ())), preferred_element_type=F32)
    xr_ref[0:pad, :] = jnp.zeros((pad, n2), F32)
    xi_ref[0:pad, :] = jnp.zeros((pad, n2), F32)
    for a in range(0, SSM_GB, 2):
        b = a + 1
        ua, ub = up_ref[a], up_ref[b]
        sa = jnp.dot(ua, bpow_ref[a], preferred_element_type=F32)
        sb = jnp.dot(ub, bpow_ref[b], preferred_element_type=F32)
        xr_ref[pad:pad + rows, :] = jnp.where(lo, sa, swap(sb))
        xi_ref[pad:pad + rows, :] = jnp.where(lo, swap(sa), sb)
        for j in range(SSM_SCAN_STEPS):
            sh = 1 << j
            lr = jnp.where(lo, lama_ref[a, j:j + 1, :], lama_ref[b, j:j + 1, :])
            li = jnp.where(lo, -lamb_ref[a, j:j + 1, :], lamb_ref[b, j:j + 1, :])
            pr = jnp.where(cidx >= sh, xr_ref[pad - sh:pad - sh + rows, :], 0.0)
            pi = jnp.where(cidx >= sh, xi_ref[pad - sh:pad - sh + rows, :], 0.0)
            cr = xr_ref[pad:pad + rows, :]
            ci = xi_ref[pad:pad + rows, :]
            xr_ref[pad:pad + rows, :] = cr + pr * lr - pi * li
            xi_ref[pad:pad + rows, :] = ci + pi * lr + pr * li
        pr = jnp.where(cidx >= 1, xr_ref[pad - 1:pad - 1 + rows, :], 0.0)
        pi = jnp.where(cidx >= 1, xi_ref[pad - 1:pad - 1 + rows, :], 0.0)
        xa = jnp.where(lo, pr, swap(pi)).astype(BF16)
        xb = jnp.where(lo, swap(pr), pi).astype(BF16)
        yp_ref[a] = jnp.dot(ua, toep_ref[a], preferred_element_type=F32) + nt_dot(xa, cpow_ref[a])
        yp_ref[b] = jnp.dot(ub, toep_ref[b], preferred_element_type=F32) + nt_dot(xb, cpow_ref[b])

    def to_tokens(rb, carry):
        r = pl.multiple_of(rb * SUBLANES, SUBLANES)
        for hf in range(nhalf):
            ys = [yp_ref[a, pl.ds(r, SUBLANES), hf * LANES:(hf + 1) * LANES] for a in range(SSM_GB)]
            xs = _piece_transpose(ys, masks)
            for b in range(SSM_GB):
                y_ref[pl.ds(r * t + hf * SSM_GB + b, SUBLANES, stride=t), :] = xs[b]
        return carry
    lax.fori_loop(0, rows // SUBLANES, to_tokens, 0, unroll=2)


def _ssm_scan(pm, toep, bpow, cpow, lama, lamb, layer, side_weights, *, nchunk):
    m = pm.shape[0]
    rows = m // SSM_T
    lanes = SSM_LANES
    n2 = 2 * SSM_STATE
    pad = 1 << (SSM_SCAN_STEPS - 1)
    nblk = W_BRANCH // LANES
    gspec = lambda a, b: pl.BlockSpec((SSM_GB, a, b), lambda i: (layer * nblk + i, 0, 0))
    s_in, s_out, s_shape = _side_cast_specs(side_weights, layer, nblk, lambda i: i)
    outs = pl.pallas_call(
        functools.partial(_ssm_kernel, nchunk=nchunk, pad=pad, n_side=len(side_weights)),
        grid=(nblk,),
        in_specs=[pl.BlockSpec((m, LANES), lambda i: (0, i)),
                  gspec(lanes, lanes), gspec(lanes, n2), gspec(lanes, n2),
                  gspec(SUBLANES, n2), gspec(SUBLANES, n2)] + s_in,
        out_specs=[pl.BlockSpec((m, LANES), lambda i: (0, i))] + s_out,
        out_shape=[jax.ShapeDtypeStruct((m, W_BRANCH), F32)] + s_shape,
        scratch_shapes=[pltpu.VMEM((SSM_GB, rows, lanes), BF16), pltpu.VMEM((SSM_GB, rows, lanes), F32),
                        pltpu.VMEM((pad + rows, n2), F32), pltpu.VMEM((pad + rows, n2), F32)],
        compiler_params=_cparams(("arbitrary",)),
        name="ssm_scan",
    )(pm, toep, bpow, cpow, lama, lamb, *side_weights)
    return outs[0], outs[1:]


def _ssm_post_kernel(yraw_ref, u_ref, d_ref, wglu_ref, o_ref):
    w = W_BRANCH
    y = yraw_ref[...] + d_ref[...] * u_ref[...]
    zg = jnp.dot(_gelu(y).astype(BF16), wglu_ref[...], preferred_element_type=F32)
    o_ref[...] = (zg[:, :w] * _sigmoid(zg[:, w:])).astype(o_ref.dtype)


def _ssm_post(yraw, u, d, wglu, *, tm=TM_POST):
    m, w = yraw.shape
    row = pl.BlockSpec((tm, w), lambda i: (i, 0))
    return pl.pallas_call(
        _ssm_post_kernel,
        grid=(m // tm,),
        in_specs=[row, row, pl.BlockSpec((1, w), lambda i: (0, 0)), pl.BlockSpec(wglu.shape, lambda i: (0, 0))],
        out_specs=row,
        out_shape=jax.ShapeDtypeStruct((m, w), BF16),
        compiler_params=_cparams(("parallel",)),
        name="ssm_post",
    )(yraw, u, d, wglu)


def _merge_kernel(*refs, n_side):
    nb = N_BRANCH
    h_ref = refs[0]
    ys = refs[1:1 + nb]
    wg = refs[1 + nb:1 + 2 * nb]
    wb = refs[1 + 2 * nb:1 + 3 * nb]
    side_in = refs[1 + 3 * nb:1 + 3 * nb + n_side]
    o_ref = refs[1 + 3 * nb + n_side]
    side_out = refs[2 + 3 * nb + n_side:2 + 3 * nb + 2 * n_side]
    h = h_ref[...]
    acc = None
    for k in range(nb):
        gate = jnp.dot(h, wg[k][...], preferred_element_type=F32)
        yb = jnp.dot(ys[k][...], wb[k][0], preferred_element_type=F32)
        term = yb * _sigmoid(gate)
        acc = term if acc is None else acc + term
    o_ref[...] = acc.astype(o_ref.dtype)
    _side_cast(side_in, side_out)


def _merge(h, ys, wgate, wbranch, side_weights, side_layer, *, tm=TM_MERGE, tn=TN_MERGE):
    m, d = h.shape
    w = W_BRANCH
    nj = d // tn
    j0 = OFF_GATE // tn
    gate_spec = lambda k: pl.BlockSpec((None, d, tn), lambda i, j: (0, 0, j0 + k * nj + j))
    br_spec = lambda k: pl.BlockSpec((1, w, tn), lambda i, j: (k, 0, j))
    s_in, s_out, s_shape = _side_cast_specs(side_weights, side_layer, (m // tm) * nj, lambda i, j: i * nj + j)
    outs = pl.pallas_call(
        functools.partial(_merge_kernel, n_side=len(side_weights)),
        grid=(m // tm, nj),
        in_specs=[pl.BlockSpec((tm, d), lambda i, j: (i, 0))]
        + [pl.BlockSpec((tm, w), lambda i, j: (i, 0))] * N_BRANCH
        + [gate_spec(k) for k in range(N_BRANCH)]
        + [br_spec(k) for k in range(N_BRANCH)]
        + s_in,
        out_specs=[pl.BlockSpec((tm, tn), lambda i, j: (i, j))] + s_out,
        out_shape=[jax.ShapeDtypeStruct((m, d), BF16)] + s_shape,
        compiler_params=_cparams(("arbitrary", "arbitrary")),
        name="gated_merge",
    )(h, *ys, *([wgate] * N_BRANCH), *([wbranch] * N_BRANCH), *side_weights)
    return outs[0], outs[1:]


def _out_kernel(m_ref, w_ref, x_ref, g_ref, gn_ref, o_ref, hn_ref):
    half = m_ref.shape[0] // 2
    for r0 in (0, half):
        rs = slice(r0, r0 + half)
        mix = jnp.dot(m_ref[rs, :], w_ref[...], preferred_element_type=F32)
        x1 = x_ref[rs, :] + _rms(mix, g_ref[...])
        o_ref[rs, :] = x1
        hn_ref[rs, :] = _rms(x1, gn_ref[...]).astype(hn_ref.dtype)


def _out_proj(merged, w_o, x2, g, g_next, layer, *, tm=TM_OUT):
    m, d = x2.shape
    row = pl.BlockSpec((tm, d), lambda i: (i, 0))
    vec = pl.BlockSpec((1, d), lambda i: (0, 0))
    return pl.pallas_call(
        _out_kernel,
        grid=(m // tm,),
        in_specs=[row, pl.BlockSpec((None, d, d), lambda i: (layer, 0, 0), pipeline_mode=pl.Buffered(1)),
                  row, vec, vec],
        out_specs=[row, row],
        out_shape=[jax.ShapeDtypeStruct((m, d), F32), jax.ShapeDtypeStruct((m, d), BF16)],
        compiler_params=_cparams(("parallel",)),
        name="out_proj",
    )(merged, w_o, x2, g, g_next)


def _ffn_kernel(*refs, rows, n_side):
    x_ref, h_ref, w1_ref, w2_ref, gpost_ref = refs[:5]
    side_in = refs[5:5 + n_side]
    o_ref = refs[5 + n_side]
    side_out = refs[6 + n_side:6 + 2 * n_side]
    j = pl.program_id(1)

    @pl.when(j == 0)
    def _():
        o_ref[...] = jnp.zeros(o_ref.shape, o_ref.dtype)

    a = jnp.dot(h_ref[...], w1_ref[...], preferred_element_type=F32)
    a = jnp.maximum(a, 0.0)
    a = (a * a).astype(BF16)
    nw = 512
    for c in range(o_ref.shape[1] // nw):
        o_ref[:, c * nw:(c + 1) * nw] += jnp.dot(a, w2_ref[:, c * nw:(c + 1) * nw],
                                                 preferred_element_type=F32)
    _side_cast(side_in, side_out)

    @pl.when(j == pl.num_programs(1) - 1)
    def _():
        def body(i, carry):
            r = pl.multiple_of(i * rows, rows)
            o_ref[pl.ds(r, rows), :] = (x_ref[pl.ds(r, rows), :]
                                        + _rms(o_ref[pl.ds(r, rows), :], gpost_ref[...]))
            return carry
        lax.fori_loop(0, x_ref.shape[0] // rows, body, 0, unroll=2)


def _ffn(x2, h2, w1, w2, gpost, side_weights, side_layer, *, tm=TM_FFN, tf=TF_FFN):
    m, d = x2.shape
    f = w1.shape[2]
    nj = f // tf
    s_in, s_out, s_shape = _side_cast_specs(side_weights, side_layer, (m // tm) * nj, lambda i, j: i * nj + j)
    outs = pl.pallas_call(
        functools.partial(_ffn_kernel, rows=NORM_ROWS, n_side=len(side_weights)),
        grid=(m // tm, nj),
        in_specs=[
            pl.BlockSpec((tm, d), lambda i, j: (i, 0)),
            pl.BlockSpec((tm, d), lambda i, j: (i, 0)),
            pl.BlockSpec((None, d, tf), lambda i, j: (0, 0, j)),
            pl.BlockSpec((None, tf, d), lambda i, j: (0, j, 0)),
            pl.BlockSpec((1, d), lambda i, j: (0, 0)),
        ] + s_in,
        out_specs=[pl.BlockSpec((tm, d), lambda i, j: (i, 0))] + s_out,
        out_shape=[jax.ShapeDtypeStruct((m, d), F32)] + s_shape,
        compiler_params=_cparams(("arbitrary", "arbitrary")),
        name="ffn",
    )(x2, h2, w1, w2, gpost, *side_weights)
    return outs[0], outs[1:]


def kernel(x, g_pre_mix, w_in, ssm_a_re, ssm_a_im, ssm_log_dt, ssm_b_re, ssm_b_im, ssm_c_re, ssm_c_im,
           ssm_d, ssm_w_glu, pool_w, pool_scale, conv_w, conv_b, conv_ln_g, conv_ln_b, gmlp_ln_g,
           gmlp_ln_b, gmlp_ws, gmlp_bs, w_branch, w_o, g_post_mix, g_pre_mlp, w_ff1, w_ff2, g_post_mlp):
    nb, seq, d = x.shape
    m = nb * seq
    depth = w_in.shape[0]
    nchunk = seq // SSM_T
    assert d == D_MODEL and nchunk == 1 << SSM_SCAN_STEPS

    toep, bpow, cpow, lama, lamb = _ssm_prep(ssm_a_re, ssm_a_im, ssm_log_dt, ssm_b_re, ssm_b_im,
                                             ssm_c_re, ssm_c_im)

    w_br2 = w_branch.reshape(depth, N_BRANCH * W_BRANCH, d)
    w_in_b = _cast_bf16(w_in, 0, CAST_ROWS)

    row = lambda a: a.reshape(1, -1)
    x2 = x.reshape(m, d)
    for l in range(depth):
        h, u, y_pool, y_conv, y_gmlp = _proj_mix(
            x2, row(g_pre_mix[l]), w_in_b, 0, pool_w[l].astype(BF16), row(pool_scale[l]),
            jnp.broadcast_to(conv_w[l][:, None, :], (CONV_WIDTH, SUBLANES, W_BRANCH)),
            row(conv_b[l]), row(conv_ln_g[l]), row(conv_ln_b[l]), row(gmlp_ln_g[l]), row(gmlp_ln_b[l]),
            gmlp_ws[l], jnp.transpose(gmlp_bs[l]), seq=seq)
        yraw, (w_br_b, w_o_b) = _ssm_scan(u, toep, bpow, cpow, lama, lamb, l, (w_br2, w_o), nchunk=nchunk)
        y_ssm = _ssm_post(yraw, u, row(ssm_d[l]), ssm_w_glu[l].astype(BF16))
        merged, (w_ff1_b, w_ff2_b) = _merge(h, (y_ssm, y_pool, y_conv, y_gmlp), w_in_b,
                                            w_br_b.reshape(N_BRANCH, W_BRANCH, d), (w_ff1, w_ff2), l)
        x2, h2 = _out_proj(merged, w_o_b, x2, row(g_post_mix[l]), row(g_pre_mlp[l]), 0)
        nxt = (w_in,) if l + 1 < depth else ()
        x2, nxt_b = _ffn(x2, h2, w_ff1_b, w_ff2_b, row(g_post_mlp[l]), nxt, l + 1)
        if nxt:
            (w_in_b,) = nxt_b
    return x2.reshape(nb, seq, d)
```

```python
import functools

import jax
import jax.numpy as jnp
from jax import lax
from jax.experimental import pallas as pl
from jax.experimental.pallas import tpu as pltpu

D_MODEL = 2048
N_BRANCH = 4
W_BRANCH = D_MODEL // 4
SSM_GROUP = 16
SSM_STATE = 64
POOL_WINDOWS = (2, 4, 8, 16)
POOL_GW = W_BRANCH // len(POOL_WINDOWS)
CONV_WIDTH = 31
GMLP_CHUNK = 128
GMLP_HEADS = 4
GMLP_HD = W_BRANCH // GMLP_HEADS
EPS = 1e-6

LANES = 128
SUBLANES = 8
BF16_ROWS = 2 * SUBLANES
V7X_VMEM_BYTES = 64 * 1024 * 1024

OFF_SSM = 0
OFF_POOL = OFF_SSM + W_BRANCH
OFF_CONV = OFF_POOL + W_BRANCH
OFF_GMLP = OFF_CONV + 2 * W_BRANCH
OFF_GATE = OFF_GMLP + 2 * W_BRANCH

SSM_T = 16
SSM_LANES = SSM_T * SSM_GROUP
SSM_SCAN_STEPS = 7
CONV_HALO = 32
POOL_HALO = SUBLANES * len(POOL_WINDOWS)
assert all(win == 2 << k for k, win in enumerate(POOL_WINDOWS))
CONV_RB = 64

VMEM_LIMIT = V7X_VMEM_BYTES - 8 * 1024 * 1024
TM_PROJ = 512
TM_MERGE, TN_MERGE = 1024, 256
TM_OUT = 512
TM_FFN, TF_FFN = 1024, 512
TM_POST = 512
CAST_ROWS = 256
NORM_ROWS = 128
BF16 = jnp.bfloat16
F32 = jnp.float32


def _cparams(sem):
    return pltpu.CompilerParams(dimension_semantics=sem, vmem_limit_bytes=VMEM_LIMIT)


def _sigmoid(x):
    return 0.5 * (1.0 + jnp.tanh(0.5 * x))


def _gelu(x):
    c = 0.7978845608028654
    return 0.5 * x * (1.0 + jnp.tanh(c * (x + 0.044715 * (x * x * x))))


def _rms(x, g):
    ms = jnp.mean(x * x, axis=-1, keepdims=True)
    return x * lax.rsqrt(ms + EPS) * g


def _layer_norm(x, g, b):
    mu = jnp.mean(x, axis=-1, keepdims=True)
    xc = x - mu
    var = jnp.mean(xc * xc, axis=-1, keepdims=True)
    return xc * lax.rsqrt(var + EPS) * g + b


def _cast_kernel(w_ref, o_ref):
    o_ref[...] = w_ref[...].astype(o_ref.dtype)


def _cast_bf16(w, layer, rows):
    _, r, c = w.shape
    return pl.pallas_call(
        _cast_kernel,
        grid=(r // rows,),
        in_specs=[pl.BlockSpec((1, rows, c), lambda i: (layer, i, 0))],
        out_specs=pl.BlockSpec((1, rows, c), lambda i: (0, i, 0)),
        out_shape=jax.ShapeDtypeStruct((1, r, c), BF16),
        compiler_params=_cparams(("parallel",)),
        name="cast_bf16",
    )(w)


def _side_cast_specs(weights, layer, nsteps, step_of):
    in_specs, out_specs, out_shapes = [], [], []
    for w in weights:
        _, r, c = w.shape
        assert r % nsteps == 0 and (r // nsteps) % BF16_ROWS == 0
        rows = r // nsteps
        in_specs.append(pl.BlockSpec((None, rows, c), lambda *ids: (layer, step_of(*ids), 0)))
        out_specs.append(pl.BlockSpec((None, rows, c), lambda *ids: (0, step_of(*ids), 0)))
        out_shapes.append(jax.ShapeDtypeStruct((1, r, c), BF16))
    return in_specs, out_specs, out_shapes


def _side_cast(in_refs, out_refs):
    for wi, wo in zip(in_refs, out_refs):
        wo[...] = wi[...].astype(wo.dtype)


def _proj_mix_kernel(x_ref, g_ref, w_ref, poolw_ref, pscale_ref, convw_ref, convb_ref, clg_ref, clb_ref,
                     glg_ref, glb_ref, ws_ref, bst_ref,
                     h_ref, u_ref, ypool_ref, yconv_ref, ygmlp_ref, vbuf, pbuf, sbuf, qbuf, ugbuf, vgbuf,
                     *, rows, tiles_per_seq):
    c = pl.program_id(0) % tiles_per_seq
    tc = x_ref.shape[0]
    w = W_BRANCH

    @pl.when(c == 0)
    def _():
        vbuf[0:CONV_HALO, :] = jnp.zeros((CONV_HALO, w), F32)
        pbuf[0:POOL_HALO, :] = jnp.zeros((POOL_HALO, w), F32)

    def body(i, carry):
        r = pl.multiple_of(i * rows, rows)
        h_ref[pl.ds(r, rows), :] = _rms(x_ref[pl.ds(r, rows), :], g_ref[...]).astype(h_ref.dtype)
        return carry
    lax.fori_loop(0, tc // rows, body, 0, unroll=2)

    def proj(off, width, r0=0, nrows=tc):
        return jnp.dot(h_ref[r0:r0 + nrows, :], w_ref[:, off:off + width], preferred_element_type=F32)

    half = tc // 2
    for r0 in range(0, tc, half):
        gm = proj(OFF_GMLP, 2 * w, r0, half)
        ugbuf[r0:r0 + half, :] = _gelu(gm[:, :w])
        vgbuf[r0:r0 + half, :] = _layer_norm(_gelu(gm[:, w:]), glg_ref[...], glb_ref[...]).astype(vgbuf.dtype)
    for r0 in range(0, tc, half):
        cv = proj(OFF_CONV, 2 * w, r0, half)
        vbuf[CONV_HALO + r0:CONV_HALO + r0 + half, :] = cv[:, :w] * _sigmoid(cv[:, w:])
    pbuf[POOL_HALO:POOL_HALO + tc, :] = proj(OFF_POOL, w)
    u_ref[...] = proj(OFF_SSM, w)

    src = pbuf
    for k in range(len(POOL_WINDOWS)):
        lo, sh, c0 = SUBLANES * (k + 1), 1 << k, k * POOL_GW
        n = POOL_HALO + tc - lo
        a_hi = src[lo:lo + n, c0:] if k == 0 else src[k - 1, lo:lo + n, c0:]
        a_lo = src[lo - sh:lo - sh + n, c0:] if k == 0 else src[k - 1, lo - sh:lo - sh + n, c0:]
        qbuf[k, lo:lo + n, c0:] = a_hi + a_lo
        src = qbuf
    t = c * tc + lax.broadcasted_iota(jnp.int32, (tc, 1), 0)
    for gi, win in enumerate(POOL_WINDOWS):
        cols = slice(gi * POOL_GW, (gi + 1) * POOL_GW)
        tok = pbuf[POOL_HALO:POOL_HALO + tc, cols]
        s = qbuf[gi, POOL_HALO:POOL_HALO + tc, cols]
        count = jnp.minimum(t + 1, win).astype(F32)
        pooled = s / count - tok
        mixed = jnp.dot(pooled.astype(BF16), poolw_ref[gi], preferred_element_type=F32)
        ypool_ref[:, cols] = (mixed * pscale_ref[:, cols]).astype(ypool_ref.dtype)
    pbuf[0:POOL_HALO, :] = pbuf[tc:tc + POOL_HALO, :]

    sl = SUBLANES
    span = CONV_HALO - sl
    for b in range(1, sl):
        sbuf[b - 1] = vbuf[sl - b:sl - b + span + tc, :]
    for rb in range(tc // CONV_RB):
        r0 = rb * CONV_RB
        acc = jnp.broadcast_to(convb_ref[...], (CONV_RB, w))
        for b in range(sl):
            for a in range((CONV_WIDTH - 1 - b) // sl + 1):
                k = CONV_WIDTH - 1 - (sl * a + b)
                lo = span - sl * a + r0
                tap = sbuf[b - 1, lo:lo + CONV_RB, :] if b else vbuf[sl + lo:sl + lo + CONV_RB, :]
                wk = convw_ref[k]
                acc = acc + (tap.reshape(CONV_RB // sl, sl, w) * wk[None]).reshape(CONV_RB, w)
        yn = _layer_norm(acc, clg_ref[...], clb_ref[...])
        yconv_ref[r0:r0 + CONV_RB, :] = (yn * _sigmoid(yn)).astype(yconv_ref.dtype)
    vbuf[0:CONV_HALO, :] = vbuf[tc:tc + CONV_HALO, :]

    ri = lax.broadcasted_iota(jnp.int32, (GMLP_CHUNK, GMLP_CHUNK), 0)
    ci = lax.broadcasted_iota(jnp.int32, (GMLP_CHUNK, GMLP_CHUNK), 1)
    causal = ri >= ci
    for hd in range(GMLP_HEADS):
        wsm = jnp.where(causal, ws_ref[hd], 0.0).astype(BF16)
        cols = slice(hd * GMLP_HD, (hd + 1) * GMLP_HD)
        for ch in range(tc // GMLP_CHUNK):
            rws = slice(ch * GMLP_CHUNK, (ch + 1) * GMLP_CHUNK)
            sv = jnp.dot(wsm, vgbuf[rws, cols], preferred_element_type=F32) + bst_ref[:, hd:hd + 1]
            ygmlp_ref[rws, cols] = (ugbuf[rws, cols] * sv).astype(ygmlp_ref.dtype)


def _proj_mix(x2, g, w_in, layer, poolw, pscale, convw, convb, clg, clb, glg, glb, ws, bst, *, seq, tm=TM_PROJ):
    m, d = x2.shape
    n = OFF_GATE
    w = W_BRANCH
    assert seq % tm == 0 and tm % GMLP_CHUNK == 0 and tm % CONV_RB == 0
    row = lambda width: pl.BlockSpec((tm, width), lambda i: (i, 0))
    full = lambda a: pl.BlockSpec(a.shape, lambda i: (0,) * a.ndim)
    args = (poolw, pscale, convw, convb, clg, clb, glg, glb, ws, bst)
    return pl.pallas_call(
        functools.partial(_proj_mix_kernel, rows=NORM_ROWS, tiles_per_seq=seq // tm),
        grid=(m // tm,),
        in_specs=[row(d), full(g),
                  pl.BlockSpec((None, d, n), lambda i: (layer, 0, 0), pipeline_mode=pl.Buffered(1))]
        + [full(a) for a in args],
        out_specs=[row(d), row(w), row(w), row(w), row(w)],
        out_shape=[jax.ShapeDtypeStruct((m, d), BF16), jax.ShapeDtypeStruct((m, w), F32)]
        + [jax.ShapeDtypeStruct((m, w), BF16)] * 3,
        scratch_shapes=[pltpu.VMEM((CONV_HALO + tm, w), F32), pltpu.VMEM((POOL_HALO + tm, w), F32),
                        pltpu.VMEM((SUBLANES - 1, CONV_HALO - SUBLANES + tm, w), F32),
                        pltpu.VMEM((len(POOL_WINDOWS), POOL_HALO + tm, w), F32),
                        pltpu.VMEM((tm, w), F32), pltpu.VMEM((tm, w), BF16)],
        compiler_params=_cparams(("arbitrary",)),
        name="proj_mix",
    )(x2, g, w_in, *args)


PREP_GB = 8


def _ssm_prep_kernel(ar_ref, ai_ref, ldt_ref, cre_ref, cim_ref, bre_ref, bim_ref,
                     toep_ref, bpow_ref, cpow_ref, lama_ref, lamb_ref, cl_s, bl_s):
    n = SSM_STATE
    p = SSM_GROUP
    lanes = SSM_LANES
    lane = lax.broadcasted_iota(jnp.int32, (1, lanes), 1)
    for e in range(PREP_GB):
        ar = ar_ref[e]
        ai = ai_ref[e]
        dt = jnp.exp(ldt_ref[e])
        mag = jnp.exp(ar * dt)
        ang = ai * dt
        lr = mag * jnp.cos(ang)
        li = mag * jnp.sin(ang)
        den = ar * ar + ai * ai
        fr = ((lr - 1.0) * ar + li * ai) / den
        fi = (li * ar - (lr - 1.0) * ai) / den
        cre = cre_ref[e]
        cim = cim_ref[e]
        br = bre_ref[e]
        bi = bim_ref[e]
        bbr = fr * br - fi * bi
        bbi = fr * bi + fi * br
        pr = jnp.ones_like(lr)
        pi = jnp.zeros_like(lr)
        for k in range(SSM_T + 1):
            cl_s[k * p:(k + 1) * p, 0:n] = cre * pr - cim * pi
            cl_s[k * p:(k + 1) * p, n:2 * n] = -(cre * pi + cim * pr)
            if k < SSM_T:
                s = SSM_T - 1 - k
                bl_s[s * p:(s + 1) * p, 0:n] = pr * bbr - pi * bbi
                bl_s[s * p:(s + 1) * p, n:2 * n] = pr * bbi + pi * bbr
                pr, pi = pr * lr - pi * li, pr * li + pi * lr
        qr, qi = pr, pi
        for j in range(SUBLANES):
            if j < SSM_SCAN_STEPS:
                lama_ref[e, j:j + 1, 0:n] = qr
                lama_ref[e, j:j + 1, n:2 * n] = qr
                lamb_ref[e, j:j + 1, 0:n] = -qi
                lamb_ref[e, j:j + 1, n:2 * n] = qi
                qr, qi = qr * qr - qi * qi, 2.0 * (qr * qi)
            else:
                lama_ref[e, j:j + 1, :] = jnp.zeros((1, 2 * n), F32)
                lamb_ref[e, j:j + 1, :] = jnp.zeros((1, 2 * n), F32)
        cpow_ref[e] = cl_s[p:(SSM_T + 1) * p, :].astype(cpow_ref.dtype)
        bpow_ref[e] = bl_s[...].astype(bpow_ref.dtype)
        bbcat = bl_s[(SSM_T - 1) * p:SSM_T * p, :]
        kt = lax.dot_general(bbcat, cl_s[0:SSM_T * p, :], (((1,), (1,)), ((), ())),
                             preferred_element_type=F32, precision=lax.Precision.HIGHEST)
        for s in range(SSM_T):
            blk = pltpu.roll(kt, s * p, axis=1) if s else kt
            toep_ref[e, s * p:(s + 1) * p, :] = jnp.where(lane >= s * p, blk, 0.0).astype(toep_ref.dtype)


def _ssm_prep(a_re, a_im, log_dt, b_re, b_im, c_re, c_im):
    n, p = SSM_STATE, SSM_GROUP
    gg = a_re.shape[0] * a_re.shape[1]
    gb = PREP_GB
    lanes = SSM_LANES
    row = lambda a: a.reshape(gg, 1, n)
    ldt = jnp.broadcast_to(log_dt.reshape(gg, 1), (gg, n))
    bt = lambda b: jnp.transpose(b.reshape(gg, n, p), (0, 2, 1))
    rspec = pl.BlockSpec((gb, 1, n), lambda g: (g, 0, 0))
    pspec = pl.BlockSpec((gb, p, n), lambda g: (g, 0, 0))
    ospec = lambda a, b: pl.BlockSpec((gb, a, b), lambda g: (g, 0, 0))
    return pl.pallas_call(
        _ssm_prep_kernel,
        grid=(gg // gb,),
        in_specs=[rspec, rspec, rspec, pspec, pspec, pspec, pspec],
        out_specs=[ospec(lanes, lanes), ospec(lanes, 2 * n), ospec(lanes, 2 * n),
                   ospec(SUBLANES, 2 * n), ospec(SUBLANES, 2 * n)],
        out_shape=[
            jax.ShapeDtypeStruct((gg, lanes, lanes), BF16),
            jax.ShapeDtypeStruct((gg, lanes, 2 * n), BF16),
            jax.ShapeDtypeStruct((gg, lanes, 2 * n), BF16),
            jax.ShapeDtypeStruct((gg, SUBLANES, 2 * n), F32),
            jax.ShapeDtypeStruct((gg, SUBLANES, 2 * n), F32),
        ],
        scratch_shapes=[pltpu.VMEM(((SSM_T + 1) * p, 2 * n), F32), pltpu.VMEM((SSM_T * p, 2 * n), F32)],
        compiler_params=_cparams(("arbitrary",)),
        name="ssm_prep",
    )(row(a_re), row(a_im), row(ldt), c_re.reshape(gg, p, n), c_im.reshape(gg, p, n), bt(b_re), bt(b_im))


SSM_GB = LANES // SSM_GROUP
SSM_RB = BF16_ROWS


def _piece_transpose(xs, masks):
    n = len(xs)
    w = LANES // n
    rolled = []
    for k in range(n):
        wk = xs[k % n]
        for a in range(1, n):
            wk = jnp.where(masks[a], xs[(a + k) % n], wk)
        rolled.append(pltpu.roll(wk, w * k, axis=1) if k else wk)
    ys = []
    for a in range(n):
        ya = rolled[(0 - a) % n]
        for b in range(1, n):
            ya = jnp.where(masks[b], rolled[(b - a) % n], ya)
        ys.append(ya)
    return ys


def _ssm_kernel(*refs, nchunk, pad, n_side):
    u_ref, toep_ref, bpow_ref, cpow_ref, lama_ref, lamb_ref = refs[:6]
    side_in = refs[6:6 + n_side]
    y_ref = refs[6 + n_side]
    side_out = refs[7 + n_side:7 + 2 * n_side]
    up_ref, yp_ref, xr_ref, xi_ref = refs[7 + 2 * n_side:]
    _side_cast(side_in, side_out)
    rows = up_ref.shape[1]
    n2 = 2 * SSM_STATE
    t = SSM_T
    lane = lax.broadcasted_iota(jnp.int32, (1, LANES), 1)
    masks = [lane // SSM_GROUP == a for a in range(SSM_GB)]
    nhalf = t // SSM_GB

    def to_chunks(rb, carry):
        r = pl.multiple_of(rb * SSM_RB, SSM_RB)
        for hf in range(nhalf):
            xs = [u_ref[pl.ds(r * t + hf * SSM_GB + b, SSM_RB, stride=t), :] for b in range(SSM_GB)]
            ys = _piece_transpose(xs, masks)
            for a in range(SSM_GB):
                up_ref[a, pl.ds(r, SSM_RB), hf * LANES:(hf + 1) * LANES] = ys[a].astype(up_ref.dtype)
        return carry
    lax.fori_loop(0, rows // SSM_RB, to_chunks, 0, unroll=2)

    cidx = lax.broadcasted_iota(jnp.int32, (rows, 1), 0) % nchunk
    lo = lane < SSM_STATE
    swap = lambda v: pltpu.roll(v, SSM_STATE, axis=1)
    nt_dot = lambda x, wt: lax.dot_general(x, wt, (((1,), (1,)), ((), ())), preferred_element_type=F32)
    xr_ref[0:pad, :] = jnp.zeros((pad, n2), F32)
    xi_ref[0:pad, :] = jnp.zeros((pad, n2), F32)
    for a in range(0, SSM_GB, 2):
        b = a + 1
        ua, ub = up_ref[a], up_ref[b]
        sa = jnp.dot(ua, bpow_ref[a], preferred_element_type=F32)
        sb = jnp.dot(ub, bpow_ref[b], preferred_element_type=F32)
        xr_ref[pad:pad + rows, :] = jnp.where(lo, sa, swap(sb))
        xi_ref[pad:pad + rows, :] = jnp.where(lo, swap(sa), sb)
        for j in range(SSM_SCAN_STEPS):
            sh = 1 << j
            lr = jnp.where(lo, lama_ref[a, j:j + 1, :], lama_ref[b, j:j + 1, :])
            li = jnp.where(lo, -lamb_ref[a, j:j + 1, :], lamb_ref[b, j:j + 1, :])
            pr = jnp.where(cidx >= sh, xr_ref[pad - sh:pad - sh + rows, :], 0.0)
            pi = jnp.where(cidx >= sh, xi_ref[pad - sh:pad - sh + rows, :], 0.0)
            cr = xr_ref[pad:pad + rows, :]
            ci = xi_ref[pad:pad + rows, :]
            xr_ref[pad:pad + rows, :] = cr + pr * lr - pi * li
            xi_ref[pad:pad + rows, :] = ci + pi * lr + pr * li
        pr = jnp.where(cidx >= 1, xr_ref[pad - 1:pad - 1 + rows, :], 0.0)
        pi = jnp.where(cidx >= 1, xi_ref[pad - 1:pad - 1 + rows, :], 0.0)
        xa = jnp.where(lo, pr, swap(pi)).astype(BF16)
        xb = jnp.where(lo, swap(pr), pi).astype(BF16)
        yp_ref[a] = jnp.dot(ua, toep_ref[a], preferred_element_type=F32) + nt_dot(xa, cpow_ref[a])
        yp_ref[b] = jnp.dot(ub, toep_ref[b], preferred_element_type=F32) + nt_dot(xb, cpow_ref[b])

    def to_tokens(rb, carry):
        r = pl.multiple_of(rb * SUBLANES, SUBLANES)
        for hf in range(nhalf):
            ys = [yp_ref[a, pl.ds(r, SUBLANES), hf * LANES:(hf + 1) * LANES] for a in range(SSM_GB)]
            xs = _piece_transpose(ys, masks)
            for b in range(SSM_GB):
                y_ref[pl.ds(r * t + hf * SSM_GB + b, SUBLANES, stride=t), :] = xs[b]
        return carry
    lax.fori_loop(0, rows // SUBLANES, to_tokens, 0, unroll=2)


def _ssm_scan(pm, toep, bpow, cpow, lama, lamb, layer, side_weights, *, nchunk):
    m = pm.shape[0]
    rows = m // SSM_T
    lanes = SSM_LANES
    n2 = 2 * SSM_STATE
    pad = 1 << (SSM_SCAN_STEPS - 1)
    nblk = W_BRANCH // LANES
    gspec = lambda a, b: pl.BlockSpec((SSM_GB, a, b), lambda i: (layer * nblk + i, 0, 0))
    s_in, s_out, s_shape = _side_cast_specs(side_weights, layer, nblk, lambda i: i)
    outs = pl.pallas_call(
        functools.partial(_ssm_kernel, nchunk=nchunk, pad=pad, n_side=len(side_weights)),
        grid=(nblk,),
        in_specs=[pl.BlockSpec((m, LANES), lambda i: (0, i)),
                  gspec(lanes, lanes), gspec(lanes, n2), gspec(lanes, n2),
                  gspec(SUBLANES, n2), gspec(SUBLANES, n2)] + s_in,
        out_specs=[pl.BlockSpec((m, LANES), lambda i: (0, i))] + s_out,
        out_shape=[jax.ShapeDtypeStruct((m, W_BRANCH), F32)] + s_shape,
        scratch_shapes=[pltpu.VMEM((SSM_GB, rows, lanes), BF16), pltpu.VMEM((SSM_GB, rows, lanes), F32),
                        pltpu.VMEM((pad + rows, n2), F32), pltpu.VMEM((pad + rows, n2), F32)],
        compiler_params=_cparams(("arbitrary",)),
        name="ssm_scan",
    )(pm, toep, bpow, cpow, lama, lamb, *side_weights)
    return outs[0], outs[1:]


def _ssm_post_kernel(yraw_ref, u_ref, d_ref, wglu_ref, o_ref):
    w = W_BRANCH
    y = yraw_ref[...] + d_ref[...] * u_ref[...]
    zg = jnp.dot(_gelu(y).astype(BF16), wglu_ref[...], preferred_element_type=F32)
    o_ref[...] = (zg[:, :w] * _sigmoid(zg[:, w:])).astype(o_ref.dtype)


def _ssm_post(yraw, u, d, wglu, *, tm=TM_POST):
    m, w = yraw.shape
    row = pl.BlockSpec((tm, w), lambda i: (i, 0))
    return pl.pallas_call(
        _ssm_post_kernel,
        grid=(m // tm,),
        in_specs=[row, row, pl.BlockSpec((1, w), lambda i: (0, 0)), pl.BlockSpec(wglu.shape, lambda i: (0, 0))],
        out_specs=row,
        out_shape=jax.ShapeDtypeStruct((m, w), BF16),
        compiler_params=_cparams(("parallel",)),
        name="ssm_post",
    )(yraw, u, d, wglu)


def _merge_kernel(*refs, n_side):
    nb = N_BRANCH
    h_ref = refs[0]
    ys = refs[1:1 + nb]
    wg = refs[1 + nb:1 + 2 * nb]
    wb = refs[1 + 2 * nb:1 + 3 * nb]
    side_in = refs[1 + 3 * nb:1 + 3 * nb + n_side]
    o_ref = refs[1 + 3 * nb + n_side]
    side_out = refs[2 + 3 * nb + n_side:2 + 3 * nb + 2 * n_side]
    h = h_ref[...]
    acc = None
    for k in range(nb):
        gate = jnp.dot(h, wg[k][...], preferred_element_type=F32)
        yb = jnp.dot(ys[k][...], wb[k][0], preferred_element_type=F32)
        term = yb * _sigmoid(gate)
        acc = term if acc is None else acc + term
    o_ref[...] = acc.astype(o_ref.dtype)
    _side_cast(side_in, side_out)


def _merge(h, ys, wgate, wbranch, side_weights, side_layer, *, tm=TM_MERGE, tn=TN_MERGE):
    m, d = h.shape
    w = W_BRANCH
    nj = d // tn
    j0 = OFF_GATE // tn
    gate_spec = lambda k: pl.BlockSpec((None, d, tn), lambda i, j: (0, 0, j0 + k * nj + j))
    br_spec = lambda k: pl.BlockSpec((1, w, tn), lambda i, j: (k, 0, j))
    s_in, s_out, s_shape = _side_cast_specs(side_weights, side_layer, (m // tm) * nj, lambda i, j: i * nj + j)
    outs = pl.pallas_call(
        functools.partial(_merge_kernel, n_side=len(side_weights)),
        grid=(m // tm, nj),
        in_specs=[pl.BlockSpec((tm, d), lambda i, j: (i, 0))]
        + [pl.BlockSpec((tm, w), lambda i, j: (i, 0))] * N_BRANCH
        + [gate_spec(k) for k in range(N_BRANCH)]
        + [br_spec(k) for k in range(N_BRANCH)]
        + s_in,
        out_specs=[pl.BlockSpec((tm, tn), lambda i, j: (i, j))] + s_out,
        out_shape=[jax.ShapeDtypeStruct((m, d), BF16)] + s_shape,
        compiler_params=_cparams(("arbitrary", "arbitrary")),
        name="gated_merge",
    )(h, *ys, *([wgate] * N_BRANCH), *([wbranch] * N_BRANCH), *side_weights)
    return outs[0], outs[1:]


def _out_kernel(m_ref, w_ref, x_ref, g_ref, gn_ref, o_ref, hn_ref):
    half = m_ref.shape[0] // 2
    for r0 in (0, half):
        rs = slice(r0, r0 + half)
        mix = jnp.dot(m_ref[rs, :], w_ref[...], preferred_element_type=F32)
        x1 = x_ref[rs, :] + _rms(mix, g_ref[...])
        o_ref[rs, :] = x1
        hn_ref[rs, :] = _rms(x1, gn_ref[...]).astype(hn_ref.dtype)


def _out_proj(merged, w_o, x2, g, g_next, layer, *, tm=TM_OUT):
    m, d = x2.shape
    row = pl.BlockSpec((tm, d), lambda i: (i, 0))
    vec = pl.BlockSpec((1, d), lambda i: (0, 0))
    return pl.pallas_call(
        _out_kernel,
        grid=(m // tm,),
        in_specs=[row, pl.BlockSpec((None, d, d), lambda i: (layer, 0, 0), pipeline_mode=pl.Buffered(1)),
                  row, vec, vec],
        out_specs=[row, row],
        out_shape=[jax.ShapeDtypeStruct((m, d), F32), jax.ShapeDtypeStruct((m, d), BF16)],
        compiler_params=_cparams(("parallel",)),
        name="out_proj",
    )(merged, w_o, x2, g, g_next)


def _ffn_kernel(*refs, rows, n_side):
    x_ref, h_ref, w1_ref, w2_ref, gpost_ref = refs[:5]
    side_in = refs[5:5 + n_side]
    o_ref = refs[5 + n_side]
    side_out = refs[6 + n_side:6 + 2 * n_side]
    j = pl.program_id(1)
    last = pl.num_programs(1) - 1
    tm, d = o_ref.shape
    nw = 512

    def step(first, final):
        hrows = tm // 2 if final else tm
        for r0 in range(0, tm, hrows):
            rs = slice(r0, r0 + hrows)
            a = jnp.dot(h_ref[rs, :], w1_ref[...], preferred_element_type=F32)
            a = jnp.maximum(a, 0.0)
            a = (a * a).astype(BF16)
            for c in range(d // nw):
                cs = slice(c * nw, (c + 1) * nw)
                part = jnp.dot(a, w2_ref[:, cs], preferred_element_type=F32)
                if first:
                    o_ref[rs, cs] = part
                else:
                    o_ref[rs, cs] += part
            if final:
                for q0 in range(r0, r0 + hrows, rows):
                    qs = slice(q0, q0 + rows)
                    o_ref[qs, :] = x_ref[qs, :] + _rms(o_ref[qs, :], gpost_ref[...])
        _side_cast(side_in, side_out)

    pl.when(j == 0)(functools.partial(step, True, False))
    pl.when((j > 0) & (j < last))(functools.partial(step, False, False))
    pl.when(j == last)(functools.partial(step, False, True))


def _ffn(x2, h2, w1, w2, gpost, side_weights, side_layer, *, tm=TM_FFN, tf=TF_FFN):
    m, d = x2.shape
    f = w1.shape[2]
    nj = f // tf
    s_in, s_out, s_shape = _side_cast_specs(side_weights, side_layer, (m // tm) * nj, lambda i, j: i * nj + j)
    outs = pl.pallas_call(
        functools.partial(_ffn_kernel, rows=NORM_ROWS, n_side=len(side_weights)),
        grid=(m // tm, nj),
        in_specs=[
            pl.BlockSpec((tm, d), lambda i, j: (i, 0)),
            pl.BlockSpec((tm, d), lambda i, j: (i, 0)),
            pl.BlockSpec((None, d, tf), lambda i, j: (0, 0, j)),
            pl.BlockSpec((None, tf, d), lambda i, j: (0, j, 0)),
            pl.BlockSpec((1, d), lambda i, j: (0, 0)),
        ] + s_in,
        out_specs=[pl.BlockSpec((tm, d), lambda i, j: (i, 0))] + s_out,
        out_shape=[jax.ShapeDtypeStruct((m, d), F32)] + s_shape,
        compiler_params=_cparams(("arbitrary", "arbitrary")),
        name="ffn",
    )(x2, h2, w1, w2, gpost, *side_weights)
    return outs[0], outs[1:]


def kernel(x, g_pre_mix, w_in, ssm_a_re, ssm_a_im, ssm_log_dt, ssm_b_re, ssm_b_im, ssm_c_re, ssm_c_im,
           ssm_d, ssm_w_glu, pool_w, pool_scale, conv_w, conv_b, conv_ln_g, conv_ln_b, gmlp_ln_g,
           gmlp_ln_b, gmlp_ws, gmlp_bs, w_branch, w_o, g_post_mix, g_pre_mlp, w_ff1, w_ff2, g_post_mlp):
    nb, seq, d = x.shape
    m = nb * seq
    depth = w_in.shape[0]
    nchunk = seq // SSM_T
    assert d == D_MODEL and nchunk == 1 << SSM_SCAN_STEPS

    toep, bpow, cpow, lama, lamb = _ssm_prep(ssm_a_re, ssm_a_im, ssm_log_dt, ssm_b_re, ssm_b_im,
                                             ssm_c_re, ssm_c_im)

    w_br2 = w_branch.reshape(depth, N_BRANCH * W_BRANCH, d)
    w_in_b = _cast_bf16(w_in, 0, CAST_ROWS)

    row = lambda a: a.reshape(1, -1)
    x2 = x.reshape(m, d)
    for l in range(depth):
        h, u, y_pool, y_conv, y_gmlp = _proj_mix(
            x2, row(g_pre_mix[l]), w_in_b, 0, pool_w[l].astype(BF16), row(pool_scale[l]),
            jnp.broadcast_to(conv_w[l][:, None, :], (CONV_WIDTH, SUBLANES, W_BRANCH)),
            row(conv_b[l]), row(conv_ln_g[l]), row(conv_ln_b[l]), row(gmlp_ln_g[l]), row(gmlp_ln_b[l]),
            gmlp_ws[l], jnp.transpose(gmlp_bs[l]), seq=seq)
        yraw, (w_br_b, w_o_b) = _ssm_scan(u, toep, bpow, cpow, lama, lamb, l, (w_br2, w_o), nchunk=nchunk)
        y_ssm = _ssm_post(yraw, u, row(ssm_d[l]), ssm_w_glu[l].astype(BF16))
        merged, (w_ff1_b, w_ff2_b) = _merge(h, (y_ssm, y_pool, y_conv, y_gmlp), w_in_b,
                                            w_br_b.reshape(N_BRANCH, W_BRANCH, d), (w_ff1, w_ff2), l)
        x2, h2 = _out_proj(merged, w_o_b, x2, row(g_post_mix[l]), row(g_pre_mlp[l]), 0)
        nxt = (w_in,) if l + 1 < depth else ()
        x2, nxt_b = _ffn(x2, h2, w_ff1_b, w_ff2_b, row(g_post_mlp[l]), nxt, l + 1)
        if nxt:
            (w_in_b,) = nxt_b
    return x2.reshape(nb, seq, d)
```

```python
import functools

import jax
import jax.numpy as jnp
from jax import lax
from jax.experimental import pallas as pl
from jax.experimental.pallas import tpu as pltpu

D_MODEL = 2048
N_BRANCH = 4
W_BRANCH = D_MODEL // 4
SSM_GROUP = 16
SSM_STATE = 64
POOL_WINDOWS = (2, 4, 8, 16)
POOL_GW = W_BRANCH // len(POOL_WINDOWS)
CONV_WIDTH = 31
GMLP_CHUNK = 128
GMLP_HEADS = 4
GMLP_HD = W_BRANCH // GMLP_HEADS
EPS = 1e-6

LANES = 128
SUBLANES = 8
BF16_ROWS = 2 * SUBLANES
V7X_VMEM_BYTES = 64 * 1024 * 1024

OFF_SSM = 0
OFF_POOL = OFF_SSM + W_BRANCH
OFF_CONV = OFF_POOL + W_BRANCH
OFF_GMLP = OFF_CONV + 2 * W_BRANCH
OFF_GATE = OFF_GMLP + 2 * W_BRANCH

SSM_T = 16
SSM_LANES = SSM_T * SSM_GROUP
SSM_SCAN_STEPS = 7
CONV_HALO = 32
POOL_HALO = SUBLANES * len(POOL_WINDOWS)
assert all(win == 2 << k for k, win in enumerate(POOL_WINDOWS))
CONV_RB = 64

VMEM_LIMIT = V7X_VMEM_BYTES - 8 * 1024 * 1024
TM_PROJ = 512
TM_MERGE, TN_MERGE = 1024, 256
TM_OUT = 512
TM_FFN, TF_FFN = 1024, 1024
TM_POST = 512
CAST_ROWS = 256
NORM_ROWS = 128
BF16 = jnp.bfloat16
F32 = jnp.float32


def _cparams(sem):
    return pltpu.CompilerParams(dimension_semantics=sem, vmem_limit_bytes=VMEM_LIMIT)


def _sigmoid(x):
    return 0.5 * (1.0 + jnp.tanh(0.5 * x))


def _gelu(x):
    c = 0.7978845608028654
    return 0.5 * x * (1.0 + jnp.tanh(c * (x + 0.044715 * (x * x * x))))


def _rms(x, g):
    ms = jnp.mean(x * x, axis=-1, keepdims=True)
    return x * lax.rsqrt(ms + EPS) * g


def _layer_norm(x, g, b):
    mu = jnp.mean(x, axis=-1, keepdims=True)
    xc = x - mu
    var = jnp.mean(xc * xc, axis=-1, keepdims=True)
    return xc * lax.rsqrt(var + EPS) * g + b


def _cast_kernel(w_ref, o_ref):
    o_ref[...] = w_ref[...].astype(o_ref.dtype)


def _cast_bf16(w, layer, rows):
    _, r, c = w.shape
    return pl.pallas_call(
        _cast_kernel,
        grid=(r // rows,),
        in_specs=[pl.BlockSpec((1, rows, c), lambda i: (layer, i, 0))],
        out_specs=pl.BlockSpec((1, rows, c), lambda i: (0, i, 0)),
        out_shape=jax.ShapeDtypeStruct((1, r, c), BF16),
        compiler_params=_cparams(("parallel",)),
        name="cast_bf16",
    )(w)


def _side_cast_specs(weights, layer, nsteps, step_of):
    in_specs, out_specs, out_shapes = [], [], []
    for w in weights:
        _, r, c = w.shape
        assert r % nsteps == 0 and (r // nsteps) % BF16_ROWS == 0
        rows = r // nsteps
        in_specs.append(pl.BlockSpec((None, rows, c), lambda *ids: (layer, step_of(*ids), 0)))
        out_specs.append(pl.BlockSpec((None, rows, c), lambda *ids: (0, step_of(*ids), 0)))
        out_shapes.append(jax.ShapeDtypeStruct((1, r, c), BF16))
    return in_specs, out_specs, out_shapes


def _side_cast(in_refs, out_refs):
    for wi, wo in zip(in_refs, out_refs):
        wo[...] = wi[...].astype(wo.dtype)


def _proj_mix_kernel(x_ref, g_ref, w_ref, poolw_ref, pscale_ref, convw_ref, convb_ref, clg_ref, clb_ref,
                     glg_ref, glb_ref, ws_ref, bst_ref,
                     h_ref, u_ref, ypool_ref, yconv_ref, ygmlp_ref, vbuf, pbuf, sbuf, qbuf, ugbuf, vgbuf,
                     *, rows, tiles_per_seq):
    c = pl.program_id(0) % tiles_per_seq
    tc = x_ref.shape[0]
    w = W_BRANCH

    @pl.when(c == 0)
    def _():
        vbuf[0:CONV_HALO, :] = jnp.zeros((CONV_HALO, w), F32)
        pbuf[0:POOL_HALO, :] = jnp.zeros((POOL_HALO, w), F32)

    def body(i, carry):
        r = pl.multiple_of(i * rows, rows)
        h_ref[pl.ds(r, rows), :] = _rms(x_ref[pl.ds(r, rows), :], g_ref[...]).astype(h_ref.dtype)
        return carry
    lax.fori_loop(0, tc // rows, body, 0, unroll=2)

    def proj(off, width, r0=0, nrows=tc):
        return jnp.dot(h_ref[r0:r0 + nrows, :], w_ref[:, off:off + width], preferred_element_type=F32)

    half = tc // 2
    for r0 in range(0, tc, half):
        gm = proj(OFF_GMLP, 2 * w, r0, half)
        ugbuf[r0:r0 + half, :] = _gelu(gm[:, :w])
        vgbuf[r0:r0 + half, :] = _layer_norm(_gelu(gm[:, w:]), glg_ref[...], glb_ref[...]).astype(vgbuf.dtype)
    for r0 in range(0, tc, half):
        cv = proj(OFF_CONV, 2 * w, r0, half)
        vbuf[CONV_HALO + r0:CONV_HALO + r0 + half, :] = cv[:, :w] * _sigmoid(cv[:, w:])
    pbuf[POOL_HALO:POOL_HALO + tc, :] = proj(OFF_POOL, w)
    u_ref[...] = proj(OFF_SSM, w)

    src = pbuf
    for k in range(len(POOL_WINDOWS)):
        lo, sh, c0 = SUBLANES * (k + 1), 1 << k, k * POOL_GW
        n = POOL_HALO + tc - lo
        a_hi = src[lo:lo + n, c0:] if k == 0 else src[k - 1, lo:lo + n, c0:]
        a_lo = src[lo - sh:lo - sh + n, c0:] if k == 0 else src[k - 1, lo - sh:lo - sh + n, c0:]
        qbuf[k, lo:lo + n, c0:] = a_hi + a_lo
        src = qbuf
    t = c * tc + lax.broadcasted_iota(jnp.int32, (tc, 1), 0)
    for gi, win in enumerate(POOL_WINDOWS):
        cols = slice(gi * POOL_GW, (gi + 1) * POOL_GW)
        tok = pbuf[POOL_HALO:POOL_HALO + tc, cols]
        s = qbuf[gi, POOL_HALO:POOL_HALO + tc, cols]
        count = jnp.minimum(t + 1, win).astype(F32)
        pooled = s / count - tok
        mixed = jnp.dot(pooled.astype(BF16), poolw_ref[gi], preferred_element_type=F32)
        ypool_ref[:, cols] = (mixed * pscale_ref[:, cols]).astype(ypool_ref.dtype)
    pbuf[0:POOL_HALO, :] = pbuf[tc:tc + POOL_HALO, :]

    sl = SUBLANES
    span = CONV_HALO - sl
    for b in range(1, sl):
        sbuf[b - 1] = vbuf[sl - b:sl - b + span + tc, :]
    for rb in range(tc // CONV_RB):
        r0 = rb * CONV_RB
        acc = jnp.broadcast_to(convb_ref[...], (CONV_RB, w))
        for b in range(sl):
            for a in range((CONV_WIDTH - 1 - b) // sl + 1):
                k = CONV_WIDTH - 1 - (sl * a + b)
                lo = span - sl * a + r0
                tap = sbuf[b - 1, lo:lo + CONV_RB, :] if b else vbuf[sl + lo:sl + lo + CONV_RB, :]
                wk = convw_ref[k]
                acc = acc + (tap.reshape(CONV_RB // sl, sl, w) * wk[None]).reshape(CONV_RB, w)
        yn = _layer_norm(acc, clg_ref[...], clb_ref[...])
        yconv_ref[r0:r0 + CONV_RB, :] = (yn * _sigmoid(yn)).astype(yconv_ref.dtype)
    vbuf[0:CONV_HALO, :] = vbuf[tc:tc + CONV_HALO, :]

    ri = lax.broadcasted_iota(jnp.int32, (GMLP_CHUNK, GMLP_CHUNK), 0)
    ci = lax.broadcasted_iota(jnp.int32, (GMLP_CHUNK, GMLP_CHUNK), 1)
    causal = ri >= ci
    for hd in range(GMLP_HEADS):
        wsm = jnp.where(causal, ws_ref[hd], 0.0).astype(BF16)
        cols = slice(hd * GMLP_HD, (hd + 1) * GMLP_HD)
        for ch in range(tc // GMLP_CHUNK):
            rws = slice(ch * GMLP_CHUNK, (ch + 1) * GMLP_CHUNK)
            sv = jnp.dot(wsm, vgbuf[rws, cols], preferred_element_type=F32) + bst_ref[:, hd:hd + 1]
            ygmlp_ref[rws, cols] = (ugbuf[rws, cols] * sv).astype(ygmlp_ref.dtype)


def _proj_mix(x2, g, w_in, layer, poolw, pscale, convw, convb, clg, clb, glg, glb, ws, bst, *, seq, tm=TM_PROJ):
    m, d = x2.shape
    n = OFF_GATE
    w = W_BRANCH
    assert seq % tm == 0 and tm % GMLP_CHUNK == 0 and tm % CONV_RB == 0
    row = lambda width: pl.BlockSpec((tm, width), lambda i: (i, 0))
    full = lambda a: pl.BlockSpec(a.shape, lambda i: (0,) * a.ndim)
    args = (poolw, pscale, convw, convb, clg, clb, glg, glb, ws, bst)
    return pl.pallas_call(
        functools.partial(_proj_mix_kernel, rows=NORM_ROWS, tiles_per_seq=seq // tm),
        grid=(m // tm,),
        in_specs=[row(d), full(g),
                  pl.BlockSpec((None, d, n), lambda i: (layer, 0, 0), pipeline_mode=pl.Buffered(1))]
        + [full(a) for a in args],
        out_specs=[row(d), row(w), row(w), row(w), row(w)],
        out_shape=[jax.ShapeDtypeStruct((m, d), BF16), jax.ShapeDtypeStruct((m, w), F32)]
        + [jax.ShapeDtypeStruct((m, w), BF16)] * 3,
        scratch_shapes=[pltpu.VMEM((CONV_HALO + tm, w), F32), pltpu.VMEM((POOL_HALO + tm, w), F32),
                        pltpu.VMEM((SUBLANES - 1, CONV_HALO - SUBLANES + tm, w), F32),
                        pltpu.VMEM((len(POOL_WINDOWS), POOL_HALO + tm, w), F32),
                        pltpu.VMEM((tm, w), F32), pltpu.VMEM((tm, w), BF16)],
        compiler_params=_cparams(("arbitrary",)),
        name="proj_mix",
    )(x2, g, w_in, *args)


PREP_GB = 8


def _ssm_prep_kernel(ar_ref, ai_ref, ldt_ref, cre_ref, cim_ref, bre_ref, bim_ref,
                     toep_ref, bpow_ref, cpow_ref, lama_ref, lamb_ref, cl_s, bl_s):
    n = SSM_STATE
    p = SSM_GROUP
    lanes = SSM_LANES
    lane = lax.broadcasted_iota(jnp.int32, (1, lanes), 1)
    for e in range(PREP_GB):
        ar = ar_ref[e]
        ai = ai_ref[e]
        dt = jnp.exp(ldt_ref[e])
        mag = jnp.exp(ar * dt)
        ang = ai * dt
        lr = mag * jnp.cos(ang)
        li = mag * jnp.sin(ang)
        den = ar * ar + ai * ai
        fr = ((lr - 1.0) * ar + li * ai) / den
        fi = (li * ar - (lr - 1.0) * ai) / den
        cre = cre_ref[e]
        cim = cim_ref[e]
        br = bre_ref[e]
        bi = bim_ref[e]
        bbr = fr * br - fi * bi
        bbi = fr * bi + fi * br
        pr = jnp.ones_like(lr)
        pi = jnp.zeros_like(lr)
        for k in range(SSM_T + 1):
            cl_s[k * p:(k + 1) * p, 0:n] = cre * pr - cim * pi
            cl_s[k * p:(k + 1) * p, n:2 * n] = -(cre * pi + cim * pr)
            if k < SSM_T:
                s = SSM_T - 1 - k
                bl_s[s * p:(s + 1) * p, 0:n] = pr * bbr - pi * bbi
                bl_s[s * p:(s + 1) * p, n:2 * n] = pr * bbi + pi * bbr
                pr, pi = pr * lr - pi * li, pr * li + pi * lr
        qr, qi = pr, pi
        for j in range(SUBLANES):
            if j < SSM_SCAN_STEPS:
                lama_ref[e, j:j + 1, 0:n] = qr
                lama_ref[e, j:j + 1, n:2 * n] = qr
                lamb_ref[e, j:j + 1, 0:n] = -qi
                lamb_ref[e, j:j + 1, n:2 * n] = qi
                qr, qi = qr * qr - qi * qi, 2.0 * (qr * qi)
            else:
                lama_ref[e, j:j + 1, :] = jnp.zeros((1, 2 * n), F32)
                lamb_ref[e, j:j + 1, :] = jnp.zeros((1, 2 * n), F32)
        cpow_ref[e] = cl_s[p:(SSM_T + 1) * p, :].astype(cpow_ref.dtype)
        bpow_ref[e] = bl_s[...].astype(bpow_ref.dtype)
        bbcat = bl_s[(SSM_T - 1) * p:SSM_T * p, :]
        kt = lax.dot_general(bbcat, cl_s[0:SSM_T * p, :], (((1,), (1,)), ((), ())),
                             preferred_element_type=F32, precision=lax.Precision.HIGHEST)
        for s in range(SSM_T):
            blk = pltpu.roll(kt, s * p, axis=1) if s else kt
            toep_ref[e, s * p:(s + 1) * p, :] = jnp.where(lane >= s * p, blk, 0.0).astype(toep_ref.dtype)


def _ssm_prep(a_re, a_im, log_dt, b_re, b_im, c_re, c_im):
    n, p = SSM_STATE, SSM_GROUP
    gg = a_re.shape[0] * a_re.shape[1]
    gb = PREP_GB
    lanes = SSM_LANES
    row = lambda a: a.reshape(gg, 1, n)
    ldt = jnp.broadcast_to(log_dt.reshape(gg, 1), (gg, n))
    bt = lambda b: jnp.transpose(b.reshape(gg, n, p), (0, 2, 1))
    rspec = pl.BlockSpec((gb, 1, n), lambda g: (g, 0, 0))
    pspec = pl.BlockSpec((gb, p, n), lambda g: (g, 0, 0))
    ospec = lambda a, b: pl.BlockSpec((gb, a, b), lambda g: (g, 0, 0))
    return pl.pallas_call(
        _ssm_prep_kernel,
        grid=(gg // gb,),
        in_specs=[rspec, rspec, rspec, pspec, pspec, pspec, pspec],
        out_specs=[ospec(lanes, lanes), ospec(lanes, 2 * n), ospec(lanes, 2 * n),
                   ospec(SUBLANES, 2 * n), ospec(SUBLANES, 2 * n)],
        out_shape=[
            jax.ShapeDtypeStruct((gg, lanes, lanes), BF16),
            jax.ShapeDtypeStruct((gg, lanes, 2 * n), BF16),
            jax.ShapeDtypeStruct((gg, lanes, 2 * n), BF16),
            jax.ShapeDtypeStruct((gg, SUBLANES, 2 * n), F32),
            jax.ShapeDtypeStruct((gg, SUBLANES, 2 * n), F32),
        ],
        scratch_shapes=[pltpu.VMEM(((SSM_T + 1) * p, 2 * n), F32), pltpu.VMEM((SSM_T * p, 2 * n), F32)],
        compiler_params=_cparams(("arbitrary",)),
        name="ssm_prep",
    )(row(a_re), row(a_im), row(ldt), c_re.reshape(gg, p, n), c_im.reshape(gg, p, n), bt(b_re), bt(b_im))


SSM_GB = LANES // SSM_GROUP
SSM_RB = BF16_ROWS


def _piece_transpose(xs, masks):
    n = len(xs)
    w = LANES // n
    rolled = []
    for k in range(n):
        wk = xs[k % n]
        for a in range(1, n):
            wk = jnp.where(masks[a], xs[(a + k) % n], wk)
        rolled.append(pltpu.roll(wk, w * k, axis=1) if k else wk)
    ys = []
    for a in range(n):
        ya = rolled[(0 - a) % n]
        for b in range(1, n):
            ya = jnp.where(masks[b], rolled[(b - a) % n], ya)
        ys.append(ya)
    return ys


def _ssm_kernel(*refs, nchunk, pad, n_side):
    u_ref, toep_ref, bpow_ref, cpow_ref, lama_ref, lamb_ref = refs[:6]
    side_in = refs[6:6 + n_side]
    y_ref = refs[6 + n_side]
    side_out = refs[7 + n_side:7 + 2 * n_side]
    up_ref, yp_ref, xr_ref, xi_ref = refs[7 + 2 * n_side:]
    _side_cast(side_in, side_out)
    rows = up_ref.shape[1]
    n2 = 2 * SSM_STATE
    t = SSM_T
    lane = lax.broadcasted_iota(jnp.int32, (1, LANES), 1)
    masks = [lane // SSM_GROUP == a for a in range(SSM_GB)]
    nhalf = t // SSM_GB

    def to_chunks(rb, carry):
        r = pl.multiple_of(rb * SSM_RB, SSM_RB)
        for hf in range(nhalf):
            xs = [u_ref[pl.ds(r * t + hf * SSM_GB + b, SSM_RB, stride=t), :] for b in range(SSM_GB)]
            ys = _piece_transpose(xs, masks)
            for a in range(SSM_GB):
                up_ref[a, pl.ds(r, SSM_RB), hf * LANES:(hf + 1) * LANES] = ys[a].astype(up_ref.dtype)
        return carry
    lax.fori_loop(0, rows // SSM_RB, to_chunks, 0, unroll=2)

    cidx = lax.broadcasted_iota(jnp.int32, (rows, 1), 0) % nchunk
    lo = lane < SSM_STATE
    swap = lambda v: pltpu.roll(v, SSM_STATE, axis=1)
    nt_dot = lambda x, wt: lax.dot_general(x, wt, (((1,), (1,)), ((), ())), preferred_element_type=F32)
    xr_ref[0:pad, :] = jnp.zeros((pad, n2), F32)
    xi_ref[0:pad, :] = jnp.zeros((pad, n2), F32)
    for a in range(0, SSM_GB, 2):
        b = a + 1
        ua, ub = up_ref[a], up_ref[b]
        sa = jnp.dot(ua, bpow_ref[a], preferred_element_type=F32)
        sb = jnp.dot(ub, bpow_ref[b], preferred_element_type=F32)
        xr_ref[pad:pad + rows, :] = jnp.where(lo, sa, swap(sb))
        xi_ref[pad:pad + rows, :] = jnp.where(lo, swap(sa), sb)
        for j in range(SSM_SCAN_STEPS):
            sh = 1 << j
            lr = jnp.where(lo, lama_ref[a, j:j + 1, :], lama_ref[b, j:j + 1, :])
            li = jnp.where(lo, -lamb_ref[a, j:j + 1, :], lamb_ref[b, j:j + 1, :])
            pr = jnp.where(cidx >= sh, xr_ref[pad - sh:pad - sh + rows, :], 0.0)
            pi = jnp.where(cidx >= sh, xi_ref[pad - sh:pad - sh + rows, :], 0.0)
            cr = xr_ref[pad:pad + rows, :]
            ci = xi_ref[pad:pad + rows, :]
            xr_ref[pad:pad + rows, :] = cr + pr * lr - pi * li
            xi_ref[pad:pad + rows, :] = ci + pi * lr + pr * li
        pr = jnp.where(cidx >= 1, xr_ref[pad - 1:pad - 1 + rows, :], 0.0)
        pi = jnp.where(cidx >= 1, xi_ref[pad - 1:pad - 1 + rows, :], 0.0)
        xa = jnp.where(lo, pr, swap(pi)).astype(BF16)
        xb = jnp.where(lo, swap(pr), pi).astype(BF16)
        yp_ref[a] = jnp.dot(ua, toep_ref[a], preferred_element_type=F32) + nt_dot(xa, cpow_ref[a])
        yp_ref[b] = jnp.dot(ub, toep_ref[b], preferred_element_type=F32) + nt_dot(xb, cpow_ref[b])

    def to_tokens(rb, carry):
        r = pl.multiple_of(rb * SUBLANES, SUBLANES)
        for hf in range(nhalf):
            ys = [yp_ref[a, pl.ds(r, SUBLANES), hf * LANES:(hf + 1) * LANES] for a in range(SSM_GB)]
            xs = _piece_transpose(ys, masks)
            for b in range(SSM_GB):
                y_ref[pl.ds(r * t + hf * SSM_GB + b, SUBLANES, stride=t), :] = xs[b]
        return carry
    lax.fori_loop(0, rows // SUBLANES, to_tokens, 0, unroll=2)


def _ssm_scan(pm, toep, bpow, cpow, lama, lamb, layer, side_weights, *, nchunk):
    m = pm.shape[0]
    rows = m // SSM_T
    lanes = SSM_LANES
    n2 = 2 * SSM_STATE
    pad = 1 << (SSM_SCAN_STEPS - 1)
    nblk = W_BRANCH // LANES
    gspec = lambda a, b: pl.BlockSpec((SSM_GB, a, b), lambda i: (layer * nblk + i, 0, 0))
    s_in, s_out, s_shape = _side_cast_specs(side_weights, layer, nblk, lambda i: i)
    outs = pl.pallas_call(
        functools.partial(_ssm_kernel, nchunk=nchunk, pad=pad, n_side=len(side_weights)),
        grid=(nblk,),
        in_specs=[pl.BlockSpec((m, LANES), lambda i: (0, i)),
                  gspec(lanes, lanes), gspec(lanes, n2), gspec(lanes, n2),
                  gspec(SUBLANES, n2), gspec(SUBLANES, n2)] + s_in,
        out_specs=[pl.BlockSpec((m, LANES), lambda i: (0, i))] + s_out,
        out_shape=[jax.ShapeDtypeStruct((m, W_BRANCH), F32)] + s_shape,
        scratch_shapes=[pltpu.VMEM((SSM_GB, rows, lanes), BF16), pltpu.VMEM((SSM_GB, rows, lanes), F32),
                        pltpu.VMEM((pad + rows, n2), F32), pltpu.VMEM((pad + rows, n2), F32)],
        compiler_params=_cparams(("arbitrary",)),
        name="ssm_scan",
    )(pm, toep, bpow, cpow, lama, lamb, *side_weights)
    return outs[0], outs[1:]


def _ssm_post_kernel(yraw_ref, u_ref, d_ref, wglu_ref, o_ref):
    w = W_BRANCH
    y = yraw_ref[...] + d_ref[...] * u_ref[...]
    zg = jnp.dot(_gelu(y).astype(BF16), wglu_ref[...], preferred_element_type=F32)
    o_ref[...] = (zg[:, :w] * _sigmoid(zg[:, w:])).astype(o_ref.dtype)


def _ssm_post(yraw, u, d, wglu, *, tm=TM_POST):
    m, w = yraw.shape
    row = pl.BlockSpec((tm, w), lambda i: (i, 0))
    return pl.pallas_call(
        _ssm_post_kernel,
        grid=(m // tm,),
        in_specs=[row, row, pl.BlockSpec((1, w), lambda i: (0, 0)), pl.BlockSpec(wglu.shape, lambda i: (0, 0))],
        out_specs=row,
        out_shape=jax.ShapeDtypeStruct((m, w), BF16),
        compiler_params=_cparams(("parallel",)),
        name="ssm_post",
    )(yraw, u, d, wglu)


def _merge_kernel(*refs, n_side):
    nb = N_BRANCH
    h_ref = refs[0]
    ys = refs[1:1 + nb]
    wg = refs[1 + nb:1 + 2 * nb]
    wb = refs[1 + 2 * nb:1 + 3 * nb]
    side_in = refs[1 + 3 * nb:1 + 3 * nb + n_side]
    o_ref = refs[1 + 3 * nb + n_side]
    side_out = refs[2 + 3 * nb + n_side:2 + 3 * nb + 2 * n_side]
    h = h_ref[...]
    acc = None
    for k in range(nb):
        gate = jnp.dot(h, wg[k][...], preferred_element_type=F32)
        yb = jnp.dot(ys[k][...], wb[k][0], preferred_element_type=F32)
        term = yb * _sigmoid(gate)
        acc = term if acc is None else acc + term
    o_ref[...] = acc.astype(o_ref.dtype)
    _side_cast(side_in, side_out)


def _merge(h, ys, wgate, wbranch, side_weights, side_layer, *, tm=TM_MERGE, tn=TN_MERGE):
    m, d = h.shape
    w = W_BRANCH
    nj = d // tn
    j0 = OFF_GATE // tn
    gate_spec = lambda k: pl.BlockSpec((None, d, tn), lambda i, j: (0, 0, j0 + k * nj + j))
    br_spec = lambda k: pl.BlockSpec((1, w, tn), lambda i, j: (k, 0, j))
    s_in, s_out, s_shape = _side_cast_specs(side_weights, side_layer, (m // tm) * nj, lambda i, j: i * nj + j)
    outs = pl.pallas_call(
        functools.partial(_merge_kernel, n_side=len(side_weights)),
        grid=(m // tm, nj),
        in_specs=[pl.BlockSpec((tm, d), lambda i, j: (i, 0))]
        + [pl.BlockSpec((tm, w), lambda i, j: (i, 0))] * N_BRANCH
        + [gate_spec(k) for k in range(N_BRANCH)]
        + [br_spec(k) for k in range(N_BRANCH)]
        + s_in,
        out_specs=[pl.BlockSpec((tm, tn), lambda i, j: (i, j))] + s_out,
        out_shape=[jax.ShapeDtypeStruct((m, d), BF16)] + s_shape,
        compiler_params=_cparams(("arbitrary", "arbitrary")),
        name="gated_merge",
    )(h, *ys, *([wgate] * N_BRANCH), *([wbranch] * N_BRANCH), *side_weights)
    return outs[0], outs[1:]


def _out_kernel(m_ref, w_ref, x_ref, g_ref, gn_ref, o_ref, hn_ref):
    half = m_ref.shape[0] // 2
    for r0 in (0, half):
        rs = slice(r0, r0 + half)
        mix = jnp.dot(m_ref[rs, :], w_ref[...], preferred_element_type=F32)
        x1 = x_ref[rs, :] + _rms(mix, g_ref[...])
        o_ref[rs, :] = x1
        hn_ref[rs, :] = _rms(x1, gn_ref[...]).astype(hn_ref.dtype)


def _out_proj(merged, w_o, x2, g, g_next, layer, *, tm=TM_OUT):
    m, d = x2.shape
    row = pl.BlockSpec((tm, d), lambda i: (i, 0))
    vec = pl.BlockSpec((1, d), lambda i: (0, 0))
    return pl.pallas_call(
        _out_kernel,
        grid=(m // tm,),
        in_specs=[row, pl.BlockSpec((None, d, d), lambda i: (layer, 0, 0), pipeline_mode=pl.Buffered(1)),
                  row, vec, vec],
        out_specs=[row, row],
        out_shape=[jax.ShapeDtypeStruct((m, d), F32), jax.ShapeDtypeStruct((m, d), BF16)],
        compiler_params=_cparams(("parallel",)),
        name="out_proj",
    )(merged, w_o, x2, g, g_next)


def _ffn_kernel(*refs, rows, n_side):
    x_ref, h_ref, w1_ref, w2_ref, gpost_ref = refs[:5]
    side_in = refs[5:5 + n_side]
    o_ref = refs[5 + n_side]
    side_out = refs[6 + n_side:6 + 2 * n_side]
    j = pl.program_id(1)
    last = pl.num_programs(1) - 1
    tm, d = o_ref.shape
    nw = 512

    def step(first, final):
        hrows = tm // 2 if final else tm
        for r0 in range(0, tm, hrows):
            rs = slice(r0, r0 + hrows)
            a = jnp.dot(h_ref[rs, :], w1_ref[...], preferred_element_type=F32)
            a = jnp.maximum(a, 0.0)
            a = (a * a).astype(BF16)
            for c in range(d // nw):
                cs = slice(c * nw, (c + 1) * nw)
                part = jnp.dot(a, w2_ref[:, cs], preferred_element_type=F32)
                if first:
                    o_ref[rs, cs] = part
                else:
                    o_ref[rs, cs] += part
            if final:
                for q0 in range(r0, r0 + hrows, rows):
                    qs = slice(q0, q0 + rows)
                    o_ref[qs, :] = x_ref[qs, :] + _rms(o_ref[qs, :], gpost_ref[...])
        _side_cast(side_in, side_out)

    pl.when(j == 0)(functools.partial(step, True, False))
    pl.when((j > 0) & (j < last))(functools.partial(step, False, False))
    pl.when(j == last)(functools.partial(step, False, True))


def _ffn(x2, h2, w1, w2, gpost, side_weights, side_layer, *, tm=TM_FFN, tf=TF_FFN):
    m, d = x2.shape
    f = w1.shape[2]
    nj = f // tf
    s_in, s_out, s_shape = _side_cast_specs(side_weights, side_layer, (m // tm) * nj, lambda i, j: i * nj + j)
    outs = pl.pallas_call(
        functools.partial(_ffn_kernel, rows=NORM_ROWS, n_side=len(side_weights)),
        grid=(m // tm, nj),
        in_specs=[
            pl.BlockSpec((tm, d), lambda i, j: (i, 0), pipeline_mode=pl.Buffered(1)),
            pl.BlockSpec((tm, d), lambda i, j: (i, 0), pipeline_mode=pl.Buffered(1)),
            pl.BlockSpec((None, d, tf), lambda i, j: (0, 0, j)),
            pl.BlockSpec((None, tf, d), lambda i, j: (0, j, 0)),
            pl.BlockSpec((1, d), lambda i, j: (0, 0)),
        ] + s_in,
        out_specs=[pl.BlockSpec((tm, d), lambda i, j: (i, 0))] + s_out,
        out_shape=[jax.ShapeDtypeStruct((m, d), F32)] + s_shape,
        compiler_params=_cparams(("arbitrary", "arbitrary")),
        name="ffn",
    )(x2, h2, w1, w2, gpost, *side_weights)
    return outs[0], outs[1:]


def kernel(x, g_pre_mix, w_in, ssm_a_re, ssm_a_im, ssm_log_dt, ssm_b_re, ssm_b_im, ssm_c_re, ssm_c_im,
           ssm_d, ssm_w_glu, pool_w, pool_scale, conv_w, conv_b, conv_ln_g, conv_ln_b, gmlp_ln_g,
           gmlp_ln_b, gmlp_ws, gmlp_bs, w_branch, w_o, g_post_mix, g_pre_mlp, w_ff1, w_ff2, g_post_mlp):
    nb, seq, d = x.shape
    m = nb * seq
    depth = w_in.shape[0]
    nchunk = seq // SSM_T
    assert d == D_MODEL and nchunk == 1 << SSM_SCAN_STEPS

    toep, bpow, cpow, lama, lamb = _ssm_prep(ssm_a_re, ssm_a_im, ssm_log_dt, ssm_b_re, ssm_b_im,
                                             ssm_c_re, ssm_c_im)

    w_br2 = w_branch.reshape(depth, N_BRANCH * W_BRANCH, d)
    w_in_b = _cast_bf16(w_in, 0, CAST_ROWS)

    row = lambda a: a.reshape(1, -1)
    x2 = x.reshape(m, d)
    for l in range(depth):
        h, u, y_pool, y_conv, y_gmlp = _proj_mix(
            x2, row(g_pre_mix[l]), w_in_b, 0, pool_w[l].astype(BF16), row(pool_scale[l]),
            jnp.broadcast_to(conv_w[l][:, None, :], (CONV_WIDTH, SUBLANES, W_BRANCH)),
            row(conv_b[l]), row(conv_ln_g[l]), row(conv_ln_b[l]), row(gmlp_ln_g[l]), row(gmlp_ln_b[l]),
            gmlp_ws[l], jnp.transpose(gmlp_bs[l]), seq=seq)
        yraw, (w_br_b, w_o_b) = _ssm_scan(u, toep, bpow, cpow, lama, lamb, l, (w_br2, w_o), nchunk=nchunk)
        y_ssm = _ssm_post(yraw, u, row(ssm_d[l]), ssm_w_glu[l].astype(BF16))
        merged, (w_ff1_b, w_ff2_b) = _merge(h, (y_ssm, y_pool, y_conv, y_gmlp), w_in_b,
                                            w_br_b.reshape(N_BRANCH, W_BRANCH, d), (w_ff1, w_ff2), l)
        x2, h2 = _out_proj(merged, w_o_b, x2, row(g_post_mix[l]), row(g_pre_mlp[l]), 0)
        nxt = (w_in,) if l + 1 < depth else ()
        x2, nxt_b = _ffn(x2, h2, w_ff1_b, w_ff2_b, row(g_post_mlp[l]), nxt, l + 1)
        if nxt:
            (w_in_b,) = nxt_b
    return x2.reshape(nb, seq, d)
```
